```python
import math
import jax
import jax.numpy as jnp
from jax import lax
import numpy as np

D_MODEL = 2048
BATCH = 32
SEQ = 256
DEPTH = 4
DEC_BATCH = 2
DEC_SEQ = 2048
PAST_LEN = 256

GRID_W = 64
F32 = jnp.float32
RMS_EPS = 1e-6
H_A = 8
DK_A = 64
DV_A = 128
MLSTM_CHUNK = 64
H_B = 8
DB = 64
DVB = 2 * DB
ATTN_Q_BLOCK = 128
ROPE_THETA = 10000.0
ROPE_PAIRS = DB // 4
D_INNER = 2 * D_MODEL
P_C = 64
H_C = D_INNER // P_C
G_C = 8
R_C = H_C // G_C
N_C = 128
SSD_CONV = 4
CONV_PAD_L = SSD_CONV // 2
CONV_PAD_R = SSD_CONV - 1 - SSD_CONV // 2
SSD_CHUNK = 128
DT_MIN = 0.001
DT_MAX = 0.1
D_FF = -(-8 * D_MODEL // (3 * 256)) * 256
N_EVEN = (DEPTH + 1) // 2
N_ODD = DEPTH // 2
A_QK = H_A * DK_A
A_V = H_A * DV_A
B_QK = H_B * 2 * DB
B_V = H_B * DVB
E_IN = 2 * A_QK + 2 * A_V + 4 * H_A + 2 * B_QK + B_V
CONV_CH = D_INNER + 2 * G_C * N_C
O_IN = D_INNER + CONV_CH + 2 * H_C

kernel_name = 'hybrid_mlstm_diffattn_ssd_prefix_denoise_step'


def _split(x, sizes):
    idx, acc = [], 0
    for s in sizes[:-1]:
        acc += s
        idx.append(acc)
    return jnp.split(x, idx, axis=-1)


def rms_norm(x, gain):
    xf = x.astype(F32)
    y = xf * lax.rsqrt(jnp.mean(xf * xf, axis=-1, keepdims=True) + RMS_EPS)
    return (y * gain.astype(F32)).astype(x.dtype)


def axial_rope_tables(n_tokens):
    n_rows = n_tokens // GRID_W
    rows, cols = jnp.meshgrid(jnp.arange(n_rows, dtype=F32), jnp.arange(GRID_W, dtype=F32), indexing='ij')
    inv_freq = ROPE_THETA ** (-jnp.arange(ROPE_PAIRS, dtype=F32) / ROPE_PAIRS)
    ang = jnp.stack([rows.reshape(-1, 1) * inv_freq, cols.reshape(-1, 1) * inv_freq], axis=0)
    return jnp.cos(ang), jnp.sin(ang)


def apply_axial_rope(x, cos, sin):
    xf = x.astype(F32)
    parts = []
    for axis in range(2):
        u = xf[..., axis * (DB // 2):(axis + 1) * (DB // 2)]
        u1, u2 = u[..., :ROPE_PAIRS], u[..., ROPE_PAIRS:]
        c_ = cos[axis][None, :, None, None, :]
        s_ = sin[axis][None, :, None, None, :]
        parts += [u1 * c_ - u2 * s_, u2 * c_ + u1 * s_]
    return jnp.concatenate(parts, axis=-1).astype(x.dtype)


def mlstm_scan(q, k, v, li, lf, state):
    b_, h_, L = q.shape[:3]
    nc = L // MLSTM_CHUNK

    def to_chunks(t):
        return jnp.moveaxis(t.reshape(b_, h_, nc, MLSTM_CHUNK, *t.shape[3:]), 2, 0)

    causal = jnp.tril(jnp.ones((MLSTM_CHUNK, MLSTM_CHUNK), dtype=bool))

    def step(carry, xs):
        C, n, m = carry
        qj, kj, vj, lij, lfj = xs
        b = jnp.cumsum(lfj, axis=-1)
        dmat = jnp.where(causal, b[..., :, None] - b[..., None, :] + lij[..., None, :], -jnp.inf)
        inter = b + m[..., None]
        m_row = jnp.maximum(inter, jnp.max(dmat, axis=-1))
        w_inter = jnp.exp(inter - m_row)
        s = jnp.einsum('bhid,bhjd->bhij', qj, kj) * jnp.exp(dmat - m_row[..., None])
        num = w_inter[..., None] * jnp.einsum('bhvd,bhid->bhiv', C, qj) + jnp.einsum('bhij,bhjv->bhiv', s, vj)
        den = w_inter * jnp.einsum('bhd,bhid->bhi', n, qj) + jnp.sum(s, axis=-1)
        h = num / jnp.maximum(jnp.abs(den), jnp.exp(-m_row))[..., None]
        b_end = b[..., -1]
        g = b_end[..., None] - b + lij
        m_new = jnp.maximum(b_end + m, jnp.max(g, axis=-1))
        decay = jnp.exp(b_end + m - m_new)
        wk = jnp.exp(g - m_new[..., None])[..., None] * kj
        C = decay[..., None, None] * C + jnp.einsum('bhjv,bhjd->bhvd', vj, wk)
        n = decay[..., None] * n + jnp.sum(wk, axis=2)
        return (C, n, m_new), h

    state = tuple(s.astype(F32) for s in state)
    final, hs = lax.scan(step, state, tuple(to_chunks(t) for t in (q, k, v, li, lf)))
    return jnp.moveaxis(hs, 0, 2).reshape(b_, h_, L, v.shape[-1]), final


def ssd_scan(x, dt, a, bm, cm, s0):
    b_, L = x.shape[:2]
    nc = L // SSD_CHUNK

    def to_chunks(t):
        return jnp.moveaxis(t.reshape(b_, nc, SSD_CHUNK, *t.shape[2:]), 1, 0)

    causal = jnp.tril(jnp.ones((SSD_CHUNK, SSD_CHUNK), dtype=bool))[None, :, :, None, None]

    def step(S, xs):
        xc, dtc, bc, cc = xs
        acum = jnp.cumsum(dtc * a, axis=1)
        seg = jnp.where(causal, acum[:, :, None] - acum[:, None, :], -jnp.inf)
        cb = jnp.einsum('bign,bjgn->bijg', cc, bc)
        w = cb[..., None] * jnp.exp(seg) * dtc[:, None]
        y = jnp.einsum('bijgr,bjgrp->bigrp', w, xc)
        y = y + jnp.einsum('bign,bgrpn->bigrp', cc, S) * jnp.exp(acum)[..., None]
        to_end = (jnp.exp(acum[:, -1:] - acum) * dtc)[..., None] * xc
        S = S * jnp.exp(acum[:, -1])[..., None, None] + jnp.einsum('bjgrp,bjgn->bgrpn', to_end, bc)
        return S, y

    S, ys = lax.scan(step, s0.astype(F32), tuple(to_chunks(t) for t in (x, dt, bm, cm)))
    return jnp.moveaxis(ys, 0, 1).reshape(x.shape), S


def diff_attend(q, k, v, lam):
    b_, lq = q.shape[:2]
    nb = lq // ATTN_Q_BLOCK
    qb = jnp.moveaxis(q.astype(F32).reshape(b_, nb, ATTN_Q_BLOCK, H_B, 2, DB), 1, 0)
    kf = k.astype(F32)
    vf = v.astype(F32)
    scale = DB ** -0.5

    def one_block(qblk):
        p = jax.nn.softmax(jnp.einsum('bqhcd,bkhcd->bchqk', qblk, kf) * scale, axis=-1)
        w = p[:, 0] - lam * p[:, 1]
        return jnp.einsum('bhqk,bkhv->bqhv', w, vf)

    out = lax.map(one_block, qb)
    return jnp.moveaxis(out, 0, 1).reshape(b_, lq, H_B, DVB)


def even_mixer(h, P, e, layer_idx, ctx):
    b_, L, _ = h.shape
    proj = jnp.einsum('bld,de->ble', h, P['w_in_even'][e])
    aq, ak, av, ao, ag, bq, bk, bv = _split(proj, [A_QK, A_QK, A_V, A_V, 4 * H_A, B_QK, B_QK, B_V])

    def heads(t, d):
        return t.astype(F32).reshape(b_, L, H_A, d).transpose(0, 2, 1, 3)

    q = heads(aq, DK_A)
    k = heads(ak, DK_A) * (DK_A ** -0.5)
    v = heads(av, DV_A)
    g = (ag.astype(F32) + P['b_gate_mlstm'][e].astype(F32)).reshape(b_, L, 4, H_A).transpose(2, 0, 3, 1)
    if ctx is None:
        zero = (jnp.zeros((b_, H_A, DV_A, DK_A), F32), jnp.zeros((b_, H_A, DK_A), F32), jnp.zeros((b_, H_A), F32))
        init_f, init_b = zero, zero
    else:
        init_f, init_b = ctx['mlstm_f'], ctx['mlstm_b']
    rev = lambda t: jnp.flip(t, axis=2)
    h_f, st_f = mlstm_scan(q, k, v, g[0], jax.nn.log_sigmoid(g[2]), init_f)
    h_b, st_b = mlstm_scan(rev(q), rev(k), rev(v), rev(g[1]), rev(jax.nn.log_sigmoid(g[3])), init_b)
    h_a = (h_f + rev(h_b)).transpose(0, 2, 1, 3)
    h_a = rms_norm(h_a, P['mlstm_norm'][e].reshape(H_A, DV_A)).reshape(b_, L, A_V)
    h_a = h_a * jax.nn.sigmoid(ao.astype(F32))

    qd = rms_norm(bq.reshape(b_, L, H_B, 2, DB), P['q_norm'][e])
    kd = rms_norm(bk.reshape(b_, L, H_B, 2, DB), P['k_norm'][e])
    vd = bv.reshape(b_, L, H_B, DVB)
    if ctx is None:
        k_all, v_all = kd, vd
    else:
        cos, sin = axial_rope_tables(L)
        qd = apply_axial_rope(qd, cos, sin)
        k_all = jnp.concatenate([ctx['k'].astype(kd.dtype), apply_axial_rope(kd, cos, sin)], axis=1)
        v_all = jnp.concatenate([ctx['v'].astype(vd.dtype), vd], axis=1)
    lam_init = 0.8 - 0.6 * math.exp(-0.3 * layer_idx)
    lam = (jnp.exp(jnp.sum(P['lambda_q1'][e] * P['lambda_k1'][e]))
           - jnp.exp(jnp.sum(P['lambda_q2'][e] * P['lambda_k2'][e])) + lam_init).astype(F32)
    o = diff_attend(qd, k_all, v_all, lam)
    o = rms_norm(o, P['diff_norm'][e]) * (1.0 - lam_init)

    cat = jnp.concatenate([h_a, o.reshape(b_, L, B_V)], axis=-1).astype(h.dtype)
    out = jnp.einsum('ble,ed->bld', cat, P['w_out_even'][e])
    if ctx is not None:
        return out, None
    new = {'k': kd, 'v': vd,
           'c': jnp.stack([st_f[0], st_b[0]], axis=1),
           'n': jnp.stack([st_f[1], st_b[1]], axis=1),
           'm': jnp.stack([st_f[2], st_b[2]], axis=1)}
    return out, new


def odd_mixer(h, P, o, ctx):
    b_, L, _ = h.shape
    proj = jnp.einsum('bld,de->ble', h, P['w_in_odd'][o])
    z, xbc, dtr = _split(proj, [D_INNER, CONV_CH, 2 * H_C])
    wconv = P['conv_w'][o].astype(xbc.dtype)[:, None, :]
    xbc = lax.conv_general_dilated(xbc, wconv, window_strides=(1,), padding=[(CONV_PAD_L, CONV_PAD_R)],
                                   dimension_numbers=('NWC', 'WIO', 'NWC'), feature_group_count=CONV_CH)
    xbc = jax.nn.silu(xbc.astype(F32) + P['conv_b'][o].astype(F32))
    xs, bm, cm = _split(xbc, [D_INNER, G_C * N_C, G_C * N_C])
    xs = xs.reshape(b_, L, G_C, R_C, P_C)
    bm = bm.reshape(b_, L, G_C, N_C)
    cm = cm.reshape(b_, L, G_C, N_C)
    dt = jax.nn.softplus(dtr.astype(F32).reshape(b_, L, 2, H_C) + P['dt_bias'][o].astype(F32))
    dt = dt.reshape(b_, L, 2, G_C, R_C)
    A = -jnp.exp(P['a_log'][o].astype(F32)).reshape(2, G_C, R_C)
    if ctx is None:
        s0f = jnp.zeros((b_, G_C, R_C, P_C, N_C), F32)
        s0b = s0f
    else:
        s0f = ctx['ssd_f'].reshape(b_, G_C, R_C, P_C, N_C)
        s0b = ctx['ssd_b'].reshape(b_, G_C, R_C, P_C, N_C)
    rev = lambda t: jnp.flip(t, axis=1)
    y_f, s_f = ssd_scan(xs, dt[:, :, 0], A[0], bm, cm, s0f)
    y_b, s_b = ssd_scan(rev(xs), rev(dt[:, :, 1]), A[1], rev(bm), rev(cm), s0b)
    y = y_f + rev(y_b) + P['d_skip'][o].astype(F32).reshape(G_C, R_C, 1) * xs
    y = y.reshape(b_, L, D_INNER) * jax.nn.silu(z.astype(F32))
    y = rms_norm(y, P['ssd_norm'][o]).astype(h.dtype)
    out = jnp.einsum('ble,ed->bld', y, P['w_out_odd'][o])
    if ctx is not None:
        return out, None
    new = {'ssd': jnp.stack([s_f.reshape(b_, H_C, P_C, N_C), s_b.reshape(b_, H_C, P_C, N_C)], axis=1)}
    return out, new


def block(x, cvec, l, P, ctx):
    mod = (jnp.einsum('bd,de->be', jax.nn.silu(cvec), P['w_ada'][l]) + P['b_ada'][l])[:, None, :]
    sh1, sc1, g1, sh2, sc2, g2 = jnp.split(mod, 6, axis=-1)
    h = rms_norm(x, P['norm_mix'][l]) * (1 + sc1) + sh1
    if l % 2 == 0:
        m, new = even_mixer(h, P, l // 2, l, ctx)
    else:
        m, new = odd_mixer(h, P, l // 2, ctx)
    x = x + g1 * m
    h = rms_norm(x, P['norm_ffn'][l]) * (1 + sc2) + sh2
    ff = jax.nn.silu(jnp.einsum('bld,df->blf', h, P['w_gate'][l])) * jnp.einsum('bld,df->blf', h, P['w_up'][l])
    x = x + g2 * jnp.einsum('blf,fd->bld', ff, P['w_down'][l])
    return x, new


def setup_inputs(seed: int = 0) -> dict:
    key = jax.random.key(seed)
    keys = iter(jax.random.split(key, 48))

    def nrm(shape, scale=1.0):
        return jax.random.normal(next(keys), shape, F32) * scale

    def gain(shape):
        return 1.0 + nrm(shape, 0.02)

    u_dt = jax.random.uniform(next(keys), (N_ODD, 2, H_C), F32)
    dt0 = jnp.exp(u_dt * (math.log(DT_MAX) - math.log(DT_MIN)) + math.log(DT_MIN))
    a_init = jax.random.uniform(next(keys), (N_ODD, 2, H_C), F32, 1.0, 16.0)
    f_bias = jnp.tile(jnp.linspace(3.0, 6.0, H_A, dtype=F32), (N_EVEN, 2))
    return {
        'x_prompt': nrm((BATCH, SEQ, D_MODEL)),
        'x_sample': nrm((DEC_BATCH, DEC_SEQ, D_MODEL)),
        'cache_attn_k': nrm((DEC_BATCH, N_EVEN, PAST_LEN, H_B, 2, DB)),
        'cache_attn_v': nrm((DEC_BATCH, N_EVEN, PAST_LEN, H_B, DVB)),
        'state_mlstm_c': nrm((DEC_BATCH, N_EVEN, 2, H_A, DV_A, DK_A)),
        'state_mlstm_n': nrm((DEC_BATCH, N_EVEN, 2, H_A, DK_A)),
        'state_mlstm_m': nrm((DEC_BATCH, N_EVEN, 2, H_A)),
        'state_ssd': nrm((DEC_BATCH, N_ODD, 2, H_C, P_C, N_C), 0.1),
        'c': nrm((DEC_BATCH, D_MODEL)),
        'c_ctx': nrm((D_MODEL,)),
        'w_ada': nrm((DEPTH, D_MODEL, 6 * D_MODEL), 0.5 * D_MODEL ** -0.5),
        'b_ada': nrm((DEPTH, 6 * D_MODEL), 0.02),
        'norm_mix': gain((DEPTH, D_MODEL)),
        'norm_ffn': gain((DEPTH, D_MODEL)),
        'w_gate': nrm((DEPTH, D_MODEL, D_FF), D_MODEL ** -0.5),
        'w_up': nrm((DEPTH, D_MODEL, D_FF), D_MODEL ** -0.5),
        'w_down': nrm((DEPTH, D_FF, D_MODEL), D_FF ** -0.5),
        'w_in_even': nrm((N_EVEN, D_MODEL, E_IN), D_MODEL ** -0.5),
        'b_gate_mlstm': jnp.concatenate([nrm((N_EVEN, 2 * H_A), 0.1), f_bias + nrm((N_EVEN, 2 * H_A), 0.1)], axis=-1),
        'mlstm_norm': gain((N_EVEN, A_V)),
        'q_norm': gain((N_EVEN, DB)),
        'k_norm': gain((N_EVEN, DB)),
        'lambda_q1': nrm((N_EVEN, DB), 0.1),
        'lambda_k1': nrm((N_EVEN, DB), 0.1),
        'lambda_q2': nrm((N_EVEN, DB), 0.1),
        'lambda_k2': nrm((N_EVEN, DB), 0.1),
        'diff_norm': gain((N_EVEN, DVB)),
        'w_out_even': nrm((N_EVEN, A_V + B_V, D_MODEL), (A_V + B_V) ** -0.5),
        'w_in_odd': nrm((N_ODD, D_MODEL, O_IN), D_MODEL ** -0.5),
        'conv_w': nrm((N_ODD, SSD_CONV, CONV_CH), SSD_CONV ** -0.5),
        'conv_b': nrm((N_ODD, CONV_CH), 0.02),
        'dt_bias': dt0 + jnp.log(-jnp.expm1(-dt0)),
        'a_log': jnp.log(a_init),
        'd_skip': gain((N_ODD, H_C)),
        'ssd_norm': gain((N_ODD, D_INNER)),
        'w_out_odd': nrm((N_ODD, D_INNER, D_MODEL), D_INNER ** -0.5),
    }


def reference(x_prompt, x_sample, cache_attn_k, cache_attn_v, state_mlstm_c, state_mlstm_n, state_mlstm_m,
              state_ssd, c, c_ctx, w_ada, b_ada, norm_mix, norm_ffn, w_gate, w_up, w_down, w_in_even,
              b_gate_mlstm, mlstm_norm, q_norm, k_norm, lambda_q1, lambda_k1, lambda_q2, lambda_k2, diff_norm,
              w_out_even, w_in_odd, conv_w, conv_b, dt_bias, a_log, d_skip, ssd_norm, w_out_odd):
    P = {'w_ada': w_ada, 'b_ada': b_ada, 'norm_mix': norm_mix, 'norm_ffn': norm_ffn,
         'w_gate': w_gate, 'w_up': w_up, 'w_down': w_down,
         'w_in_even': w_in_even, 'b_gate_mlstm': b_gate_mlstm, 'mlstm_norm': mlstm_norm,
         'q_norm': q_norm, 'k_norm': k_norm, 'lambda_q1': lambda_q1, 'lambda_k1': lambda_k1,
         'lambda_q2': lambda_q2, 'lambda_k2': lambda_k2, 'diff_norm': diff_norm, 'w_out_even': w_out_even,
         'w_in_odd': w_in_odd, 'conv_w': conv_w, 'conv_b': conv_b, 'dt_bias': dt_bias, 'a_log': a_log,
         'd_skip': d_skip, 'ssd_norm': ssd_norm, 'w_out_odd': w_out_odd}

    y_prompt = x_prompt
    even_new, odd_new = [], []
    for l in range(DEPTH):
        y_prompt, new = block(y_prompt, c_ctx[None, :], l, P, None)
        if l % 2 == 0:
            even_new.append(new)
        else:
            odd_new.append(new)
    new_attn_k = jnp.stack([s['k'] for s in even_new], axis=1)
    new_attn_v = jnp.stack([s['v'] for s in even_new], axis=1)
    new_mlstm_c = jnp.stack([s['c'] for s in even_new], axis=1)
    new_mlstm_n = jnp.stack([s['n'] for s in even_new], axis=1)
    new_mlstm_m = jnp.stack([s['m'] for s in even_new], axis=1)
    new_ssd = jnp.stack([s['ssd'] for s in odd_new], axis=1)

    y_sample = x_sample
    for l in range(DEPTH):
        i = l // 2
        if l % 2 == 0:
            ctx = {'k': cache_attn_k[:, i], 'v': cache_attn_v[:, i],
                   'mlstm_f': (state_mlstm_c[:, i, 0], state_mlstm_n[:, i, 0], state_mlstm_m[:, i, 0]),
                   'mlstm_b': (state_mlstm_c[:, i, 1], state_mlstm_n[:, i, 1], state_mlstm_m[:, i, 1])}
        else:
            ctx = {'ssd_f': state_ssd[:, i, 0], 'ssd_b': state_ssd[:, i, 1]}
        y_sample, _ = block(y_sample, c, l, P, ctx)

    return (y_prompt, y_sample, new_attn_k, new_attn_v, new_mlstm_c, new_mlstm_n, new_mlstm_m, new_ssd)
```

```python
import functools
import math

import jax
import jax.numpy as jnp
import numpy as np
from jax import lax
from jax.experimental import pallas as pl
from jax.experimental.pallas import tpu as pltpu

F32 = jnp.float32
BF16 = jnp.bfloat16

D_MODEL = 2048
DEPTH = 4
GRID_W = 64
RMS_EPS = 1e-6
H_A, DK_A, DV_A, MLSTM_CHUNK = 8, 64, 128, 64
H_B, DB, DVB = 8, 64, 128
ROPE_THETA = 10000.0
ROPE_PAIRS = DB // 4
D_INNER = 2 * D_MODEL
P_C, N_C, G_C = 64, 128, 8
H_C = D_INNER // P_C
R_C = H_C // G_C
SSD_CONV = 4
SSD_CHUNK = 128
D_FF = -(-8 * D_MODEL // (3 * 256)) * 256
A_QK, A_V = H_A * DK_A, H_A * DV_A
B_QK, B_V = H_B * 2 * DB, H_B * DVB
CONV_CH = D_INNER + 2 * G_C * N_C

LANES = 128
VMEM_LIMIT = 56 * 1024 * 1024

TM = 512
ADA_TN = 1024
E_COLS = 6400
E_TN = 1280
O_COLS = 10752
O_TN = 1536
FF_TN = 512
OUT_TN = 512
ATT_TQ = 256

EB_Q, EB_K, EB_V, EB_O = 0, 4, 8, 16
EB_BQ, EB_BK, EB_BV, EB_G = 24, 32, 40, 48
OB_Z, OB_X, OB_B, OB_C, OB_DT = 0, 32, 64, 72, 80


def _cparams(sem):
    return pltpu.CompilerParams(dimension_semantics=sem, vmem_limit_bytes=VMEM_LIMIT)


def _silu(x):
    return x * jax.nn.sigmoid(x)


def _split3(x):
    hi = x.astype(BF16)
    r1 = x - hi.astype(F32)
    mid = r1.astype(BF16)
    lo = (r1 - mid.astype(F32)).astype(BF16)
    return hi, mid, lo


def _dot(a, b):
    return jnp.dot(a, b, preferred_element_type=F32)


def _dot3_l(x, m):
    hi, mid, lo = _split3(x)
    return _dot(hi, m) + _dot(mid, m) + _dot(lo, m)


def _dot3_r(m, x):
    hi, mid, lo = _split3(x)
    return _dot(m, hi) + _dot(m, mid) + _dot(m, lo)


def _row_group(i, tm, seg):
    n_ctx, dec_len = seg
    return jnp.maximum((i * tm - n_ctx) // dec_len + 1, 0)


def _ada_kernel(c_ref, w_ref, b_ref, o_ref):
    s = _silu(c_ref[...]).astype(BF16)
    o_ref[...] = _dot(s, w_ref[...].astype(BF16)) + b_ref[...]


def ada_modulation(cvecs, w_ada, b_ada):
    depth, d, n = w_ada.shape
    return pl.pallas_call(
        _ada_kernel,
        grid=(depth, n // ADA_TN),
        in_specs=[
            pl.BlockSpec((8, d), lambda l, j: (0, 0)),
            pl.BlockSpec((None, d, ADA_TN), lambda l, j: (l, 0, j)),
            pl.BlockSpec((None, 1, ADA_TN), lambda l, j: (l, 0, j)),
        ],
        out_specs=pl.BlockSpec((None, 8, ADA_TN), lambda l, j: (l, 0, j)),
        out_shape=jax.ShapeDtypeStruct((depth, 8, n), F32),
        compiler_params=_cparams(("parallel", "parallel")),
        name="ada_modulation",
    )(cvecs, w_ada, b_ada.reshape(depth, 1, n))


def _norm_mod(x, gain, sc, sh):
    ms = jnp.mean(x * x, axis=-1, keepdims=True)
    y = x * lax.rsqrt(ms + RMS_EPS) * gain
    return y * (1.0 + sc) + sh


def _inproj_kernel(x_ref, gain_ref, sc_ref, sh_ref, w_ref, o_ref, h_ref, *, nblk):
    @pl.when(pl.program_id(1) == 0)
    def _():
        h_ref[...] = _norm_mod(x_ref[...], gain_ref[...], sc_ref[...], sh_ref[...]).astype(BF16)

    acc = _dot(h_ref[...], w_ref[...])
    for k in range(nblk):
        o_ref[k] = acc[:, k * LANES:(k + 1) * LANES]


def in_projection(x, gain, sc, sh, w, tn, seg):
    m, d = x.shape
    n = w.shape[1]
    nblk = tn // LANES
    grp = lambda i, j: (_row_group(i, TM, seg), 0, 0)
    return pl.pallas_call(
        functools.partial(_inproj_kernel, nblk=nblk),
        grid=(m // TM, n // tn),
        in_specs=[
            pl.BlockSpec((TM, d), lambda i, j: (i, 0)),
            pl.BlockSpec((1, d), lambda i, j: (0, 0)),
            pl.BlockSpec((None, 1, d), grp),
            pl.BlockSpec((None, 1, d), grp),
            pl.BlockSpec((d, tn), lambda i, j: (0, j)),
        ],
        out_specs=pl.BlockSpec((nblk, TM, LANES), lambda i, j: (j, i, 0)),
        out_shape=jax.ShapeDtypeStruct((n // LANES, m, LANES), F32),
        scratch_shapes=[pltpu.VMEM((TM, d), BF16)],
        compiler_params=_cparams(("parallel", "arbitrary")),
        name="in_projection",
    )(x, gain, sc, sh, w)


def _ffn_kernel(x_ref, gain_ref, sc_ref, sh_ref, g_ref, wg_ref, wu_ref, wd_ref, o_ref, h_ref, acc_ref):
    f = pl.program_id(1)

    @pl.when(f == 0)
    def _():
        h_ref[...] = _norm_mod(x_ref[...], gain_ref[...], sc_ref[...], sh_ref[...]).astype(BF16)
        acc_ref[...] = jnp.zeros_like(acc_ref)

    h = h_ref[...]
    gate = _dot(h, wg_ref[...])
    up = _dot(h, wu_ref[...])
    ff = (_silu(gate) * up).astype(BF16)
    acc_ref[...] += _dot(ff, wd_ref[...])

    @pl.when(f == pl.num_programs(1) - 1)
    def _():
        o_ref[...] = x_ref[...] + g_ref[...] * acc_ref[...]


def ffn(x, gain, sc, sh, g, wg, wu, wd, seg):
    m, d = x.shape
    dff = wg.shape[1]
    grp = lambda i, f: (_row_group(i, TM, seg), 0, 0)
    return pl.pallas_call(
        _ffn_kernel,
        grid=(m // TM, dff // FF_TN),
        in_specs=[
            pl.BlockSpec((TM, d), lambda i, f: (i, 0)),
            pl.BlockSpec((1, d), lambda i, f: (0, 0)),
            pl.BlockSpec((None, 1, d), grp),
            pl.BlockSpec((None, 1, d), grp),
            pl.BlockSpec((None, 1, d), grp),
            pl.BlockSpec((d, FF_TN), lambda i, f: (0, f)),
            pl.BlockSpec((d, FF_TN), lambda i, f: (0, f)),
            pl.BlockSpec((FF_TN, d), lambda i, f: (f, 0)),
        ],
        out_specs=pl.BlockSpec((TM, d), lambda i, f: (i, 0)),
        out_shape=jax.ShapeDtypeStruct((m, d), F32),
        scratch_shapes=[pltpu.VMEM((TM, d), BF16), pltpu.VMEM((TM, d), F32)],
        compiler_params=_cparams(("parallel", "arbitrary")),
        name="ffn",
    )(x, gain, sc, sh, g, wg, wu, wd)


def _outproj_even_kernel(x_ref, g_ref, a_ref, b_ref, wa_ref, wb_ref, o_ref):
    acc = _dot(a_ref[...], wa_ref[...]) + _dot(b_ref[...], wb_ref[...])
    o_ref[...] = x_ref[...] + g_ref[...] * acc


def out_projection_even(x, g, ha, ob, wa, wb, seg):
    m, d = x.shape
    ka, kb = ha.shape[1], ob.shape[1]
    grp = lambda i, j: (_row_group(i, TM, seg), 0, j)
    return pl.pallas_call(
        _outproj_even_kernel,
        grid=(m // TM, d // OUT_TN),
        in_specs=[
            pl.BlockSpec((TM, OUT_TN), lambda i, j: (i, j)),
            pl.BlockSpec((None, 1, OUT_TN), grp),
            pl.BlockSpec((TM, ka), lambda i, j: (i, 0)),
            pl.BlockSpec((TM, kb), lambda i, j: (i, 0)),
            pl.BlockSpec((ka, OUT_TN), lambda i, j: (0, j)),
            pl.BlockSpec((kb, OUT_TN), lambda i, j: (0, j)),
        ],
        out_specs=pl.BlockSpec((TM, OUT_TN), lambda i, j: (i, j)),
        out_shape=jax.ShapeDtypeStruct((m, d), F32),
        compiler_params=_cparams(("parallel", "arbitrary")),
        name="out_projection_even",
    )(x, g, ha, ob, wa, wb)


def _outproj_odd_kernel(x_ref, g_ref, y_ref, gain_ref, w_ref, o_ref, yn_ref):
    @pl.when(pl.program_id(1) == 0)
    def _():
        y = y_ref[...]
        ms = jnp.mean(y * y, axis=-1, keepdims=True)
        yn_ref[...] = (y * lax.rsqrt(ms + RMS_EPS) * gain_ref[...]).astype(BF16)

    o_ref[...] = x_ref[...] + g_ref[...] * _dot(yn_ref[...], w_ref[...])


def out_projection_odd(x, g, y, gain, w, seg):
    m, d = x.shape
    k = y.shape[1]
    grp = lambda i, j: (_row_group(i, TM, seg), 0, j)
    return pl.pallas_call(
        _outproj_odd_kernel,
        grid=(m // TM, d // OUT_TN),
        in_specs=[
            pl.BlockSpec((TM, OUT_TN), lambda i, j: (i, j)),
            pl.BlockSpec((None, 1, OUT_TN), grp),
            pl.BlockSpec((TM, k), lambda i, j: (i, 0)),
            pl.BlockSpec((1, k), lambda i, j: (0, 0)),
            pl.BlockSpec((k, OUT_TN), lambda i, j: (0, j)),
        ],
        out_specs=pl.BlockSpec((TM, OUT_TN), lambda i, j: (i, j)),
        out_shape=jax.ShapeDtypeStruct((m, d), F32),
        scratch_shapes=[pltpu.VMEM((TM, k), BF16)],
        compiler_params=_cparams(("parallel", "arbitrary")),
        name="out_projection_odd",
    )(x, g, y, gain, w)


def _lane_iota(shape):
    return lax.broadcasted_iota(jnp.int32, shape, len(shape) - 1)


def _half_rms_norm(x, gain):
    lo = _lane_iota(x.shape) < DB
    x2 = x * x
    s0 = jnp.sum(jnp.where(lo, x2, 0.0), axis=-1, keepdims=True)
    s1 = jnp.sum(jnp.where(lo, 0.0, x2), axis=-1, keepdims=True)
    r = jnp.where(lo, lax.rsqrt(s0 * (1.0 / DB) + RMS_EPS), lax.rsqrt(s1 * (1.0 / DB) + RMS_EPS))
    return x * r * gain


def _rope(x, cos, sin_signed):
    first = (_lane_iota(x.shape) % 32) < ROPE_PAIRS
    partner = jnp.where(first, pltpu.roll(x, LANES - ROPE_PAIRS, 1), pltpu.roll(x, ROPE_PAIRS, 1))
    return x * cos + partner * sin_signed


def _softmax_rows(s):
    p = jnp.exp(s - jnp.max(s, axis=-1, keepdims=True))
    return p * (1.0 / jnp.sum(p, axis=-1, keepdims=True))


def _diff_attn_core(q, ks_ref, vs_ref, lam, dgain, lam_init):
    lo = _lane_iota(q.shape) < DB
    qs = q * (DB ** -0.5)
    q1 = jnp.where(lo, qs, 0.0).astype(BF16)
    q2 = jnp.where(lo, 0.0, qs).astype(BF16)
    kk = ks_ref[...]
    nt = (((1,), (1,)), ((), ()))
    s1 = lax.dot_general(q1, kk, nt, preferred_element_type=F32)
    s2 = lax.dot_general(q2, kk, nt, preferred_element_type=F32)
    w = _softmax_rows(s1) - lam * _softmax_rows(s2)
    o = _dot(w.astype(BF16), vs_ref[...])
    ms = jnp.mean(o * o, axis=-1, keepdims=True)
    return o * lax.rsqrt(ms + RMS_EPS) * dgain * (1.0 - lam_init)


def _lambda_value(lam_ref, lam_init):
    lp = lam_ref[...]
    a = jnp.sum(lp[0:1] * lp[1:2], axis=-1, keepdims=True)
    b = jnp.sum(lp[2:3] * lp[3:4], axis=-1, keepdims=True)
    return jnp.exp(a) - jnp.exp(b) + lam_init


def _attn_ctx_kernel(lam_ref, q_ref, k_ref, v_ref, qg_ref, kg_ref, dg_ref, o_ref, kd_ref, vd_ref, ks_ref, vs_ref,
                     *, lam_init):
    kd = _half_rms_norm(k_ref[...], kg_ref[...])
    kd_ref[...] = kd
    ks_ref[...] = kd.astype(BF16)
    v = v_ref[...]
    vd_ref[...] = v
    vs_ref[...] = v.astype(BF16)
    q = _half_rms_norm(q_ref[...], qg_ref[...])
    lam = _lambda_value(lam_ref, lam_init)
    o_ref[...] = _diff_attn_core(q, ks_ref, vs_ref, lam, dg_ref[...], lam_init).astype(BF16)


def diff_attention_ctx(proj, lam_p, qg, kg, dg, nb, seq, lam_init):
    m = nb * seq
    vec = pl.BlockSpec((1, LANES), lambda b, h: (0, 0))
    slab = lambda base: pl.BlockSpec((None, seq, LANES), lambda b, h: (base + h, b, 0))
    outb = pl.BlockSpec((seq, LANES), lambda b, h: (b, h))
    return pl.pallas_call(
        functools.partial(_attn_ctx_kernel, lam_init=lam_init),
        grid=(nb, H_B),
        in_specs=[pl.BlockSpec((4, DB), lambda b, h: (0, 0)), slab(EB_BQ), slab(EB_BK), slab(EB_BV), vec, vec, vec],
        out_specs=[outb, outb, outb],
        out_shape=[jax.ShapeDtypeStruct((m, B_V), BF16), jax.ShapeDtypeStruct((m, B_QK), F32),
                   jax.ShapeDtypeStruct((m, B_V), F32)],
        scratch_shapes=[pltpu.VMEM((seq, LANES), BF16), pltpu.VMEM((seq, LANES), BF16)],
        compiler_params=_cparams(("parallel", "parallel")),
        name="diff_attention_ctx",
    )(lam_p, proj, proj, proj, qg, kg, dg)


def _attn_dec_kernel(lam_ref, q_ref, k_ref, v_ref, kc_ref, vc_ref, qg_ref, kg_ref, dg_ref, cq_ref, sq_ref, ck_ref,
                     sk_ref, o_ref, ks_ref, vs_ref, *, lam_init, past):
    @pl.when(pl.program_id(2) == 0)
    def _():
        ks_ref[0:past, :] = kc_ref[...].astype(BF16)
        vs_ref[0:past, :] = vc_ref[...].astype(BF16)
        kd = _rope(_half_rms_norm(k_ref[...], kg_ref[...]), ck_ref[...], sk_ref[...])
        ks_ref[past:, :] = kd.astype(BF16)
        vs_ref[past:, :] = v_ref[...].astype(BF16)

    q = _rope(_half_rms_norm(q_ref[...], qg_ref[...]), cq_ref[...], sq_ref[...])
    lam = _lambda_value(lam_ref, lam_init)
    o_ref[...] = _diff_attn_core(q, ks_ref, vs_ref, lam, dg_ref[...], lam_init).astype(BF16)


def diff_attention_dec(proj, cache_k, cache_v, lam_p, qg, kg, dg, cos, sin, row0, nb, seq, lam_init):
    past = cache_k.shape[1]
    nq = seq // ATT_TQ
    rq, rk = row0 // ATT_TQ, row0 // seq
    vec = pl.BlockSpec((1, LANES), lambda b, h, i: (0, 0))
    kslab = lambda base: pl.BlockSpec((None, seq, LANES), lambda b, h, i: (base + h, rk + b, 0))
    cslab = pl.BlockSpec((None, past, LANES), lambda b, h, i: (b, 0, h))
    tq = pl.BlockSpec((ATT_TQ, LANES), lambda b, h, i: (i, 0))
    tk = pl.BlockSpec((seq, LANES), lambda b, h, i: (0, 0))
    return pl.pallas_call(
        functools.partial(_attn_dec_kernel, lam_init=lam_init, past=past),
        grid=(nb, H_B, nq),
        in_specs=[pl.BlockSpec((4, DB), lambda b, h, i: (0, 0)),
                  pl.BlockSpec((None, ATT_TQ, LANES), lambda b, h, i: (EB_BQ + h, rq + b * nq + i, 0)),
                  kslab(EB_BK), kslab(EB_BV), cslab, cslab, vec, vec, vec, tq, tq, tk, tk],
        out_specs=pl.BlockSpec((ATT_TQ, LANES), lambda b, h, i: (b * nq + i, h)),
        out_shape=jax.ShapeDtypeStruct((nb * seq, B_V), BF16),
        scratch_shapes=[pltpu.VMEM((past + seq, LANES), BF16), pltpu.VMEM((past + seq, LANES), BF16)],
        compiler_params=_cparams(("parallel", "parallel", "arbitrary")),
        name="diff_attention_dec",
    )(lam_p, proj, proj, proj, cache_k, cache_v, qg, kg, dg, cos, sin, cos, sin)


def rope_tables(n_tokens):
    n_rows = n_tokens // GRID_W
    rows, cols = jnp.meshgrid(jnp.arange(n_rows, dtype=F32), jnp.arange(GRID_W, dtype=F32), indexing='ij')
    inv_freq = ROPE_THETA ** (-jnp.arange(ROPE_PAIRS, dtype=F32) / ROPE_PAIRS)
    ang = jnp.stack([rows.reshape(-1, 1) * inv_freq, cols.reshape(-1, 1) * inv_freq], axis=0)
    cos, sin = jnp.cos(ang), jnp.sin(ang)
    cos_half = jnp.concatenate([cos[0], cos[0], cos[1], cos[1]], axis=-1)
    sin_half = jnp.concatenate([-sin[0], sin[0], -sin[1], sin[1]], axis=-1)
    return jnp.tile(cos_half, (1, 2)), jnp.tile(sin_half, (1, 2))


def _tri(n, lower):
    r = lax.broadcasted_iota(jnp.int32, (n, n), 0)
    c = lax.broadcasted_iota(jnp.int32, (n, n), 1)
    return jnp.where((r >= c) if lower else (r <= c), 1.0, 0.0).astype(BF16)


def _mlstm_direction(d, rows, q_ref, k_ref, v_ref, gcol, grow, qk, c_ref, n_prev, m_prev, tri_l, tri_u):
    lc = MLSTM_CHUNK
    ri = lax.broadcasted_iota(jnp.int32, (lc, lc), 0)
    ci = lax.broadcasted_iota(jnp.int32, (lc, lc), 1)
    mask = (ci <= ri) if d == 0 else (ci >= ri)
    lane = _lane_iota((lc, LANES))
    lo = lane < DK_A
    fcol = jax.nn.log_sigmoid(gcol)
    frow = jax.nn.log_sigmoid(grow)
    bcol_all = _dot3_r(tri_l if d == 0 else tri_u, fcol)
    brow_all = _dot3_l(frow, tri_u if d == 0 else tri_l)
    q = q_ref[rows, :]
    k = k_ref[rows, :] * (DK_A ** -0.5)
    cst = c_ref[d]
    qn = q * n_prev
    hs, m_news, e_cols, decays = [], [], [], []
    for j in range(2):
        li_lane, lf_lane = 8 * d + j, 16 + 8 * d + j
        b_col = bcol_all[:, lf_lane:lf_lane + 1]
        b_row = brow_all[lf_lane:lf_lane + 1, :]
        li_col = gcol[:, li_lane:li_lane + 1]
        li_row = grow[li_lane:li_lane + 1, :]
        m_p = m_prev[j]
        dmat = jnp.where(mask, b_col - b_row + li_row, -jnp.inf)
        inter = b_col + m_p
        m_row = jnp.maximum(inter, jnp.max(dmat, axis=-1, keepdims=True))
        w_inter = jnp.exp(inter - m_row)
        s = qk[j] * jnp.exp(dmat - m_row)
        qm = jnp.where(lo, q, 0.0) if j == 0 else jnp.where(lo, 0.0, q)
        vj = v_ref[j, rows, :].astype(BF16)
        num = w_inter * _dot(qm.astype(BF16), cst.astype(BF16)) + _dot(s.astype(BF16), vj)
        qn_j = jnp.sum(jnp.where(lo, qn, 0.0) if j == 0 else jnp.where(lo, 0.0, qn), axis=-1, keepdims=True)
        den = w_inter * qn_j + jnp.sum(s, axis=-1, keepdims=True)
        hs.append(num * (1.0 / jnp.maximum(jnp.abs(den), jnp.exp(-m_row))))
        b_end = b_col[lc - 1:lc, :] if d == 0 else b_col[0:1, :]
        g_col = b_end - b_col + li_col
        m_new = jnp.maximum(b_end + m_p, jnp.max(g_col, axis=0, keepdims=True))
        decays.append(jnp.exp(b_end + m_p - m_new))
        e_cols.append(jnp.exp(g_col - m_new))
        m_news.append(m_new)
    wk = k * jnp.where(lo, e_cols[0], e_cols[1])
    wkb = wk.astype(BF16)
    tn = (((0,), (0,)), ((), ()))
    u0 = lax.dot_general(wkb, v_ref[0, rows, :].astype(BF16), tn, preferred_element_type=F32)
    u1 = lax.dot_general(wkb, v_ref[1, rows, :].astype(BF16), tn, preferred_element_type=F32)
    top = lax.broadcasted_iota(jnp.int32, (LANES, LANES), 0) < DK_A
    c_ref[d] = jnp.where(top, decays[0], decays[1]) * cst + jnp.where(top, u0, u1)
    lane1 = _lane_iota((1, LANES)) < DK_A
    n_new = jnp.where(lane1, decays[0], decays[1]) * n_prev + jnp.sum(wk, axis=0, keepdims=True)
    return jnp.concatenate(hs, axis=-1), n_new, m_news


def _mlstm_kernel(q_ref, k_ref, v_ref, ao_ref, g_ref, bias_ref, gain_ref, c0_ref, n0_ref, m0_ref,
                  h_ref, cf_ref, nf_ref, mf_ref, gc_ref, gt_ref, hf_ref, hb_ref, *, seq):
    lc = MLSTM_CHUNK
    nc = seq // lc
    pair = pl.program_id(1)
    shift = (LANES - 2 * pair) % LANES
    bias = pltpu.roll(jnp.broadcast_to(bias_ref[...], (8, LANES)), shift, 1)[0:1]
    for blk in range(seq // LANES):
        r = slice(blk * LANES, (blk + 1) * LANES)
        gs = pltpu.roll(g_ref[r, :], shift, 1) + bias
        gc_ref[r, :] = gs
        gst = gs.T
        for half in range(LANES // lc):
            gt_ref[blk * (LANES // lc) + half] = gst[0:32, half * lc:(half + 1) * lc]
    cf_ref[...] = c0_ref[...]
    tri_l, tri_u = _tri(lc, True), _tri(lc, False)
    nt = (((1,), (1,)), ((), ()))
    lo = _lane_iota((lc, LANES)) < DK_A

    def chunk_qk(rows):
        q = q_ref[rows, :]
        kb = (k_ref[rows, :] * (DK_A ** -0.5)).astype(BF16)
        return [lax.dot_general(jnp.where(lo, q, 0.0).astype(BF16), kb, nt, preferred_element_type=F32),
                lax.dot_general(jnp.where(lo, 0.0, q).astype(BF16), kb, nt, preferred_element_type=F32)]

    def body(c, carry):
        n_f, m_f0, m_f1, n_b, m_b0, m_b1 = carry
        rf = pl.ds(pl.multiple_of(c * lc, lc), lc)
        rb = pl.ds(pl.multiple_of((nc - 1 - c) * lc, lc), lc)
        h_f, n_f, m_f = _mlstm_direction(0, rf, q_ref, k_ref, v_ref, gc_ref[rf, :], gt_ref[c], chunk_qk(rf),
                                         cf_ref, n_f, (m_f0, m_f1), tri_l, tri_u)
        hf_ref[rf, :] = h_f
        h_b, n_b, m_b = _mlstm_direction(1, rb, q_ref, k_ref, v_ref, gc_ref[rb, :], gt_ref[nc - 1 - c], chunk_qk(rb),
                                         cf_ref, n_b, (m_b0, m_b1), tri_l, tri_u)
        hb_ref[rb, :] = h_b
        return n_f, m_f[0], m_f[1], n_b, m_b[0], m_b[1]

    m0 = m0_ref[...]
    init = (n0_ref[0], m0[0][:, 0:1], m0[0][:, DK_A:DK_A + 1], n0_ref[1], m0[1][:, 0:1], m0[1][:, DK_A:DK_A + 1])
    n_f, m_f0, m_f1, n_b, m_b0, m_b1 = lax.fori_loop(0, nc, body, init)
    nf_ref[0] = n_f
    nf_ref[1] = n_b
    lane1 = _lane_iota((1, LANES)) < DK_A
    mf_ref[0] = jnp.where(lane1, m_f0, m_f1)
    mf_ref[1] = jnp.where(lane1, m_b0, m_b1)

    gain = gain_ref[...]
    for blk in range(seq // LANES):
        r = slice(blk * LANES, (blk + 1) * LANES)
        outs = []
        for j in range(2):
            cs = slice(j * DV_A, (j + 1) * DV_A)
            hh = hf_ref[r, cs] + hb_ref[r, cs]
            ms = jnp.mean(hh * hh, axis=-1, keepdims=True)
            outs.append(hh * lax.rsqrt(ms + RMS_EPS) * gain[:, cs] * jax.nn.sigmoid(ao_ref[j, r, :]))
        h_ref[r, :] = jnp.concatenate(outs, axis=-1).astype(BF16)


def mlstm(proj, bias, gain, c0, n0, m0, row0, nb, seq):
    rb = row0 // seq
    npair = H_A // 2
    slab = lambda base: pl.BlockSpec((None, seq, LANES), lambda b, p: (base + p, rb + b, 0))
    slab2 = lambda base: pl.BlockSpec((2, seq, LANES), lambda b, p: (base // 2 + p, rb + b, 0))
    st_c = pl.BlockSpec((None, None, 2, LANES, LANES), lambda b, p: (b, p, 0, 0, 0))
    st_v = pl.BlockSpec((None, None, 2, 1, LANES), lambda b, p: (b, p, 0, 0, 0))
    return pl.pallas_call(
        functools.partial(_mlstm_kernel, seq=seq),
        grid=(nb, npair),
        in_specs=[slab(EB_Q), slab(EB_K), slab2(EB_V), slab2(EB_O),
                  pl.BlockSpec((None, seq, LANES), lambda b, p: (EB_G, rb + b, 0)),
                  pl.BlockSpec((1, LANES), lambda b, p: (0, 0)),
                  pl.BlockSpec((1, 2 * DV_A), lambda b, p: (0, p)),
                  st_c, st_v, st_v],
        out_specs=[pl.BlockSpec((seq, 2 * DV_A), lambda b, p: (b, p)), st_c, st_v, st_v],
        out_shape=[jax.ShapeDtypeStruct((nb * seq, A_V), BF16),
                   jax.ShapeDtypeStruct((nb, npair, 2, LANES, LANES), F32),
                   jax.ShapeDtypeStruct((nb, npair, 2, 1, LANES), F32),
                   jax.ShapeDtypeStruct((nb, npair, 2, 1, LANES), F32)],
        scratch_shapes=[pltpu.VMEM((seq, LANES), F32), pltpu.VMEM((seq // MLSTM_CHUNK, 32, MLSTM_CHUNK), F32),
                        pltpu.VMEM((seq, 2 * DV_A), F32),
                        pltpu.VMEM((seq, 2 * DV_A), F32)],
        compiler_params=_cparams(("parallel", "parallel")),
        name="mlstm",
    )(proj, proj, proj, proj, proj, bias, gain, c0, n0, m0)


def _softplus(x):
    return jnp.maximum(x, 0.0) + jnp.log1p(jnp.exp(-jnp.abs(x)))


def _conv_silu(x, w, b):
    n = x.shape[0]
    t = lax.broadcasted_iota(jnp.int32, x.shape, 0)
    y = w[2:3] * x
    y = y + w[0:1] * jnp.where(t >= 2, pltpu.roll(x, 2, 0), 0.0)
    y = y + w[1:2] * jnp.where(t >= 1, pltpu.roll(x, 1, 0), 0.0)
    y = y + w[3:4] * jnp.where(t <= n - 2, pltpu.roll(x, n - 1, 0), 0.0)
    return _silu(y + b)


def _ssd_kernel(xs_ref, bm_ref, cm_ref, z_ref, dtr_ref, wx_ref, wb_ref, wc_ref, bx_ref, bb_ref, bc_ref,
                dtb_ref, alog_ref, dsk_ref, *rest, seq, has_init):
    if has_init:
        sf0_ref, sb0_ref = rest[:2]
        rest = rest[2:]
    y_ref, sf_ref, sb_ref, xa_ref, ba_ref, ca_ref, dt_ref, yb_ref = rest
    lc = SSD_CHUNK
    nc = seq // lc
    nslab = R_C * P_C // LANES
    grp = pl.program_id(1)
    lane = _lane_iota((lc, LANES))
    lane1 = _lane_iota((1, LANES))
    lo = lane < P_C
    lo1 = lane1 < P_C

    for s in range(nslab):
        cs = slice(s * LANES, (s + 1) * LANES)
        xa_ref[:, cs] = _conv_silu(xs_ref[s], wx_ref[:, cs], bx_ref[:, cs])
    ba_ref[...] = _conv_silu(bm_ref[...], wb_ref[...], bb_ref[...])
    ca_ref[...] = _conv_silu(cm_ref[...], wc_ref[...], bc_ref[...])

    dtr = dtr_ref[...]
    lane_l = _lane_iota(dtr.shape)
    fwd = pltpu.roll(dtr, (LANES - R_C * grp) % LANES, 1)
    bwd = pltpu.roll(dtr, (LANES - H_C + R_C - R_C * grp) % LANES, 1)
    dt_all = _softplus(jnp.where(lane_l < R_C, fwd, bwd) + dtb_ref[...])
    dt_ref[...] = jnp.where(lane_l < 2 * R_C, dt_all, 0.0)
    a_neg = -jnp.exp(alog_ref[...])

    tri_l, tri_u = _tri(lc, True), _tri(lc, False)
    ri = lax.broadcasted_iota(jnp.int32, (lc, lc), 0)
    ci = lax.broadcasted_iota(jnp.int32, (lc, lc), 1)
    causal, anti = ci <= ri, ci >= ri
    nt = (((1,), (1,)), ((), ()))

    def head_mask(j, v):
        return (jnp.where(lo, v, 0.0) if j == 0 else jnp.where(lo, 0.0, v)).astype(BF16)

    sf_ref[...] = sf0_ref[...] if has_init else jnp.zeros(sf_ref.shape, F32)

    def fwd_body(c, carry):
        rows = pl.ds(pl.multiple_of(c * lc, lc), lc)
        dt = dt_ref[rows, :]
        acum = _dot3_r(tri_l, dt * a_neg)
        a_end = acum[lc - 1:lc, :]
        te_t = (jnp.exp(a_end - acum) * dt).T
        ea = jnp.exp(acum)
        dec = jnp.exp(a_end)
        cc = ca_ref[rows, :]
        bc_t = ba_ref[rows, :].T
        for s in range(nslab):
            cs = slice(s * LANES, (s + 1) * LANES)
            st = sf_ref[:, cs]
            xs = xa_ref[rows, cs]
            acc_y = jnp.zeros((lc, LANES), F32)
            acc_s = jnp.zeros((N_C, LANES), F32)
            for j in range(2):
                r = 2 * s + j
                acc_y = acc_y + _dot((ea[:, r:r + 1] * cc).astype(BF16), head_mask(j, st))
                acc_s = acc_s + _dot((bc_t * te_t[r:r + 1, :]).astype(BF16), head_mask(j, xs))
            yb_ref[rows, cs] = acc_y
            sf_ref[:, cs] = jnp.where(lo1, dec[:, 2 * s:2 * s + 1], dec[:, 2 * s + 1:2 * s + 2]) * st + acc_s
        return carry

    lax.fori_loop(0, nc, fwd_body, 0)

    sb_ref[...] = sb0_ref[...] if has_init else jnp.zeros(sb_ref.shape, F32)

    def bwd_body(i, carry):
        rows = pl.ds(pl.multiple_of((nc - 1 - i) * lc, lc), lc)
        dt = dt_ref[rows, :]
        da = dt * a_neg
        acum_b = _dot3_r(tri_u, da)
        acum = jnp.where(lane < R_C, _dot3_r(tri_l, da), acum_b)
        acum_t = acum.T
        dt_t = dt.T
        a_end = acum_b[0:1, :]
        te_t = (jnp.exp(a_end - acum_b) * dt).T
        ea = jnp.exp(acum)
        dec = jnp.exp(a_end)
        cc = ca_ref[rows, :]
        bc = ba_ref[rows, :]
        bc_t = bc.T
        cb = lax.dot_general(cc.astype(BF16), bc.astype(BF16), nt, preferred_element_type=F32)
        for s in range(nslab):
            cs = slice(s * LANES, (s + 1) * LANES)
            st = sb_ref[:, cs]
            xs = xa_ref[rows, cs]
            acc_y = yb_ref[rows, cs] + dsk_ref[:, cs] * xs
            acc_s = jnp.zeros((N_C, LANES), F32)
            for j in range(2):
                r = 2 * s + j
                rb = R_C + r
                wf = jnp.exp(jnp.where(causal, acum[:, r:r + 1] - acum_t[r:r + 1, :], -jnp.inf)) * dt_t[r:r + 1, :]
                wb = jnp.exp(jnp.where(anti, acum[:, rb:rb + 1] - acum_t[rb:rb + 1, :], -jnp.inf)) * dt_t[rb:rb + 1, :]
                xm = head_mask(j, xs)
                acc_y = acc_y + _dot((cb * (wf + wb)).astype(BF16), xm)
                acc_y = acc_y + _dot((ea[:, rb:rb + 1] * cc).astype(BF16), head_mask(j, st))
                acc_s = acc_s + _dot((bc_t * te_t[rb:rb + 1, :]).astype(BF16), xm)
            y_ref[rows, cs] = acc_y * _silu(z_ref[s, rows, :])
            dslab = jnp.where(lo1, dec[:, R_C + 2 * s:R_C + 2 * s + 1], dec[:, R_C + 2 * s + 1:R_C + 2 * s + 2])
            sb_ref[:, cs] = dslab * st + acc_s
        return carry

    lax.fori_loop(0, nc, bwd_body, 0)


def ssd(proj, conv_w, conv_b, dtb, alog, dsk, s0, row0, nb, seq):
    rb = row0 // seq
    gw = R_C * P_C
    nslab = gw // LANES
    slab = lambda base: pl.BlockSpec((None, seq, LANES), lambda b, g: (base + g, rb + b, 0))
    slab4 = lambda base: pl.BlockSpec((nslab, seq, LANES), lambda b, g: (base // nslab + g, rb + b, 0))
    xo, bo, co = 0, D_INNER // LANES, (D_INNER + G_C * N_C) // LANES
    cw = lambda rws, width, off: pl.BlockSpec((rws, width), lambda b, g: (0, off * LANES // width + g))
    vec = pl.BlockSpec((None, 1, LANES), lambda b, g: (g, 0, 0))
    st_in = lambda d: pl.BlockSpec((None, None, None, N_C, gw), lambda b, g: (b, d, g, 0, 0))
    st_out = pl.BlockSpec((None, None, N_C, gw), lambda b, g: (b, g, 0, 0))
    has_init = s0 is not None
    return pl.pallas_call(
        functools.partial(_ssd_kernel, seq=seq, has_init=has_init),
        grid=(nb, G_C),
        in_specs=[slab4(OB_X), slab(OB_B), slab(OB_C), slab4(OB_Z),
                  pl.BlockSpec((None, seq, LANES), lambda b, g: (OB_DT, rb + b, 0)),
                  cw(SSD_CONV, gw, xo), cw(SSD_CONV, LANES, bo), cw(SSD_CONV, LANES, co),
                  cw(1, gw, xo), cw(1, LANES, bo), cw(1, LANES, co),
                  vec, vec, pl.BlockSpec((1, gw), lambda b, g: (0, g))] + ([st_in(0), st_in(1)] if has_init else []),
        out_specs=[pl.BlockSpec((seq, gw), lambda b, g: (b, g)), st_out, st_out],
        out_shape=[jax.ShapeDtypeStruct((nb * seq, D_INNER), F32),
                   jax.ShapeDtypeStruct((nb, G_C, N_C, gw), F32),
                   jax.ShapeDtypeStruct((nb, G_C, N_C, gw), F32)],
        scratch_shapes=[pltpu.VMEM((seq, gw), F32), pltpu.VMEM((seq, LANES), F32), pltpu.VMEM((seq, LANES), F32),
                        pltpu.VMEM((seq, LANES), F32), pltpu.VMEM((seq, gw), F32)],
        compiler_params=_cparams(("parallel", "parallel")),
        name="ssd",
    )(proj, proj, proj, proj, proj, conv_w, conv_w, conv_w, conv_b, conv_b, conv_b, dtb, alog, dsk,
      *((s0, s0) if has_init else ()))


def _pair_layout_c(c):
    nb = c.shape[0]
    c = c.transpose(0, 2, 1, 4, 3).reshape(nb, H_A // 2, 2, 2, DK_A, DV_A)
    return c.transpose(0, 1, 3, 2, 4, 5).reshape(nb, H_A // 2, 2, 2 * DK_A, DV_A)


def _pair_layout_c_inv(c):
    nb = c.shape[0]
    c = c.reshape(nb, H_A // 2, 2, 2, DK_A, DV_A).transpose(0, 2, 1, 3, 5, 4)
    return c.reshape(nb, 2, H_A, DV_A, DK_A)


def _pair_layout_n(n):
    nb = n.shape[0]
    return n.reshape(nb, 2, H_A // 2, 2 * DK_A).transpose(0, 2, 1, 3).reshape(nb, H_A // 2, 2, 1, 2 * DK_A)


def _pair_layout_n_inv(n):
    nb = n.shape[0]
    return n.reshape(nb, H_A // 2, 2, 2 * DK_A).transpose(0, 2, 1, 3).reshape(nb, 2, H_A, DK_A)


def _pair_layout_m(m):
    return _pair_layout_n(jnp.repeat(m[..., None], DK_A, axis=-1))


def _pair_layout_m_inv(m):
    return _pair_layout_n_inv(m)[..., 0]


def _ssd_state_layout(s):
    nb = s.shape[0]
    return s.reshape(nb, 2, G_C, R_C, P_C, N_C).transpose(0, 1, 2, 5, 3, 4).reshape(nb, 2, G_C, N_C, R_C * P_C)


def _ssd_state_layout_inv(s):
    nb = s.shape[0]
    return s.reshape(nb, G_C, N_C, R_C, P_C).transpose(0, 1, 3, 4, 2).reshape(nb, H_C, P_C, N_C)


def _group_compact(v):
    t = v.reshape(2, G_C, R_C).transpose(1, 0, 2).reshape(G_C, 1, 2 * R_C)
    return jnp.pad(t, ((0, 0), (0, 0), (0, LANES - 2 * R_C)))


def _pad_cols(w, n):
    return jnp.pad(w, ((0, 0), (0, n - w.shape[1])))


def kernel(x_prompt, x_sample, cache_attn_k, cache_attn_v, state_mlstm_c, state_mlstm_n, state_mlstm_m, state_ssd,
           c, c_ctx, w_ada, b_ada, norm_mix, norm_ffn, w_gate, w_up, w_down, w_in_even, b_gate_mlstm, mlstm_norm,
           q_norm, k_norm, lambda_q1, lambda_k1, lambda_q2, lambda_k2, diff_norm, w_out_even, w_in_odd, conv_w,
           conv_b, dt_bias, a_log, d_skip, ssd_norm, w_out_odd):
    nb_c, seq_c, d = x_prompt.shape
    nb_d, seq_d, _ = x_sample.shape
    n_ctx, n_dec = nb_c * seq_c, nb_d * seq_d
    seg = (n_ctx, seq_d)
    depth = w_ada.shape[0]

    x = jnp.concatenate([x_prompt.reshape(n_ctx, d), x_sample.reshape(n_dec, d)], axis=0)
    cvecs = jnp.concatenate([c_ctx[None, :], c, jnp.zeros((8 - 1 - nb_d, d), F32)], axis=0)
    mods = ada_modulation(cvecs, w_ada, b_ada)

    def mod(l, k):
        return mods[l, :, k * d:(k + 1) * d].reshape(8, 1, d)

    cos, sin = rope_tables(seq_d)
    new_k, new_v, new_c, new_n, new_m, new_s = [], [], [], [], [], []
    for l in range(depth):
        i = l // 2
        sh1, sc1, g1, sh2, sc2, g2 = (mod(l, k) for k in range(6))
        if l % 2 == 0:
            w = w_in_even[i]
            gate0 = 2 * A_QK + 2 * A_V
            w = jnp.concatenate([w[:, :gate0], w[:, gate0 + 4 * H_A:], w[:, gate0:gate0 + 4 * H_A]], axis=1)
            proj = in_projection(x, norm_mix[l][None], sc1, sh1, _pad_cols(w, E_COLS).astype(BF16), E_TN, seg)
            bias = jnp.pad(b_gate_mlstm[i][None], ((0, 0), (0, LANES - 4 * H_A)))
            gain_a = mlstm_norm[i][None]
            zc = jnp.zeros((nb_c, H_A // 2, 2, LANES, LANES), F32)
            zv = jnp.zeros((nb_c, H_A // 2, 2, 1, LANES), F32)
            h_c, cf, nf, mf = mlstm(proj, bias, gain_a, zc, zv, zv, 0, nb_c, seq_c)
            h_d, _, _, _ = mlstm(proj, bias, gain_a, _pair_layout_c(state_mlstm_c[:, i]),
                                 _pair_layout_n(state_mlstm_n[:, i]), _pair_layout_m(state_mlstm_m[:, i]),
                                 n_ctx, nb_d, seq_d)
            new_c.append(_pair_layout_c_inv(cf))
            new_n.append(_pair_layout_n_inv(nf[:, :, :, 0]))
            new_m.append(_pair_layout_m_inv(mf[:, :, :, 0]))

            lam_init = 0.8 - 0.6 * math.exp(-0.3 * l)
            lam_p = jnp.stack([lambda_q1[i], lambda_k1[i], lambda_q2[i], lambda_k2[i]], axis=0)
            qg = jnp.tile(q_norm[i][None], (1, 2))
            kg = jnp.tile(k_norm[i][None], (1, 2))
            dg = diff_norm[i][None]
            o_c, kd, vd = diff_attention_ctx(proj, lam_p, qg, kg, dg, nb_c, seq_c, lam_init)
            past = cache_attn_k.shape[2]
            o_d = diff_attention_dec(proj, cache_attn_k[:, i].reshape(nb_d, past, B_QK),
                                     cache_attn_v[:, i].reshape(nb_d, past, B_V), lam_p, qg, kg, dg, cos, sin,
                                     n_ctx, nb_d, seq_d, lam_init)
            new_k.append(kd.reshape(nb_c, seq_c, H_B, 2, DB))
            new_v.append(vd.reshape(nb_c, seq_c, H_B, DVB))
            wo = w_out_even[i].astype(BF16)
            x = out_projection_even(x, g1, jnp.concatenate([h_c, h_d], axis=0), jnp.concatenate([o_c, o_d], axis=0),
                                    wo[:A_V], wo[A_V:], seg)
        else:
            proj = in_projection(x, norm_mix[l][None], sc1, sh1, _pad_cols(w_in_odd[i], O_COLS).astype(BF16), O_TN,
                                 seg)
            dtb, alog = _group_compact(dt_bias[i]), _group_compact(a_log[i])
            dsk = jnp.repeat(d_skip[i], P_C)[None]
            cw, cb = conv_w[i], conv_b[i][None]
            y_c, sf, sb = ssd(proj, cw, cb, dtb, alog, dsk, None, 0, nb_c, seq_c)
            y_d, _, _ = ssd(proj, cw, cb, dtb, alog, dsk, _ssd_state_layout(state_ssd[:, i]), n_ctx, nb_d, seq_d)
            new_s.append(jnp.stack([_ssd_state_layout_inv(sf), _ssd_state_layout_inv(sb)], axis=1))
            x = out_projection_odd(x, g1, jnp.concatenate([y_c, y_d], axis=0), ssd_norm[i][None],
                                   w_out_odd[i].astype(BF16), seg)
        x = ffn(x, norm_ffn[l][None], sc2, sh2, g2, w_gate[l].astype(BF16), w_up[l].astype(BF16),
                w_down[l].astype(BF16), seg)

    y_prompt = x[:n_ctx].reshape(nb_c, seq_c, d)
    y_sample = x[n_ctx:].reshape(nb_d, seq_d, d)
    return (y_prompt, y_sample, jnp.stack(new_k, axis=1), jnp.stack(new_v, axis=1), jnp.stack(new_c, axis=1),
            jnp.stack(new_n, axis=1), jnp.stack(new_m, axis=1), jnp.stack(new_s, axis=1))
```

```python
import functools
import math

import jax
import jax.numpy as jnp
import numpy as np
from jax import lax
from jax.experimental import pallas as pl
from jax.experimental.pallas import tpu as pltpu

F32 = jnp.float32
BF16 = jnp.bfloat16

D_MODEL = 2048
DEPTH = 4
GRID_W = 64
RMS_EPS = 1e-6
H_A, DK_A, DV_A, MLSTM_CHUNK = 8, 64, 128, 64
H_B, DB, DVB = 8, 64, 128
ROPE_THETA = 10000.0
ROPE_PAIRS = DB // 4
D_INNER = 2 * D_MODEL
P_C, N_C, G_C = 64, 128, 8
H_C = D_INNER // P_C
R_C = H_C // G_C
SSD_CONV = 4
SSD_CHUNK = 128
D_FF = -(-8 * D_MODEL // (3 * 256)) * 256
A_QK, A_V = H_A * DK_A, H_A * DV_A
B_QK, B_V = H_B * 2 * DB, H_B * DVB
CONV_CH = D_INNER + 2 * G_C * N_C

LANES = 128
VMEM_LIMIT = 56 * 1024 * 1024

TM = 512
ADA_TN = 1024
E_COLS = 6400
E_TN = 1280
O_COLS = 10752
O_TN = 1536
FF_TN = 512
OUT_TN = 512
ATT_TQ = 256
MLSTM_UNROLL = 2

EB_Q, EB_K, EB_V, EB_O = 0, 4, 8, 16
EB_BQ, EB_BK, EB_BV, EB_G = 24, 32, 40, 48
OB_Z, OB_X, OB_B, OB_C, OB_DT = 0, 32, 64, 72, 80


def _cparams(sem):
    return pltpu.CompilerParams(dimension_semantics=sem, vmem_limit_bytes=VMEM_LIMIT)


def _silu(x):
    return x * jax.nn.sigmoid(x)


def _split3(x):
    hi = x.astype(BF16)
    r1 = x - hi.astype(F32)
    mid = r1.astype(BF16)
    lo = (r1 - mid.astype(F32)).astype(BF16)
    return hi, mid, lo


def _dot(a, b):
    return jnp.dot(a, b, preferred_element_type=F32)


def _dot3_l(x, m):
    hi, mid, lo = _split3(x)
    return _dot(hi, m) + _dot(mid, m) + _dot(lo, m)


def _dot3_r(m, x):
    hi, mid, lo = _split3(x)
    return _dot(m, hi) + _dot(m, mid) + _dot(m, lo)


def _row_group(i, mod_rows):
    return 0 if mod_rows is None else 1 + (i * TM) // mod_rows


def _ada_kernel(c_ref, w_ref, b_ref, o_ref):
    s = _silu(c_ref[...]).astype(BF16)
    o_ref[...] = _dot(s, w_ref[...].astype(BF16)) + b_ref[...]


def ada_modulation(cvecs, w_ada, b_ada):
    depth, d, n = w_ada.shape
    return pl.pallas_call(
        _ada_kernel,
        grid=(depth, n // ADA_TN),
        in_specs=[
            pl.BlockSpec((8, d), lambda l, j: (0, 0)),
            pl.BlockSpec((None, d, ADA_TN), lambda l, j: (l, 0, j)),
            pl.BlockSpec((None, 1, ADA_TN), lambda l, j: (l, 0, j)),
        ],
        out_specs=pl.BlockSpec((None, 8, ADA_TN), lambda l, j: (l, 0, j)),
        out_shape=jax.ShapeDtypeStruct((depth, 8, n), F32),
        compiler_params=_cparams(("parallel", "parallel")),
        name="ada_modulation",
    )(cvecs, w_ada, b_ada.reshape(depth, 1, n))


def _norm_mod(x, gain, sc, sh):
    ms = jnp.mean(x * x, axis=-1, keepdims=True)
    y = x * lax.rsqrt(ms + RMS_EPS) * gain
    return y * (1.0 + sc) + sh


def _inproj_kernel(x_ref, gain_ref, sc_ref, sh_ref, w_ref, o_ref, h_ref, *, nblk):
    @pl.when(pl.program_id(1) == 0)
    def _():
        h_ref[...] = _norm_mod(x_ref[...], gain_ref[...], sc_ref[...], sh_ref[...]).astype(BF16)

    acc = _dot(h_ref[...], w_ref[...])
    for k in range(nblk):
        o_ref[k] = acc[:, k * LANES:(k + 1) * LANES]


def in_projection(x, gain, sc, sh, w, tn, mod_rows):
    m, d = x.shape
    n = w.shape[1]
    nblk = tn // LANES
    grp = lambda i, j: (_row_group(i, mod_rows), 0, 0)
    return pl.pallas_call(
        functools.partial(_inproj_kernel, nblk=nblk),
        grid=(m // TM, n // tn),
        in_specs=[
            pl.BlockSpec((TM, d), lambda i, j: (i, 0)),
            pl.BlockSpec((1, d), lambda i, j: (0, 0)),
            pl.BlockSpec((None, 1, d), grp),
            pl.BlockSpec((None, 1, d), grp),
            pl.BlockSpec((d, tn), lambda i, j: (0, j)),
        ],
        out_specs=pl.BlockSpec((nblk, TM, LANES), lambda i, j: (j, i, 0)),
        out_shape=jax.ShapeDtypeStruct((n // LANES, m, LANES), F32),
        scratch_shapes=[pltpu.VMEM((TM, d), BF16)],
        compiler_params=_cparams(("parallel", "arbitrary")),
        name="in_projection",
    )(x, gain, sc, sh, w)


def _ffn_kernel(x_ref, gain_ref, sc_ref, sh_ref, g_ref, wg_ref, wu_ref, wd_ref, o_ref, h_ref, acc_ref):
    f = pl.program_id(1)

    @pl.when(f == 0)
    def _():
        h_ref[...] = _norm_mod(x_ref[...], gain_ref[...], sc_ref[...], sh_ref[...]).astype(BF16)
        acc_ref[...] = jnp.zeros_like(acc_ref)

    h = h_ref[...]
    gate = _dot(h, wg_ref[...])
    up = _dot(h, wu_ref[...])
    ff = (_silu(gate) * up).astype(BF16)
    acc_ref[...] += _dot(ff, wd_ref[...])

    @pl.when(f == pl.num_programs(1) - 1)
    def _():
        o_ref[...] = x_ref[...] + g_ref[...] * acc_ref[...]


def ffn(x, gain, sc, sh, g, wg, wu, wd, mod_rows):
    m, d = x.shape
    dff = wg.shape[1]
    grp = lambda i, f: (_row_group(i, mod_rows), 0, 0)
    return pl.pallas_call(
        _ffn_kernel,
        grid=(m // TM, dff // FF_TN),
        in_specs=[
            pl.BlockSpec((TM, d), lambda i, f: (i, 0)),
            pl.BlockSpec((1, d), lambda i, f: (0, 0)),
            pl.BlockSpec((None, 1, d), grp),
            pl.BlockSpec((None, 1, d), grp),
            pl.BlockSpec((None, 1, d), grp),
            pl.BlockSpec((d, FF_TN), lambda i, f: (0, f)),
            pl.BlockSpec((d, FF_TN), lambda i, f: (0, f)),
            pl.BlockSpec((FF_TN, d), lambda i, f: (f, 0)),
        ],
        out_specs=pl.BlockSpec((TM, d), lambda i, f: (i, 0)),
        out_shape=jax.ShapeDtypeStruct((m, d), F32),
        scratch_shapes=[pltpu.VMEM((TM, d), BF16), pltpu.VMEM((TM, d), F32)],
        compiler_params=_cparams(("parallel", "arbitrary")),
        name="ffn",
    )(x, gain, sc, sh, g, wg, wu, wd)


def _outproj_even_kernel(x_ref, g_ref, a_ref, b_ref, wa_ref, wb_ref, o_ref):
    acc = _dot(a_ref[...], wa_ref[...]) + _dot(b_ref[...], wb_ref[...])
    o_ref[...] = x_ref[...] + g_ref[...] * acc


def out_projection_even(x, g, ha, ob, wa, wb, mod_rows):
    m, d = x.shape
    ka, kb = ha.shape[1], ob.shape[1]
    grp = lambda i, j: (_row_group(i, mod_rows), 0, j)
    return pl.pallas_call(
        _outproj_even_kernel,
        grid=(m // TM, d // OUT_TN),
        in_specs=[
            pl.BlockSpec((TM, OUT_TN), lambda i, j: (i, j)),
            pl.BlockSpec((None, 1, OUT_TN), grp),
            pl.BlockSpec((TM, ka), lambda i, j: (i, 0)),
            pl.BlockSpec((TM, kb), lambda i, j: (i, 0)),
            pl.BlockSpec((ka, OUT_TN), lambda i, j: (0, j)),
            pl.BlockSpec((kb, OUT_TN), lambda i, j: (0, j)),
        ],
        out_specs=pl.BlockSpec((TM, OUT_TN), lambda i, j: (i, j)),
        out_shape=jax.ShapeDtypeStruct((m, d), F32),
        compiler_params=_cparams(("parallel", "arbitrary")),
        name="out_projection_even",
    )(x, g, ha, ob, wa, wb)


def _outproj_odd_kernel(x_ref, g_ref, y_ref, gain_ref, w_ref, o_ref, yn_ref):
    @pl.when(pl.program_id(1) == 0)
    def _():
        y = y_ref[...]
        ms = jnp.mean(y * y, axis=-1, keepdims=True)
        yn_ref[...] = (y * lax.rsqrt(ms + RMS_EPS) * gain_ref[...]).astype(BF16)

    o_ref[...] = x_ref[...] + g_ref[...] * _dot(yn_ref[...], w_ref[...])


def out_projection_odd(x, g, y, gain, w, mod_rows):
    m, d = x.shape
    k = y.shape[1]
    grp = lambda i, j: (_row_group(i, mod_rows), 0, j)
    return pl.pallas_call(
        _outproj_odd_kernel,
        grid=(m // TM, d // OUT_TN),
        in_specs=[
            pl.BlockSpec((TM, OUT_TN), lambda i, j: (i, j)),
            pl.BlockSpec((None, 1, OUT_TN), grp),
            pl.BlockSpec((TM, k), lambda i, j: (i, 0)),
            pl.BlockSpec((1, k), lambda i, j: (0, 0)),
            pl.BlockSpec((k, OUT_TN), lambda i, j: (0, j)),
        ],
        out_specs=pl.BlockSpec((TM, OUT_TN), lambda i, j: (i, j)),
        out_shape=jax.ShapeDtypeStruct((m, d), F32),
        scratch_shapes=[pltpu.VMEM((TM, k), BF16)],
        compiler_params=_cparams(("parallel", "arbitrary")),
        name="out_projection_odd",
    )(x, g, y, gain, w)


def _lane_iota(shape):
    return lax.broadcasted_iota(jnp.int32, shape, len(shape) - 1)


def _half_rms_norm(x, gain):
    lo = _lane_iota(x.shape) < DB
    x2 = x * x
    s0 = jnp.sum(jnp.where(lo, x2, 0.0), axis=-1, keepdims=True)
    s1 = jnp.sum(jnp.where(lo, 0.0, x2), axis=-1, keepdims=True)
    r = jnp.where(lo, lax.rsqrt(s0 * (1.0 / DB) + RMS_EPS), lax.rsqrt(s1 * (1.0 / DB) + RMS_EPS))
    return x * r * gain


def _rope(x, cos, sin_signed):
    first = (_lane_iota(x.shape) % 32) < ROPE_PAIRS
    partner = jnp.where(first, pltpu.roll(x, LANES - ROPE_PAIRS, 1), pltpu.roll(x, ROPE_PAIRS, 1))
    return x * cos + partner * sin_signed


def _softmax_pv(s, vv):
    p = jnp.exp(s - jnp.max(s, axis=-1, keepdims=True))
    return _dot(p.astype(BF16), vv) * (1.0 / jnp.sum(p, axis=-1, keepdims=True))


def _diff_attn_core(q, kk, vv, lam, dgain, lam_init):
    lo = _lane_iota(q.shape) < DB
    qs = q * (DB ** -0.5)
    q1 = jnp.where(lo, qs, 0.0).astype(BF16)
    q2 = jnp.where(lo, 0.0, qs).astype(BF16)
    nt = (((1,), (1,)), ((), ()))
    s1 = lax.dot_general(q1, kk, nt, preferred_element_type=F32)
    s2 = lax.dot_general(q2, kk, nt, preferred_element_type=F32)
    o = _softmax_pv(s1, vv) - lam * _softmax_pv(s2, vv)
    ms = jnp.mean(o * o, axis=-1, keepdims=True)
    return o * lax.rsqrt(ms + RMS_EPS) * dgain * (1.0 - lam_init)


def _lambda_value(lam_ref, lam_init):
    lp = lam_ref[...]
    a = jnp.sum(lp[0:1] * lp[1:2], axis=-1, keepdims=True)
    b = jnp.sum(lp[2:3] * lp[3:4], axis=-1, keepdims=True)
    return jnp.exp(a) - jnp.exp(b) + lam_init


def _attn_ctx_kernel(lam_ref, q_ref, k_ref, v_ref, qg_ref, kg_ref, dg_ref, o_ref, kd_ref, vd_ref, *, lam_init):
    lam = _lambda_value(lam_ref, lam_init)
    for h in range(H_B):
        cs = slice(h * LANES, (h + 1) * LANES)
        kd = _half_rms_norm(k_ref[h], kg_ref[...])
        kd_ref[:, cs] = kd
        v = v_ref[h]
        vd_ref[:, cs] = v
        q = _half_rms_norm(q_ref[h], qg_ref[...])
        o = _diff_attn_core(q, kd.astype(BF16), v.astype(BF16), lam, dg_ref[...], lam_init)
        o_ref[:, cs] = o.astype(BF16)


def diff_attention_ctx(proj, lam_p, qg, kg, dg, nb, seq, lam_init):
    m = nb * seq
    vec = pl.BlockSpec((1, LANES), lambda b: (0, 0))
    slab = lambda base: pl.BlockSpec((H_B, seq, LANES), lambda b: (base // H_B, b, 0))
    outb = pl.BlockSpec((seq, H_B * LANES), lambda b: (b, 0))
    return pl.pallas_call(
        functools.partial(_attn_ctx_kernel, lam_init=lam_init),
        grid=(nb,),
        in_specs=[pl.BlockSpec((4, DB), lambda b: (0, 0)), slab(EB_BQ), slab(EB_BK), slab(EB_BV), vec, vec, vec],
        out_specs=[outb, outb, outb],
        out_shape=[jax.ShapeDtypeStruct((m, B_V), BF16), jax.ShapeDtypeStruct((m, B_QK), F32),
                   jax.ShapeDtypeStruct((m, B_V), F32)],
        compiler_params=_cparams(("parallel",)),
        name="diff_attention_ctx",
    )(lam_p, proj, proj, proj, qg, kg, dg)


def _attn_dec_kernel(lam_ref, q_ref, k_ref, v_ref, kc_ref, vc_ref, qg_ref, kg_ref, dg_ref, cq_ref, sq_ref, ck_ref,
                     sk_ref, o_ref, ks_ref, vs_ref, *, lam_init, past):
    @pl.when(pl.program_id(2) == 0)
    def _():
        ks_ref[0:past, :] = kc_ref[...].astype(BF16)
        vs_ref[0:past, :] = vc_ref[...].astype(BF16)
        kd = _rope(_half_rms_norm(k_ref[...], kg_ref[...]), ck_ref[...], sk_ref[...])
        ks_ref[past:, :] = kd.astype(BF16)
        vs_ref[past:, :] = v_ref[...].astype(BF16)

    q = _rope(_half_rms_norm(q_ref[...], qg_ref[...]), cq_ref[...], sq_ref[...])
    lam = _lambda_value(lam_ref, lam_init)
    o_ref[...] = _diff_attn_core(q, ks_ref[...], vs_ref[...], lam, dg_ref[...], lam_init).astype(BF16)


def diff_attention_dec(proj, cache_k, cache_v, lam_p, qg, kg, dg, cos, sin, row0, nb, seq, lam_init):
    past = cache_k.shape[1]
    nq = seq // ATT_TQ
    rq, rk = row0 // ATT_TQ, row0 // seq
    vec = pl.BlockSpec((1, LANES), lambda b, h, i: (0, 0))
    kslab = lambda base: pl.BlockSpec((None, seq, LANES), lambda b, h, i: (base + h, rk + b, 0))
    cslab = pl.BlockSpec((None, past, LANES), lambda b, h, i: (b, 0, h))
    tq = pl.BlockSpec((ATT_TQ, LANES), lambda b, h, i: (i, 0))
    tk = pl.BlockSpec((seq, LANES), lambda b, h, i: (0, 0))
    return pl.pallas_call(
        functools.partial(_attn_dec_kernel, lam_init=lam_init, past=past),
        grid=(nb, H_B, nq),
        in_specs=[pl.BlockSpec((4, DB), lambda b, h, i: (0, 0)),
                  pl.BlockSpec((None, ATT_TQ, LANES), lambda b, h, i: (EB_BQ + h, rq + b * nq + i, 0)),
                  kslab(EB_BK), kslab(EB_BV), cslab, cslab, vec, vec, vec, tq, tq, tk, tk],
        out_specs=pl.BlockSpec((ATT_TQ, LANES), lambda b, h, i: (b * nq + i, h)),
        out_shape=jax.ShapeDtypeStruct((nb * seq, B_V), BF16),
        scratch_shapes=[pltpu.VMEM((past + seq, LANES), BF16), pltpu.VMEM((past + seq, LANES), BF16)],
        compiler_params=_cparams(("parallel", "parallel", "arbitrary")),
        name="diff_attention_dec",
    )(lam_p, proj, proj, proj, cache_k, cache_v, qg, kg, dg, cos, sin, cos, sin)


def rope_tables(n_tokens):
    n_rows = n_tokens // GRID_W
    rows, cols = jnp.meshgrid(jnp.arange(n_rows, dtype=F32), jnp.arange(GRID_W, dtype=F32), indexing='ij')
    inv_freq = ROPE_THETA ** (-jnp.arange(ROPE_PAIRS, dtype=F32) / ROPE_PAIRS)
    ang = jnp.stack([rows.reshape(-1, 1) * inv_freq, cols.reshape(-1, 1) * inv_freq], axis=0)
    cos, sin = jnp.cos(ang), jnp.sin(ang)
    cos_half = jnp.concatenate([cos[0], cos[0], cos[1], cos[1]], axis=-1)
    sin_half = jnp.concatenate([-sin[0], sin[0], -sin[1], sin[1]], axis=-1)
    return jnp.tile(cos_half, (1, 2)), jnp.tile(sin_half, (1, 2))


def _tri(n, lower):
    r = lax.broadcasted_iota(jnp.int32, (n, n), 0)
    c = lax.broadcasted_iota(jnp.int32, (n, n), 1)
    return jnp.where((r >= c) if lower else (r <= c), 1.0, 0.0).astype(BF16)


def _mlstm_direction(d, rows, q_ref, k_ref, v_ref, gcol, grow, qk, c_ref, n_prev, m_prev, tri_l, tri_u):
    lc = MLSTM_CHUNK
    ri = lax.broadcasted_iota(jnp.int32, (lc, lc), 0)
    ci = lax.broadcasted_iota(jnp.int32, (lc, lc), 1)
    mask = (ci <= ri) if d == 0 else (ci >= ri)
    lane = _lane_iota((lc, LANES))
    lo = lane < DK_A
    fcol = jax.nn.log_sigmoid(gcol)
    frow = jax.nn.log_sigmoid(grow)
    bcol_all = _dot3_r(tri_l if d == 0 else tri_u, fcol)
    brow_all = _dot3_l(frow, tri_u if d == 0 else tri_l)
    q = q_ref[rows, :]
    k = k_ref[rows, :] * (DK_A ** -0.5)
    cst = c_ref[d]
    qn = q * n_prev
    hs, m_news, e_cols, decays = [], [], [], []
    for j in range(2):
        li_lane, lf_lane = 8 * d + j, 16 + 8 * d + j
        b_col = bcol_all[:, lf_lane:lf_lane + 1]
        b_row = brow_all[lf_lane:lf_lane + 1, :]
        li_col = gcol[:, li_lane:li_lane + 1]
        li_row = grow[li_lane:li_lane + 1, :]
        m_p = m_prev[j]
        dmat = jnp.where(mask, b_col - b_row + li_row, -jnp.inf)
        inter = b_col + m_p
        m_row = jnp.maximum(inter, jnp.max(dmat, axis=-1, keepdims=True))
        w_inter = jnp.exp(inter - m_row)
        s = qk[j] * jnp.exp(dmat - m_row)
        qm = jnp.where(lo, q, 0.0) if j == 0 else jnp.where(lo, 0.0, q)
        vj = v_ref[j, rows, :].astype(BF16)
        num = w_inter * _dot(qm.astype(BF16), cst.astype(BF16)) + _dot(s.astype(BF16), vj)
        qn_j = jnp.sum(jnp.where(lo, qn, 0.0) if j == 0 else jnp.where(lo, 0.0, qn), axis=-1, keepdims=True)
        den = w_inter * qn_j + jnp.sum(s, axis=-1, keepdims=True)
        hs.append(num * (1.0 / jnp.maximum(jnp.abs(den), jnp.exp(-m_row))))
        b_end = b_col[lc - 1:lc, :] if d == 0 else b_col[0:1, :]
        g_col = b_end - b_col + li_col
        m_new = jnp.maximum(b_end + m_p, jnp.max(g_col, axis=0, keepdims=True))
        decays.append(jnp.exp(b_end + m_p - m_new))
        e_cols.append(jnp.exp(g_col - m_new))
        m_news.append(m_new)
    wk = k * jnp.where(lo, e_cols[0], e_cols[1])
    wkb = wk.astype(BF16)
    tn = (((0,), (0,)), ((), ()))
    u0 = lax.dot_general(wkb, v_ref[0, rows, :].astype(BF16), tn, preferred_element_type=F32)
    u1 = lax.dot_general(wkb, v_ref[1, rows, :].astype(BF16), tn, preferred_element_type=F32)
    top = lax.broadcasted_iota(jnp.int32, (LANES, LANES), 0) < DK_A
    c_ref[d] = jnp.where(top, decays[0], decays[1]) * cst + jnp.where(top, u0, u1)
    lane1 = _lane_iota((1, LANES)) < DK_A
    n_new = jnp.where(lane1, decays[0], decays[1]) * n_prev + jnp.sum(wk, axis=0, keepdims=True)
    return jnp.concatenate(hs, axis=-1), n_new, m_news


def _mlstm_kernel(q_ref, k_ref, v_ref, ao_ref, g_ref, bias_ref, gain_ref, c0_ref, n0_ref, m0_ref,
                  h_ref, cf_ref, nf_ref, mf_ref, gc_ref, gt_ref, hf_ref, hb_ref, *, seq):
    lc = MLSTM_CHUNK
    nc = seq // lc
    pair = pl.program_id(1)
    shift = (LANES - 2 * pair) % LANES
    bias = pltpu.roll(jnp.broadcast_to(bias_ref[...], (8, LANES)), shift, 1)[0:1]
    for blk in range(seq // LANES):
        r = slice(blk * LANES, (blk + 1) * LANES)
        gs = pltpu.roll(g_ref[r, :], shift, 1) + bias
        gc_ref[r, :] = gs
        gst = gs.T
        for half in range(LANES // lc):
            gt_ref[blk * (LANES // lc) + half] = gst[0:32, half * lc:(half + 1) * lc]
    cf_ref[...] = c0_ref[...]
    tri_l, tri_u = _tri(lc, True), _tri(lc, False)
    nt = (((1,), (1,)), ((), ()))
    lo = _lane_iota((lc, LANES)) < DK_A

    def chunk_qk(rows):
        q = q_ref[rows, :]
        kb = (k_ref[rows, :] * (DK_A ** -0.5)).astype(BF16)
        return [lax.dot_general(jnp.where(lo, q, 0.0).astype(BF16), kb, nt, preferred_element_type=F32),
                lax.dot_general(jnp.where(lo, 0.0, q).astype(BF16), kb, nt, preferred_element_type=F32)]

    def body(c, carry):
        n_f, m_f0, m_f1, n_b, m_b0, m_b1 = carry
        rf = pl.ds(pl.multiple_of(c * lc, lc), lc)
        rb = pl.ds(pl.multiple_of((nc - 1 - c) * lc, lc), lc)
        h_f, n_f, m_f = _mlstm_direction(0, rf, q_ref, k_ref, v_ref, gc_ref[rf, :], gt_ref[c], chunk_qk(rf),
                                         cf_ref, n_f, (m_f0, m_f1), tri_l, tri_u)
        hf_ref[rf, :] = h_f
        h_b, n_b, m_b = _mlstm_direction(1, rb, q_ref, k_ref, v_ref, gc_ref[rb, :], gt_ref[nc - 1 - c], chunk_qk(rb),
                                         cf_ref, n_b, (m_b0, m_b1), tri_l, tri_u)
        hb_ref[rb, :] = h_b
        return n_f, m_f[0], m_f[1], n_b, m_b[0], m_b[1]

    m0 = m0_ref[...]
    init = (n0_ref[0], m0[0][:, 0:1], m0[0][:, DK_A:DK_A + 1], n0_ref[1], m0[1][:, 0:1], m0[1][:, DK_A:DK_A + 1])
    n_f, m_f0, m_f1, n_b, m_b0, m_b1 = lax.fori_loop(0, nc, body, init, unroll=MLSTM_UNROLL)
    nf_ref[0] = n_f
    nf_ref[1] = n_b
    lane1 = _lane_iota((1, LANES)) < DK_A
    mf_ref[0] = jnp.where(lane1, m_f0, m_f1)
    mf_ref[1] = jnp.where(lane1, m_b0, m_b1)

    gain = gain_ref[...]
    for blk in range(seq // LANES):
        r = slice(blk * LANES, (blk + 1) * LANES)
        outs = []
        for j in range(2):
            cs = slice(j * DV_A, (j + 1) * DV_A)
            hh = hf_ref[r, cs] + hb_ref[r, cs]
            ms = jnp.mean(hh * hh, axis=-1, keepdims=True)
            outs.append(hh * lax.rsqrt(ms + RMS_EPS) * gain[:, cs] * jax.nn.sigmoid(ao_ref[j, r, :]))
        h_ref[r, :] = jnp.concatenate(outs, axis=-1).astype(BF16)


def mlstm(proj, bias, gain, c0, n0, m0, row0, nb, seq):
    rb = row0 // seq
    npair = H_A // 2
    slab = lambda base: pl.BlockSpec((None, seq, LANES), lambda b, p: (base + p, rb + b, 0))
    slab2 = lambda base: pl.BlockSpec((2, seq, LANES), lambda b, p: (base // 2 + p, rb + b, 0))
    st_c = pl.BlockSpec((None, None, 2, LANES, LANES), lambda b, p: (b, p, 0, 0, 0))
    st_v = pl.BlockSpec((None, None, 2, 1, LANES), lambda b, p: (b, p, 0, 0, 0))
    return pl.pallas_call(
        functools.partial(_mlstm_kernel, seq=seq),
        grid=(nb, npair),
        in_specs=[slab(EB_Q), slab(EB_K), slab2(EB_V), slab2(EB_O),
                  pl.BlockSpec((None, seq, LANES), lambda b, p: (EB_G, rb + b, 0)),
                  pl.BlockSpec((1, LANES), lambda b, p: (0, 0)),
                  pl.BlockSpec((1, 2 * DV_A), lambda b, p: (0, p)),
                  st_c, st_v, st_v],
        out_specs=[pl.BlockSpec((seq, 2 * DV_A), lambda b, p: (b, p)), st_c, st_v, st_v],
        out_shape=[jax.ShapeDtypeStruct((nb * seq, A_V), BF16),
                   jax.ShapeDtypeStruct((nb, npair, 2, LANES, LANES), F32),
                   jax.ShapeDtypeStruct((nb, npair, 2, 1, LANES), F32),
                   jax.ShapeDtypeStruct((nb, npair, 2, 1, LANES), F32)],
        scratch_shapes=[pltpu.VMEM((seq, LANES), F32), pltpu.VMEM((seq // MLSTM_CHUNK, 32, MLSTM_CHUNK), F32),
                        pltpu.VMEM((seq, 2 * DV_A), F32),
                        pltpu.VMEM((seq, 2 * DV_A), F32)],
        compiler_params=_cparams(("parallel", "parallel")),
        name="mlstm",
    )(proj, proj, proj, proj, proj, bias, gain, c0, n0, m0)


def _softplus(x):
    return jnp.maximum(x, 0.0) + jnp.log1p(jnp.exp(-jnp.abs(x)))


def _conv_silu(x, w, b):
    n = x.shape[0]
    t = lax.broadcasted_iota(jnp.int32, x.shape, 0)
    y = w[2:3] * x
    y = y + w[0:1] * jnp.where(t >= 2, pltpu.roll(x, 2, 0), 0.0)
    y = y + w[1:2] * jnp.where(t >= 1, pltpu.roll(x, 1, 0), 0.0)
    y = y + w[3:4] * jnp.where(t <= n - 2, pltpu.roll(x, n - 1, 0), 0.0)
    return _silu(y + b)


def _ssd_kernel(xs_ref, bm_ref, cm_ref, z_ref, dtr_ref, wx_ref, wb_ref, wc_ref, bx_ref, bb_ref, bc_ref,
                dtb_ref, alog_ref, dsk_ref, *rest, seq, has_init):
    if has_init:
        sf0_ref, sb0_ref = rest[:2]
        rest = rest[2:]
    y_ref, sf_ref, sb_ref, xa_ref, ba_ref, ca_ref, dt_ref, yb_ref = rest
    lc = SSD_CHUNK
    nc = seq // lc
    nslab = R_C * P_C // LANES
    grp = pl.program_id(1)
    lane = _lane_iota((lc, LANES))
    lane1 = _lane_iota((1, LANES))
    lo = lane < P_C
    lo1 = lane1 < P_C

    for s in range(nslab):
        cs = slice(s * LANES, (s + 1) * LANES)
        xa_ref[:, cs] = _conv_silu(xs_ref[s], wx_ref[:, cs], bx_ref[:, cs])
    ba_ref[...] = _conv_silu(bm_ref[...], wb_ref[...], bb_ref[...])
    ca_ref[...] = _conv_silu(cm_ref[...], wc_ref[...], bc_ref[...])

    dtr = dtr_ref[...]
    lane_l = _lane_iota(dtr.shape)
    fwd = pltpu.roll(dtr, (LANES - R_C * grp) % LANES, 1)
    bwd = pltpu.roll(dtr, (LANES - H_C + R_C - R_C * grp) % LANES, 1)
    dt_all = _softplus(jnp.where(lane_l < R_C, fwd, bwd) + dtb_ref[...])
    dt_ref[...] = jnp.where(lane_l < 2 * R_C, dt_all, 0.0)
    a_neg = -jnp.exp(alog_ref[...])

    tri_l, tri_u = _tri(lc, True), _tri(lc, False)
    ri = lax.broadcasted_iota(jnp.int32, (lc, lc), 0)
    ci = lax.broadcasted_iota(jnp.int32, (lc, lc), 1)
    causal, anti = ci <= ri, ci >= ri
    nt = (((1,), (1,)), ((), ()))

    def head_mask(j, v):
        return (jnp.where(lo, v, 0.0) if j == 0 else jnp.where(lo, 0.0, v)).astype(BF16)

    sf_ref[...] = sf0_ref[...] if has_init else jnp.zeros(sf_ref.shape, F32)

    def fwd_body(c, carry):
        rows = pl.ds(pl.multiple_of(c * lc, lc), lc)
        dt = dt_ref[rows, :]
        acum = _dot3_r(tri_l, dt * a_neg)
        a_end = acum[lc - 1:lc, :]
        te_t = (jnp.exp(a_end - acum) * dt).T
        ea = jnp.exp(acum)
        dec = jnp.exp(a_end)
        cc = ca_ref[rows, :]
        bc_t = ba_ref[rows, :].T
        for s in range(nslab):
            cs = slice(s * LANES, (s + 1) * LANES)
            st = sf_ref[:, cs]
            xs = xa_ref[rows, cs]
            acc_y = jnp.zeros((lc, LANES), F32)
            acc_s = jnp.zeros((N_C, LANES), F32)
            for j in range(2):
                r = 2 * s + j
                acc_y = acc_y + _dot((ea[:, r:r + 1] * cc).astype(BF16), head_mask(j, st))
                acc_s = acc_s + _dot((bc_t * te_t[r:r + 1, :]).astype(BF16), head_mask(j, xs))
            yb_ref[rows, cs] = acc_y
            sf_ref[:, cs] = jnp.where(lo1, dec[:, 2 * s:2 * s + 1], dec[:, 2 * s + 1:2 * s + 2]) * st + acc_s
        return carry

    lax.fori_loop(0, nc, fwd_body, 0)

    sb_ref[...] = sb0_ref[...] if has_init else jnp.zeros(sb_ref.shape, F32)

    def bwd_body(i, carry):
        rows = pl.ds(pl.multiple_of((nc - 1 - i) * lc, lc), lc)
        dt = dt_ref[rows, :]
        da = dt * a_neg
        acum_b = _dot3_r(tri_u, da)
        acum = jnp.where(lane < R_C, _dot3_r(tri_l, da), acum_b)
        acum_t = acum.T
        dt_t = dt.T
        a_end = acum_b[0:1, :]
        te_t = (jnp.exp(a_end - acum_b) * dt).T
        ea = jnp.exp(acum)
        dec = jnp.exp(a_end)
        cc = ca_ref[rows, :]
        bc = ba_ref[rows, :]
        bc_t = bc.T
        cb = lax.dot_general(cc.astype(BF16), bc.astype(BF16), nt, preferred_element_type=F32)
        for s in range(nslab):
            cs = slice(s * LANES, (s + 1) * LANES)
            st = sb_ref[:, cs]
            xs = xa_ref[rows, cs]
            acc_y = yb_ref[rows, cs] + dsk_ref[:, cs] * xs
            acc_s = jnp.zeros((N_C, LANES), F32)
            for j in range(2):
                r = 2 * s + j
                rb = R_C + r
                wf = jnp.exp(jnp.where(causal, acum[:, r:r + 1] - acum_t[r:r + 1, :], -jnp.inf)) * dt_t[r:r + 1, :]
                wb = jnp.exp(jnp.where(anti, acum[:, rb:rb + 1] - acum_t[rb:rb + 1, :], -jnp.inf)) * dt_t[rb:rb + 1, :]
                xm = head_mask(j, xs)
                acc_y = acc_y + _dot((cb * (wf + wb)).astype(BF16), xm)
                acc_y = acc_y + _dot((ea[:, rb:rb + 1] * cc).astype(BF16), head_mask(j, st))
                acc_s = acc_s + _dot((bc_t * te_t[rb:rb + 1, :]).astype(BF16), xm)
            y_ref[rows, cs] = acc_y * _silu(z_ref[s, rows, :])
            dslab = jnp.where(lo1, dec[:, R_C + 2 * s:R_C + 2 * s + 1], dec[:, R_C + 2 * s + 1:R_C + 2 * s + 2])
            sb_ref[:, cs] = dslab * st + acc_s
        return carry

    lax.fori_loop(0, nc, bwd_body, 0)


def ssd(proj, conv_w, conv_b, dtb, alog, dsk, s0, row0, nb, seq):
    rb = row0 // seq
    gw = R_C * P_C
    nslab = gw // LANES
    slab = lambda base: pl.BlockSpec((None, seq, LANES), lambda b, g: (base + g, rb + b, 0))
    slab4 = lambda base: pl.BlockSpec((nslab, seq, LANES), lambda b, g: (base // nslab + g, rb + b, 0))
    xo, bo, co = 0, D_INNER // LANES, (D_INNER + G_C * N_C) // LANES
    cw = lambda rws, width, off: pl.BlockSpec((rws, width), lambda b, g: (0, off * LANES // width + g))
    vec = pl.BlockSpec((None, 1, LANES), lambda b, g: (g, 0, 0))
    st_in = lambda d: pl.BlockSpec((None, None, None, N_C, gw), lambda b, g: (b, d, g, 0, 0))
    st_out = pl.BlockSpec((None, None, N_C, gw), lambda b, g: (b, g, 0, 0))
    has_init = s0 is not None
    return pl.pallas_call(
        functools.partial(_ssd_kernel, seq=seq, has_init=has_init),
        grid=(nb, G_C),
        in_specs=[slab4(OB_X), slab(OB_B), slab(OB_C), slab4(OB_Z),
                  pl.BlockSpec((None, seq, LANES), lambda b, g: (OB_DT, rb + b, 0)),
                  cw(SSD_CONV, gw, xo), cw(SSD_CONV, LANES, bo), cw(SSD_CONV, LANES, co),
                  cw(1, gw, xo), cw(1, LANES, bo), cw(1, LANES, co),
                  vec, vec, pl.BlockSpec((1, gw), lambda b, g: (0, g))] + ([st_in(0), st_in(1)] if has_init else []),
        out_specs=[pl.BlockSpec((seq, gw), lambda b, g: (b, g)), st_out, st_out],
        out_shape=[jax.ShapeDtypeStruct((nb * seq, D_INNER), F32),
                   jax.ShapeDtypeStruct((nb, G_C, N_C, gw), F32),
                   jax.ShapeDtypeStruct((nb, G_C, N_C, gw), F32)],
        scratch_shapes=[pltpu.VMEM((seq, gw), F32), pltpu.VMEM((seq, LANES), F32), pltpu.VMEM((seq, LANES), F32),
                        pltpu.VMEM((seq, LANES), F32), pltpu.VMEM((seq, gw), F32)],
        compiler_params=_cparams(("parallel", "parallel")),
        name="ssd",
    )(proj, proj, proj, proj, proj, conv_w, conv_w, conv_w, conv_b, conv_b, conv_b, dtb, alog, dsk,
      *((s0, s0) if has_init else ()))


SSD_REP = 2 * R_C


def _pack3(x, lane):
    hi = x.astype(BF16).astype(F32)
    r1 = x - hi
    mid = r1.astype(BF16).astype(F32)
    r2 = r1 - mid
    grp = lane % (3 * SSD_REP)
    return jnp.where(grp < SSD_REP, x, jnp.where(grp < 2 * SSD_REP, r1, r2)).astype(BF16)


def _ssd2_kernel(xs_ref, bm_ref, cm_ref, z_ref, dtr_ref, wx_ref, wb_ref, wc_ref, bx_ref, bb_ref, bc_ref,
                 dtb_ref, alog_ref, dsk_ref, rep_ref, selw_ref, expf_ref, expb_ref, *rest, seq, has_init):
    if has_init:
        sf0_ref, sb0_ref = rest[:2]
        rest = rest[2:]
    y_ref, sf_ref, sb_ref, xa_ref, ba_ref, ca_ref, dt_ref, af_ref, yb_ref = rest
    lc = SSD_CHUNK
    nc = seq // lc
    gw = R_C * P_C
    nslab = gw // LANES
    lane = _lane_iota((lc, LANES))
    lo = lane < P_C
    fwd_lane = (lane % SSD_REP) < R_C
    first3 = lane < 3 * SSD_REP

    for s in range(nslab):
        cs = slice(s * LANES, (s + 1) * LANES)
        xa_ref[:, cs] = _conv_silu(xs_ref[s], wx_ref[:, cs], bx_ref[:, cs])
    ba_ref[...] = _conv_silu(bm_ref[...], wb_ref[...], bb_ref[...])
    ca_ref[...] = _conv_silu(cm_ref[...], wc_ref[...], bc_ref[...])

    dt_ref[...] = _softplus(_dot3_l(dtr_ref[...], rep_ref[...]) + dtb_ref[...])
    a_neg = -jnp.exp(alog_ref[...])

    tri_l, tri_u = _tri(lc, True), _tri(lc, False)
    ri = lax.broadcasted_iota(jnp.int32, (lc, lc), 0)
    ci = lax.broadcasted_iota(jnp.int32, (lc, lc), 1)
    causal, anti = ci <= ri, ci >= ri
    nt = (((1,), (1,)), ((), ()))

    def expand(ea, te, e_ref):
        ex = _dot(_pack3(jnp.where(first3, ea, te), lane), e_ref[...])
        return ex[:, :gw], ex[:, gw:]

    sf_ref[...] = sf0_ref[...] if has_init else jnp.zeros(sf_ref.shape, F32)

    def fwd_body(c, carry):
        rows = pl.ds(pl.multiple_of(c * lc, lc), lc)
        dt = dt_ref[rows, :]
        acum = _dot3_r(tri_l, dt * a_neg)
        af_ref[rows, :] = acum
        a_end = acum[lc - 1:lc, :]
        ea_x, te_x = expand(jnp.exp(acum), jnp.exp(a_end - acum) * dt, expf_ref)
        st = sf_ref[...]
        yb_ref[rows, :] = _dot(ca_ref[rows, :].astype(BF16), st.astype(BF16)) * ea_x
        bc_t = ba_ref[rows, :].T.astype(BF16)
        sf_ref[...] = st * ea_x[lc - 1:lc, :] + _dot(bc_t, (xa_ref[rows, :] * te_x).astype(BF16))
        return carry

    lax.fori_loop(0, nc, fwd_body, 0)

    sb_ref[...] = sb0_ref[...] if has_init else jnp.zeros(sb_ref.shape, F32)

    def bwd_body(i, carry):
        rows = pl.ds(pl.multiple_of((nc - 1 - i) * lc, lc), lc)
        dt = dt_ref[rows, :]
        acum_b = _dot3_r(tri_u, dt * a_neg)
        acum = jnp.where(fwd_lane, af_ref[rows, :], acum_b)
        acum_t = acum.T
        dt_t = dt.T
        ab = _dot(_pack3(acum, lane), selw_ref[...])
        a_end = acum_b[0:1, :]
        ea_x, te_x = expand(jnp.exp(acum_b), jnp.exp(a_end - acum_b) * dt, expb_ref)
        cc = ca_ref[rows, :].astype(BF16)
        bc = ba_ref[rows, :]
        cb = lax.dot_general(cc, bc.astype(BF16), nt, preferred_element_type=F32)
        st = sb_ref[...]
        xs = xa_ref[rows, :]
        y_inter = _dot(cc, st.astype(BF16)) * ea_x
        for s in range(nslab):
            cs = slice(s * LANES, (s + 1) * LANES)
            ws = []
            for j in range(2):
                r = 2 * s + j
                rb = R_C + r
                wf = jnp.exp(jnp.where(causal, ab[:, r * LANES:(r + 1) * LANES] - acum_t[r:r + 1, :], -jnp.inf))
                wb = jnp.exp(jnp.where(anti, ab[:, rb * LANES:(rb + 1) * LANES] - acum_t[rb:rb + 1, :], -jnp.inf))
                ws.append((cb * (wf * dt_t[r:r + 1, :] + wb * dt_t[rb:rb + 1, :])).astype(BF16))
            xsl = xs[:, cs]
            x2 = jnp.concatenate([jnp.where(lo, xsl, 0.0), jnp.where(lo, 0.0, xsl)], axis=0).astype(BF16)
            y = yb_ref[rows, cs] + y_inter[:, cs] + dsk_ref[:, cs] * xsl + _dot(jnp.concatenate(ws, axis=1), x2)
            y_ref[rows, cs] = y * _silu(z_ref[s, rows, :])
        sb_ref[...] = st * ea_x[0:1, :] + _dot(bc.T.astype(BF16), (xs * te_x).astype(BF16))
        return carry

    lax.fori_loop(0, nc, bwd_body, 0)


def _ssd_constants():
    gw = R_C * P_C
    rep = np.zeros((G_C, LANES, LANES), np.float32)
    for g in range(G_C):
        for l in range(LANES):
            c = l % SSD_REP
            rep[g, (R_C * g + c) if c < R_C else (H_C + R_C * g + c - R_C), l] = 1.0
    selw = np.zeros((LANES, SSD_REP * LANES), np.float32)
    expf = np.zeros((LANES, 2 * gw), np.float32)
    expb = np.zeros((LANES, 2 * gw), np.float32)
    for l in range(3 * SSD_REP):
        c = l % SSD_REP
        selw[l, c * LANES:(c + 1) * LANES] = 1.0
    for l in range(6 * SSD_REP):
        c, q = l % SSD_REP, l // (3 * SSD_REP)
        if c < R_C:
            expf[l, q * gw + c * P_C:q * gw + (c + 1) * P_C] = 1.0
        else:
            expb[l, q * gw + (c - R_C) * P_C:q * gw + (c - R_C + 1) * P_C] = 1.0
    return tuple(jnp.asarray(a, BF16) for a in (rep, selw, expf, expb))


def ssd2(proj, conv_w, conv_b, dtb, alog, dsk, s0, nb, seq):
    gw = R_C * P_C
    nslab = gw // LANES
    slab = lambda base: pl.BlockSpec((None, seq, LANES), lambda b, g: (base + g, b, 0))
    slab4 = lambda base: pl.BlockSpec((nslab, seq, LANES), lambda b, g: (base // nslab + g, b, 0))
    xo, bo, co = 0, D_INNER // LANES, (D_INNER + G_C * N_C) // LANES
    cw = lambda rws, width, off: pl.BlockSpec((rws, width), lambda b, g: (0, off * LANES // width + g))
    vec = pl.BlockSpec((None, 1, LANES), lambda b, g: (g, 0, 0))
    full = lambda a: pl.BlockSpec(a.shape, lambda b, g: (0,) * a.ndim)
    st_in = lambda d: pl.BlockSpec((None, None, None, N_C, gw), lambda b, g: (b, d, g, 0, 0))
    st_out = pl.BlockSpec((None, None, N_C, gw), lambda b, g: (b, g, 0, 0))
    rep, selw, expf, expb = _ssd_constants()
    has_init = s0 is not None
    return pl.pallas_call(
        functools.partial(_ssd2_kernel, seq=seq, has_init=has_init),
        grid=(nb, G_C),
        in_specs=[slab4(OB_X), slab(OB_B), slab(OB_C), slab4(OB_Z),
                  pl.BlockSpec((None, seq, LANES), lambda b, g: (OB_DT, b, 0)),
                  cw(SSD_CONV, gw, xo), cw(SSD_CONV, LANES, bo), cw(SSD_CONV, LANES, co),
                  cw(1, gw, xo), cw(1, LANES, bo), cw(1, LANES, co),
                  vec, vec, pl.BlockSpec((1, gw), lambda b, g: (0, g)),
                  pl.BlockSpec((None, LANES, LANES), lambda b, g: (g, 0, 0)), full(selw), full(expf), full(expb)]
        + ([st_in(0), st_in(1)] if has_init else []),
        out_specs=[pl.BlockSpec((seq, gw), lambda b, g: (b, g)), st_out, st_out],
        out_shape=[jax.ShapeDtypeStruct((nb * seq, D_INNER), F32),
                   jax.ShapeDtypeStruct((nb, G_C, N_C, gw), F32),
                   jax.ShapeDtypeStruct((nb, G_C, N_C, gw), F32)],
        scratch_shapes=[pltpu.VMEM((seq, gw), F32), pltpu.VMEM((seq, LANES), F32), pltpu.VMEM((seq, LANES), F32),
                        pltpu.VMEM((seq, LANES), F32), pltpu.VMEM((seq, LANES), F32), pltpu.VMEM((seq, gw), F32)],
        compiler_params=_cparams(("parallel", "parallel")),
        name="ssd",
    )(proj, proj, proj, proj, proj, conv_w, conv_w, conv_w, conv_b, conv_b, conv_b, dtb, alog, dsk,
      rep, selw, expf, expb, *((s0, s0) if has_init else ()))


def _pair_layout_c(c):
    nb = c.shape[0]
    c = c.transpose(0, 2, 1, 4, 3).reshape(nb, H_A // 2, 2, 2, DK_A, DV_A)
    return c.transpose(0, 1, 3, 2, 4, 5).reshape(nb, H_A // 2, 2, 2 * DK_A, DV_A)


def _pair_layout_c_inv(c):
    nb = c.shape[0]
    c = c.reshape(nb, H_A // 2, 2, 2, DK_A, DV_A).transpose(0, 2, 1, 3, 5, 4)
    return c.reshape(nb, 2, H_A, DV_A, DK_A)


def _pair_layout_n(n):
    nb = n.shape[0]
    return n.reshape(nb, 2, H_A // 2, 2 * DK_A).transpose(0, 2, 1, 3).reshape(nb, H_A // 2, 2, 1, 2 * DK_A)


def _pair_layout_n_inv(n):
    nb = n.shape[0]
    return n.reshape(nb, H_A // 2, 2, 2 * DK_A).transpose(0, 2, 1, 3).reshape(nb, 2, H_A, DK_A)


def _pair_layout_m(m):
    return _pair_layout_n(jnp.repeat(m[..., None], DK_A, axis=-1))


def _pair_layout_m_inv(m):
    return _pair_layout_n_inv(m)[..., 0]


def _ssd_state_layout(s):
    nb = s.shape[0]
    return s.reshape(nb, 2, G_C, R_C, P_C, N_C).transpose(0, 1, 2, 5, 3, 4).reshape(nb, 2, G_C, N_C, R_C * P_C)


def _ssd_state_layout_inv(s):
    nb = s.shape[0]
    return s.reshape(nb, G_C, N_C, R_C, P_C).transpose(0, 1, 3, 4, 2).reshape(nb, H_C, P_C, N_C)


def _group_replicated(v):
    t = v.reshape(2, G_C, R_C).transpose(1, 0, 2).reshape(G_C, 1, SSD_REP)
    return jnp.tile(t, (1, 1, LANES // SSD_REP))


def _pad_cols(w, n):
    return jnp.pad(w, ((0, 0), (0, n - w.shape[1])))


def kernel(x_prompt, x_sample, cache_attn_k, cache_attn_v, state_mlstm_c, state_mlstm_n, state_mlstm_m, state_ssd,
           c, c_ctx, w_ada, b_ada, norm_mix, norm_ffn, w_gate, w_up, w_down, w_in_even, b_gate_mlstm, mlstm_norm,
           q_norm, k_norm, lambda_q1, lambda_k1, lambda_q2, lambda_k2, diff_norm, w_out_even, w_in_odd, conv_w,
           conv_b, dt_bias, a_log, d_skip, ssd_norm, w_out_odd):
    nb_c, seq_c, d = x_prompt.shape
    nb_d, seq_d, _ = x_sample.shape
    n_ctx, n_dec = nb_c * seq_c, nb_d * seq_d
    depth = w_ada.shape[0]

    xc, xd = x_prompt.reshape(n_ctx, d), x_sample.reshape(n_dec, d)
    cvecs = jnp.concatenate([c_ctx[None, :], c, jnp.zeros((8 - 1 - nb_d, d), F32)], axis=0)
    mods = ada_modulation(cvecs, w_ada, b_ada)

    def mod(l, k):
        return mods[l, :, k * d:(k + 1) * d].reshape(8, 1, d)

    cos, sin = rope_tables(seq_d)
    new_k, new_v, new_c, new_n, new_m, new_s = [], [], [], [], [], []
    for l in range(depth):
        i = l // 2
        sh1, sc1, g1, sh2, sc2, g2 = (mod(l, k) for k in range(6))
        if l % 2 == 0:
            w = w_in_even[i]
            gate0 = 2 * A_QK + 2 * A_V
            w = jnp.concatenate([w[:, :gate0], w[:, gate0 + 4 * H_A:], w[:, gate0:gate0 + 4 * H_A]], axis=1)
            wb16 = _pad_cols(w, E_COLS).astype(BF16)
            gmix = norm_mix[l][None]
            pc = in_projection(xc, gmix, sc1, sh1, wb16, E_TN, None)
            pd = in_projection(xd, gmix, sc1, sh1, wb16, E_TN, seq_d)
            bias = jnp.pad(b_gate_mlstm[i][None], ((0, 0), (0, LANES - 4 * H_A)))
            gain_a = mlstm_norm[i][None]
            zc = jnp.zeros((nb_c, H_A // 2, 2, LANES, LANES), F32)
            zv = jnp.zeros((nb_c, H_A // 2, 2, 1, LANES), F32)
            h_c, cf, nf, mf = mlstm(pc, bias, gain_a, zc, zv, zv, 0, nb_c, seq_c)
            h_d, _, _, _ = mlstm(pd, bias, gain_a, _pair_layout_c(state_mlstm_c[:, i]),
                                 _pair_layout_n(state_mlstm_n[:, i]), _pair_layout_m(state_mlstm_m[:, i]),
                                 0, nb_d, seq_d)
            new_c.append(_pair_layout_c_inv(cf))
            new_n.append(_pair_layout_n_inv(nf[:, :, :, 0]))
            new_m.append(_pair_layout_m_inv(mf[:, :, :, 0]))

            lam_init = 0.8 - 0.6 * math.exp(-0.3 * l)
            lam_p = jnp.stack([lambda_q1[i], lambda_k1[i], lambda_q2[i], lambda_k2[i]], axis=0)
            qg = jnp.tile(q_norm[i][None], (1, 2))
            kg = jnp.tile(k_norm[i][None], (1, 2))
            dg = diff_norm[i][None]
            o_c, kd, vd = diff_attention_ctx(pc, lam_p, qg, kg, dg, nb_c, seq_c, lam_init)
            past = cache_attn_k.shape[2]
            o_d = diff_attention_dec(pd, cache_attn_k[:, i].reshape(nb_d, past, B_QK),
                                     cache_attn_v[:, i].reshape(nb_d, past, B_V), lam_p, qg, kg, dg, cos, sin,
                                     0, nb_d, seq_d, lam_init)
            new_k.append(kd.reshape(nb_c, seq_c, H_B, 2, DB))
            new_v.append(vd.reshape(nb_c, seq_c, H_B, DVB))
            wo = w_out_even[i].astype(BF16)
            xc = out_projection_even(xc, g1, h_c, o_c, wo[:A_V], wo[A_V:], None)
            xd = out_projection_even(xd, g1, h_d, o_d, wo[:A_V], wo[A_V:], seq_d)
        else:
            wb16 = _pad_cols(w_in_odd[i], O_COLS).astype(BF16)
            gmix = norm_mix[l][None]
            pc = in_projection(xc, gmix, sc1, sh1, wb16, O_TN, None)
            pd = in_projection(xd, gmix, sc1, sh1, wb16, O_TN, seq_d)
            dtb, alog = _group_replicated(dt_bias[i]), _group_replicated(a_log[i])
            dsk = jnp.repeat(d_skip[i], P_C)[None]
            cw, cb = conv_w[i], conv_b[i][None]
            y_c, sf, sb = ssd2(pc, cw, cb, dtb, alog, dsk, None, nb_c, seq_c)
            y_d, _, _ = ssd2(pd, cw, cb, dtb, alog, dsk, _ssd_state_layout(state_ssd[:, i]), nb_d, seq_d)
            new_s.append(jnp.stack([_ssd_state_layout_inv(sf), _ssd_state_layout_inv(sb)], axis=1))
            gs, wo = ssd_norm[i][None], w_out_odd[i].astype(BF16)
            xc = out_projection_odd(xc, g1, y_c, gs, wo, None)
            xd = out_projection_odd(xd, g1, y_d, gs, wo, seq_d)
        ffn_w = (w_gate[l].astype(BF16), w_up[l].astype(BF16), w_down[l].astype(BF16))
        xc = ffn(xc, norm_ffn[l][None], sc2, sh2, g2, *ffn_w, None)
        xd = ffn(xd, norm_ffn[l][None], sc2, sh2, g2, *ffn_w, seq_d)

    y_prompt = xc.reshape(nb_c, seq_c, d)
    y_sample = xd.reshape(nb_d, seq_d, d)
    return (y_prompt, y_sample, jnp.stack(new_k, axis=1), jnp.stack(new_v, axis=1), jnp.stack(new_c, axis=1),
            jnp.stack(new_n, axis=1), jnp.stack(new_m, axis=1), jnp.stack(new_s, axis=1))
```

```python
import functools
import math

import jax
import jax.numpy as jnp
import numpy as np
from jax import lax
from jax.experimental import pallas as pl
from jax.experimental.pallas import tpu as pltpu

F32 = jnp.float32
BF16 = jnp.bfloat16

D_MODEL = 2048
DEPTH = 4
GRID_W = 64
RMS_EPS = 1e-6
H_A, DK_A, DV_A, MLSTM_CHUNK = 8, 64, 128, 64
H_B, DB, DVB = 8, 64, 128
ROPE_THETA = 10000.0
ROPE_PAIRS = DB // 4
D_INNER = 2 * D_MODEL
P_C, N_C, G_C = 64, 128, 8
H_C = D_INNER // P_C
R_C = H_C // G_C
SSD_CONV = 4
SSD_CHUNK = 128
D_FF = -(-8 * D_MODEL // (3 * 256)) * 256
A_QK, A_V = H_A * DK_A, H_A * DV_A
B_QK, B_V = H_B * 2 * DB, H_B * DVB
CONV_CH = D_INNER + 2 * G_C * N_C

LANES = 128
VMEM_LIMIT = 56 * 1024 * 1024

TM = 512
INP_TM = 1024
FFN_TM = 512
ADA_TN = 1024
E_COLS = 6400
E_TN = 1280
O_COLS = 10752
O_TN = 1536
FF_TN = 512
OUT_TN = 512
ATT_TQ = 256
MLSTM_UNROLL = 2

EB_Q, EB_K, EB_V, EB_O = 0, 4, 8, 16
EB_BQ, EB_BK, EB_BV, EB_G = 24, 32, 40, 48
OB_Z, OB_X, OB_B, OB_C, OB_DT = 0, 32, 64, 72, 80


def _cparams(sem):
    return pltpu.CompilerParams(dimension_semantics=sem, vmem_limit_bytes=VMEM_LIMIT)


def _silu(x):
    return x * jax.nn.sigmoid(x)


def _split3(x):
    hi = x.astype(BF16)
    r1 = x - hi.astype(F32)
    mid = r1.astype(BF16)
    lo = (r1 - mid.astype(F32)).astype(BF16)
    return hi, mid, lo


def _dot(a, b):
    return jnp.dot(a, b, preferred_element_type=F32)


def _dot3_l(x, m):
    hi, mid, lo = _split3(x)
    return _dot(hi, m) + _dot(mid, m) + _dot(lo, m)


def _dot3_r(m, x):
    hi, mid, lo = _split3(x)
    return _dot(m, hi) + _dot(m, mid) + _dot(m, lo)


def _row_group(i, tm, mod_rows):
    return 0 if mod_rows is None else 1 + (i * tm) // mod_rows


def _ada_kernel(c_ref, w_ref, b_ref, o_ref):
    s = _silu(c_ref[...]).astype(BF16)
    o_ref[...] = _dot(s, w_ref[...].astype(BF16)) + b_ref[...]


def ada_modulation(cvecs, w_ada, b_ada):
    depth, d, n = w_ada.shape
    return pl.pallas_call(
        _ada_kernel,
        grid=(depth, n // ADA_TN),
        in_specs=[
            pl.BlockSpec((8, d), lambda l, j: (0, 0)),
            pl.BlockSpec((None, d, ADA_TN), lambda l, j: (l, 0, j)),
            pl.BlockSpec((None, 1, ADA_TN), lambda l, j: (l, 0, j)),
        ],
        out_specs=pl.BlockSpec((None, 8, ADA_TN), lambda l, j: (l, 0, j)),
        out_shape=jax.ShapeDtypeStruct((depth, 8, n), F32),
        compiler_params=_cparams(("parallel", "parallel")),
        name="ada_modulation",
    )(cvecs, w_ada, b_ada.reshape(depth, 1, n))


def _norm_mod(x, gain, sc, sh):
    ms = jnp.mean(x * x, axis=-1, keepdims=True)
    y = x * lax.rsqrt(ms + RMS_EPS) * gain
    return y * (1.0 + sc) + sh


def _inproj_kernel(x_ref, gain_ref, sc_ref, sh_ref, w_ref, o_ref, h_ref, *, nblk):
    @pl.when(pl.program_id(1) == 0)
    def _():
        h_ref[...] = _norm_mod(x_ref[...], gain_ref[...], sc_ref[...], sh_ref[...]).astype(BF16)

    acc = _dot(h_ref[...], w_ref[...])
    for k in range(nblk):
        o_ref[k] = acc[:, k * LANES:(k + 1) * LANES]


def in_projection(x, gain, sc, sh, w, tn, mod_rows):
    m, d = x.shape
    n = w.shape[1]
    nblk = tn // LANES
    tm = INP_TM
    grp = lambda i, j: (_row_group(i, tm, mod_rows), 0, 0)
    return pl.pallas_call(
        functools.partial(_inproj_kernel, nblk=nblk),
        grid=(m // tm, n // tn),
        in_specs=[
            pl.BlockSpec((tm, d), lambda i, j: (i, 0)),
            pl.BlockSpec((1, d), lambda i, j: (0, 0)),
            pl.BlockSpec((None, 1, d), grp),
            pl.BlockSpec((None, 1, d), grp),
            pl.BlockSpec((d, tn), lambda i, j: (0, j)),
        ],
        out_specs=pl.BlockSpec((nblk, tm, LANES), lambda i, j: (j, i, 0)),
        out_shape=jax.ShapeDtypeStruct((n // LANES, m, LANES), F32),
        scratch_shapes=[pltpu.VMEM((tm, d), BF16)],
        compiler_params=_cparams(("parallel", "arbitrary")),
        name="in_projection",
    )(x, gain, sc, sh, w)


def _ffn_kernel(x_ref, gain_ref, sc_ref, sh_ref, g_ref, wg_ref, wu_ref, wd_ref, o_ref, h_ref, acc_ref):
    f = pl.program_id(1)

    @pl.when(f == 0)
    def _():
        h_ref[...] = _norm_mod(x_ref[...], gain_ref[...], sc_ref[...], sh_ref[...]).astype(BF16)
        acc_ref[...] = jnp.zeros_like(acc_ref)

    h = h_ref[...]
    gate = _dot(h, wg_ref[...])
    up = _dot(h, wu_ref[...])
    ff = (_silu(gate) * up).astype(BF16)
    acc_ref[...] += _dot(ff, wd_ref[...])

    @pl.when(f == pl.num_programs(1) - 1)
    def _():
        o_ref[...] = x_ref[...] + g_ref[...] * acc_ref[...]


def ffn(x, gain, sc, sh, g, wg, wu, wd, mod_rows):
    m, d = x.shape
    dff = wg.shape[1]
    tm = FFN_TM
    grp = lambda i, f: (_row_group(i, tm, mod_rows), 0, 0)
    return pl.pallas_call(
        _ffn_kernel,
        grid=(m // tm, dff // FF_TN),
        in_specs=[
            pl.BlockSpec((tm, d), lambda i, f: (i, 0)),
            pl.BlockSpec((1, d), lambda i, f: (0, 0)),
            pl.BlockSpec((None, 1, d), grp),
            pl.BlockSpec((None, 1, d), grp),
            pl.BlockSpec((None, 1, d), grp),
            pl.BlockSpec((d, FF_TN), lambda i, f: (0, f)),
            pl.BlockSpec((d, FF_TN), lambda i, f: (0, f)),
            pl.BlockSpec((FF_TN, d), lambda i, f: (f, 0)),
        ],
        out_specs=pl.BlockSpec((tm, d), lambda i, f: (i, 0)),
        out_shape=jax.ShapeDtypeStruct((m, d), F32),
        scratch_shapes=[pltpu.VMEM((tm, d), BF16), pltpu.VMEM((tm, d), F32)],
        compiler_params=_cparams(("parallel", "arbitrary")),
        name="ffn",
    )(x, gain, sc, sh, g, wg, wu, wd)


def _outproj_even_kernel(x_ref, g_ref, a_ref, b_ref, wa_ref, wb_ref, o_ref):
    acc = _dot(a_ref[...], wa_ref[...]) + _dot(b_ref[...], wb_ref[...])
    o_ref[...] = x_ref[...] + g_ref[...] * acc


def out_projection_even(x, g, ha, ob, wa, wb, mod_rows):
    m, d = x.shape
    ka, kb = ha.shape[1], ob.shape[1]
    grp = lambda i, j: (_row_group(i, TM, mod_rows), 0, j)
    return pl.pallas_call(
        _outproj_even_kernel,
        grid=(m // TM, d // OUT_TN),
        in_specs=[
            pl.BlockSpec((TM, OUT_TN), lambda i, j: (i, j)),
            pl.BlockSpec((None, 1, OUT_TN), grp),
            pl.BlockSpec((TM, ka), lambda i, j: (i, 0)),
            pl.BlockSpec((TM, kb), lambda i, j: (i, 0)),
            pl.BlockSpec((ka, OUT_TN), lambda i, j: (0, j)),
            pl.BlockSpec((kb, OUT_TN), lambda i, j: (0, j)),
        ],
        out_specs=pl.BlockSpec((TM, OUT_TN), lambda i, j: (i, j)),
        out_shape=jax.ShapeDtypeStruct((m, d), F32),
        compiler_params=_cparams(("parallel", "arbitrary")),
        name="out_projection_even",
    )(x, g, ha, ob, wa, wb)


def _outproj_odd_kernel(x_ref, g_ref, y_ref, gain_ref, w_ref, o_ref, yn_ref):
    @pl.when(pl.program_id(1) == 0)
    def _():
        y = y_ref[...]
        ms = jnp.mean(y * y, axis=-1, keepdims=True)
        yn_ref[...] = (y * lax.rsqrt(ms + RMS_EPS) * gain_ref[...]).astype(BF16)

    o_ref[...] = x_ref[...] + g_ref[...] * _dot(yn_ref[...], w_ref[...])


def out_projection_odd(x, g, y, gain, w, mod_rows):
    m, d = x.shape
    k = y.shape[1]
    grp = lambda i, j: (_row_group(i, TM, mod_rows), 0, j)
    return pl.pallas_call(
        _outproj_odd_kernel,
        grid=(m // TM, d // OUT_TN),
        in_specs=[
            pl.BlockSpec((TM, OUT_TN), lambda i, j: (i, j)),
            pl.BlockSpec((None, 1, OUT_TN), grp),
            pl.BlockSpec((TM, k), lambda i, j: (i, 0)),
            pl.BlockSpec((1, k), lambda i, j: (0, 0)),
            pl.BlockSpec((k, OUT_TN), lambda i, j: (0, j)),
        ],
        out_specs=pl.BlockSpec((TM, OUT_TN), lambda i, j: (i, j)),
        out_shape=jax.ShapeDtypeStruct((m, d), F32),
        scratch_shapes=[pltpu.VMEM((TM, k), BF16)],
        compiler_params=_cparams(("parallel", "arbitrary")),
        name="out_projection_odd",
    )(x, g, y, gain, w)


def _lane_iota(shape):
    return lax.broadcasted_iota(jnp.int32, shape, len(shape) - 1)


def _half_rms_norm(x, gain):
    lo = _lane_iota(x.shape) < DB
    x2 = x * x
    s0 = jnp.sum(jnp.where(lo, x2, 0.0), axis=-1, keepdims=True)
    s1 = jnp.sum(jnp.where(lo, 0.0, x2), axis=-1, keepdims=True)
    r = jnp.where(lo, lax.rsqrt(s0 * (1.0 / DB) + RMS_EPS), lax.rsqrt(s1 * (1.0 / DB) + RMS_EPS))
    return x * r * gain


def _rope(x, cos, sin_signed):
    first = (_lane_iota(x.shape) % 32) < ROPE_PAIRS
    partner = jnp.where(first, pltpu.roll(x, LANES - ROPE_PAIRS, 1), pltpu.roll(x, ROPE_PAIRS, 1))
    return x * cos + partner * sin_signed


def _softmax_pv(s, vv):
    p = jnp.exp(s - jnp.max(s, axis=-1, keepdims=True))
    return _dot(p.astype(BF16), vv) * (1.0 / jnp.sum(p, axis=-1, keepdims=True))


def _diff_attn_core(q, kk, vv, lam, dgain, lam_init):
    lo = _lane_iota(q.shape) < DB
    qs = q * (DB ** -0.5)
    q1 = jnp.where(lo, qs, 0.0).astype(BF16)
    q2 = jnp.where(lo, 0.0, qs).astype(BF16)
    nt = (((1,), (1,)), ((), ()))
    s1 = lax.dot_general(q1, kk, nt, preferred_element_type=F32)
    s2 = lax.dot_general(q2, kk, nt, preferred_element_type=F32)
    o = _softmax_pv(s1, vv) - lam * _softmax_pv(s2, vv)
    ms = jnp.mean(o * o, axis=-1, keepdims=True)
    return o * lax.rsqrt(ms + RMS_EPS) * dgain * (1.0 - lam_init)


def _lambda_value(lam_ref, lam_init):
    lp = lam_ref[...]
    a = jnp.sum(lp[0:1] * lp[1:2], axis=-1, keepdims=True)
    b = jnp.sum(lp[2:3] * lp[3:4], axis=-1, keepdims=True)
    return jnp.exp(a) - jnp.exp(b) + lam_init


def _attn_ctx_kernel(lam_ref, q_ref, k_ref, v_ref, qg_ref, kg_ref, dg_ref, o_ref, kd_ref, vd_ref, *, lam_init):
    lam = _lambda_value(lam_ref, lam_init)
    for h in range(H_B):
        cs = slice(h * LANES, (h + 1) * LANES)
        kd = _half_rms_norm(k_ref[h], kg_ref[...])
        kd_ref[:, cs] = kd
        v = v_ref[h]
        vd_ref[:, cs] = v
        q = _half_rms_norm(q_ref[h], qg_ref[...])
        o = _diff_attn_core(q, kd.astype(BF16), v.astype(BF16), lam, dg_ref[...], lam_init)
        o_ref[:, cs] = o.astype(BF16)


def diff_attention_ctx(proj, lam_p, qg, kg, dg, nb, seq, lam_init):
    m = nb * seq
    vec = pl.BlockSpec((1, LANES), lambda b: (0, 0))
    slab = lambda base: pl.BlockSpec((H_B, seq, LANES), lambda b: (base // H_B, b, 0))
    outb = pl.BlockSpec((seq, H_B * LANES), lambda b: (b, 0))
    return pl.pallas_call(
        functools.partial(_attn_ctx_kernel, lam_init=lam_init),
        grid=(nb,),
        in_specs=[pl.BlockSpec((4, DB), lambda b: (0, 0)), slab(EB_BQ), slab(EB_BK), slab(EB_BV), vec, vec, vec],
        out_specs=[outb, outb, outb],
        out_shape=[jax.ShapeDtypeStruct((m, B_V), BF16), jax.ShapeDtypeStruct((m, B_QK), F32),
                   jax.ShapeDtypeStruct((m, B_V), F32)],
        compiler_params=_cparams(("parallel",)),
        name="diff_attention_ctx",
    )(lam_p, proj, proj, proj, qg, kg, dg)


def _attn_dec_kernel(lam_ref, q_ref, k_ref, v_ref, kc_ref, vc_ref, qg_ref, kg_ref, dg_ref, cq_ref, sq_ref, ck_ref,
                     sk_ref, o_ref, ks_ref, vs_ref, *, lam_init, past):
    @pl.when(pl.program_id(2) == 0)
    def _():
        ks_ref[0:past, :] = kc_ref[...].astype(BF16)
        vs_ref[0:past, :] = vc_ref[...].astype(BF16)
        kd = _rope(_half_rms_norm(k_ref[...], kg_ref[...]), ck_ref[...], sk_ref[...])
        ks_ref[past:, :] = kd.astype(BF16)
        vs_ref[past:, :] = v_ref[...].astype(BF16)

    q = _rope(_half_rms_norm(q_ref[...], qg_ref[...]), cq_ref[...], sq_ref[...])
    lam = _lambda_value(lam_ref, lam_init)
    o_ref[...] = _diff_attn_core(q, ks_ref[...], vs_ref[...], lam, dg_ref[...], lam_init).astype(BF16)


def diff_attention_dec(proj, cache_k, cache_v, lam_p, qg, kg, dg, cos, sin, row0, nb, seq, lam_init):
    past = cache_k.shape[1]
    nq = seq // ATT_TQ
    rq, rk = row0 // ATT_TQ, row0 // seq
    vec = pl.BlockSpec((1, LANES), lambda b, h, i: (0, 0))
    kslab = lambda base: pl.BlockSpec((None, seq, LANES), lambda b, h, i: (base + h, rk + b, 0))
    cslab = pl.BlockSpec((None, past, LANES), lambda b, h, i: (b, 0, h))
    tq = pl.BlockSpec((ATT_TQ, LANES), lambda b, h, i: (i, 0))
    tk = pl.BlockSpec((seq, LANES), lambda b, h, i: (0, 0))
    return pl.pallas_call(
        functools.partial(_attn_dec_kernel, lam_init=lam_init, past=past),
        grid=(nb, H_B, nq),
        in_specs=[pl.BlockSpec((4, DB), lambda b, h, i: (0, 0)),
                  pl.BlockSpec((None, ATT_TQ, LANES), lambda b, h, i: (EB_BQ + h, rq + b * nq + i, 0)),
                  kslab(EB_BK), kslab(EB_BV), cslab, cslab, vec, vec, vec, tq, tq, tk, tk],
        out_specs=pl.BlockSpec((ATT_TQ, LANES), lambda b, h, i: (b * nq + i, h)),
        out_shape=jax.ShapeDtypeStruct((nb * seq, B_V), BF16),
        scratch_shapes=[pltpu.VMEM((past + seq, LANES), BF16), pltpu.VMEM((past + seq, LANES), BF16)],
        compiler_params=_cparams(("parallel", "parallel", "arbitrary")),
        name="diff_attention_dec",
    )(lam_p, proj, proj, proj, cache_k, cache_v, qg, kg, dg, cos, sin, cos, sin)


def rope_tables(n_tokens):
    n_rows = n_tokens // GRID_W
    rows, cols = jnp.meshgrid(jnp.arange(n_rows, dtype=F32), jnp.arange(GRID_W, dtype=F32), indexing='ij')
    inv_freq = ROPE_THETA ** (-jnp.arange(ROPE_PAIRS, dtype=F32) / ROPE_PAIRS)
    ang = jnp.stack([rows.reshape(-1, 1) * inv_freq, cols.reshape(-1, 1) * inv_freq], axis=0)
    cos, sin = jnp.cos(ang), jnp.sin(ang)
    cos_half = jnp.concatenate([cos[0], cos[0], cos[1], cos[1]], axis=-1)
    sin_half = jnp.concatenate([-sin[0], sin[0], -sin[1], sin[1]], axis=-1)
    return jnp.tile(cos_half, (1, 2)), jnp.tile(sin_half, (1, 2))


def _tri(n, lower):
    r = lax.broadcasted_iota(jnp.int32, (n, n), 0)
    c = lax.broadcasted_iota(jnp.int32, (n, n), 1)
    return jnp.where((r >= c) if lower else (r <= c), 1.0, 0.0).astype(BF16)


def _mlstm_direction(d, rows, q_ref, k_ref, v_ref, gcol, grow, qk, c_ref, n_prev, m_prev, tri_l, tri_u):
    lc = MLSTM_CHUNK
    ri = lax.broadcasted_iota(jnp.int32, (lc, lc), 0)
    ci = lax.broadcasted_iota(jnp.int32, (lc, lc), 1)
    mask = (ci <= ri) if d == 0 else (ci >= ri)
    lane = _lane_iota((lc, LANES))
    lo = lane < DK_A
    fcol = jax.nn.log_sigmoid(gcol)
    frow = jax.nn.log_sigmoid(grow)
    bcol_all = _dot3_r(tri_l if d == 0 else tri_u, fcol)
    brow_all = _dot3_l(frow, tri_u if d == 0 else tri_l)
    q = q_ref[rows, :]
    k = k_ref[rows, :] * (DK_A ** -0.5)
    cst = c_ref[d]
    qn = q * n_prev
    hs, m_news, e_cols, decays = [], [], [], []
    for j in range(2):
        li_lane, lf_lane = 8 * d + j, 16 + 8 * d + j
        b_col = bcol_all[:, lf_lane:lf_lane + 1]
        b_row = brow_all[lf_lane:lf_lane + 1, :]
        li_col = gcol[:, li_lane:li_lane + 1]
        li_row = grow[li_lane:li_lane + 1, :]
        m_p = m_prev[j]
        dmat = jnp.where(mask, b_col - b_row + li_row, -jnp.inf)
        inter = b_col + m_p
        m_row = jnp.maximum(inter, jnp.max(dmat, axis=-1, keepdims=True))
        w_inter = jnp.exp(inter - m_row)
        s = qk[j] * jnp.exp(dmat - m_row)
        qm = jnp.where(lo, q, 0.0) if j == 0 else jnp.where(lo, 0.0, q)
        vj = v_ref[j, rows, :].astype(BF16)
        num = w_inter * _dot(qm.astype(BF16), cst.astype(BF16)) + _dot(s.astype(BF16), vj)
        qn_j = jnp.sum(jnp.where(lo, qn, 0.0) if j == 0 else jnp.where(lo, 0.0, qn), axis=-1, keepdims=True)
        den = w_inter * qn_j + jnp.sum(s, axis=-1, keepdims=True)
        hs.append(num * (1.0 / jnp.maximum(jnp.abs(den), jnp.exp(-m_row))))
        b_end = b_col[lc - 1:lc, :] if d == 0 else b_col[0:1, :]
        g_col = b_end - b_col + li_col
        m_new = jnp.maximum(b_end + m_p, jnp.max(g_col, axis=0, keepdims=True))
        decays.append(jnp.exp(b_end + m_p - m_new))
        e_cols.append(jnp.exp(g_col - m_new))
        m_news.append(m_new)
    wk = k * jnp.where(lo, e_cols[0], e_cols[1])
    wkb = wk.astype(BF16)
    tn = (((0,), (0,)), ((), ()))
    u0 = lax.dot_general(wkb, v_ref[0, rows, :].astype(BF16), tn, preferred_element_type=F32)
    u1 = lax.dot_general(wkb, v_ref[1, rows, :].astype(BF16), tn, preferred_element_type=F32)
    top = lax.broadcasted_iota(jnp.int32, (LANES, LANES), 0) < DK_A
    c_ref[d] = jnp.where(top, decays[0], decays[1]) * cst + jnp.where(top, u0, u1)
    lane1 = _lane_iota((1, LANES)) < DK_A
    n_new = jnp.where(lane1, decays[0], decays[1]) * n_prev + jnp.sum(wk, axis=0, keepdims=True)
    return jnp.concatenate(hs, axis=-1), n_new, m_news


def _mlstm_kernel(q_ref, k_ref, v_ref, ao_ref, g_ref, bias_ref, gain_ref, c0_ref, n0_ref, m0_ref,
                  h_ref, cf_ref, nf_ref, mf_ref, gc_ref, gt_ref, hf_ref, hb_ref, *, seq):
    lc = MLSTM_CHUNK
    nc = seq // lc
    pair = pl.program_id(1)
    shift = (LANES - 2 * pair) % LANES
    bias = pltpu.roll(jnp.broadcast_to(bias_ref[...], (8, LANES)), shift, 1)[0:1]
    for blk in range(seq // LANES):
        r = slice(blk * LANES, (blk + 1) * LANES)
        gs = pltpu.roll(g_ref[r, :], shift, 1) + bias
        gc_ref[r, :] = gs
        gst = gs.T
        for half in range(LANES // lc):
            gt_ref[blk * (LANES // lc) + half] = gst[0:32, half * lc:(half + 1) * lc]
    cf_ref[...] = c0_ref[...]
    tri_l, tri_u = _tri(lc, True), _tri(lc, False)
    nt = (((1,), (1,)), ((), ()))
    lo = _lane_iota((lc, LANES)) < DK_A

    def chunk_qk(rows):
        q = q_ref[rows, :]
        kb = (k_ref[rows, :] * (DK_A ** -0.5)).astype(BF16)
        return [lax.dot_general(jnp.where(lo, q, 0.0).astype(BF16), kb, nt, preferred_element_type=F32),
                lax.dot_general(jnp.where(lo, 0.0, q).astype(BF16), kb, nt, preferred_element_type=F32)]

    def body(c, carry):
        n_f, m_f0, m_f1, n_b, m_b0, m_b1 = carry
        rf = pl.ds(pl.multiple_of(c * lc, lc), lc)
        rb = pl.ds(pl.multiple_of((nc - 1 - c) * lc, lc), lc)
        h_f, n_f, m_f = _mlstm_direction(0, rf, q_ref, k_ref, v_ref, gc_ref[rf, :], gt_ref[c], chunk_qk(rf),
                                         cf_ref, n_f, (m_f0, m_f1), tri_l, tri_u)
        hf_ref[rf, :] = h_f
        h_b, n_b, m_b = _mlstm_direction(1, rb, q_ref, k_ref, v_ref, gc_ref[rb, :], gt_ref[nc - 1 - c], chunk_qk(rb),
                                         cf_ref, n_b, (m_b0, m_b1), tri_l, tri_u)
        hb_ref[rb, :] = h_b
        return n_f, m_f[0], m_f[1], n_b, m_b[0], m_b[1]

    m0 = m0_ref[...]
    init = (n0_ref[0], m0[0][:, 0:1], m0[0][:, DK_A:DK_A + 1], n0_ref[1], m0[1][:, 0:1], m0[1][:, DK_A:DK_A + 1])
    n_f, m_f0, m_f1, n_b, m_b0, m_b1 = lax.fori_loop(0, nc, body, init, unroll=MLSTM_UNROLL)
    nf_ref[0] = n_f
    nf_ref[1] = n_b
    lane1 = _lane_iota((1, LANES)) < DK_A
    mf_ref[0] = jnp.where(lane1, m_f0, m_f1)
    mf_ref[1] = jnp.where(lane1, m_b0, m_b1)

    gain = gain_ref[...]
    for blk in range(seq // LANES):
        r = slice(blk * LANES, (blk + 1) * LANES)
        outs = []
        for j in range(2):
            cs = slice(j * DV_A, (j + 1) * DV_A)
            hh = hf_ref[r, cs] + hb_ref[r, cs]
            ms = jnp.mean(hh * hh, axis=-1, keepdims=True)
            outs.append(hh * lax.rsqrt(ms + RMS_EPS) * gain[:, cs] * jax.nn.sigmoid(ao_ref[j, r, :]))
        h_ref[r, :] = jnp.concatenate(outs, axis=-1).astype(BF16)


def mlstm(proj, bias, gain, c0, n0, m0, row0, nb, seq):
    rb = row0 // seq
    npair = H_A // 2
    slab = lambda base: pl.BlockSpec((None, seq, LANES), lambda b, p: (base + p, rb + b, 0))
    slab2 = lambda base: pl.BlockSpec((2, seq, LANES), lambda b, p: (base // 2 + p, rb + b, 0))
    st_c = pl.BlockSpec((None, None, 2, LANES, LANES), lambda b, p: (b, p, 0, 0, 0))
    st_v = pl.BlockSpec((None, None, 2, 1, LANES), lambda b, p: (b, p, 0, 0, 0))
    return pl.pallas_call(
        functools.partial(_mlstm_kernel, seq=seq),
        grid=(nb, npair),
        in_specs=[slab(EB_Q), slab(EB_K), slab2(EB_V), slab2(EB_O),
                  pl.BlockSpec((None, seq, LANES), lambda b, p: (EB_G, rb + b, 0)),
                  pl.BlockSpec((1, LANES), lambda b, p: (0, 0)),
                  pl.BlockSpec((1, 2 * DV_A), lambda b, p: (0, p)),
                  st_c, st_v, st_v],
        out_specs=[pl.BlockSpec((seq, 2 * DV_A), lambda b, p: (b, p)), st_c, st_v, st_v],
        out_shape=[jax.ShapeDtypeStruct((nb * seq, A_V), BF16),
                   jax.ShapeDtypeStruct((nb, npair, 2, LANES, LANES), F32),
                   jax.ShapeDtypeStruct((nb, npair, 2, 1, LANES), F32),
                   jax.ShapeDtypeStruct((nb, npair, 2, 1, LANES), F32)],
        scratch_shapes=[pltpu.VMEM((seq, LANES), F32), pltpu.VMEM((seq // MLSTM_CHUNK, 32, MLSTM_CHUNK), F32),
                        pltpu.VMEM((seq, 2 * DV_A), F32),
                        pltpu.VMEM((seq, 2 * DV_A), F32)],
        compiler_params=_cparams(("parallel", "parallel")),
        name="mlstm",
    )(proj, proj, proj, proj, proj, bias, gain, c0, n0, m0)


def _softplus(x):
    return jnp.maximum(x, 0.0) + jnp.log1p(jnp.exp(-jnp.abs(x)))


SSD_REP = 2 * R_C


def _pack3(x, lane):
    hi = x.astype(BF16).astype(F32)
    r1 = x - hi
    mid = r1.astype(BF16).astype(F32)
    r2 = r1 - mid
    grp = lane % (3 * SSD_REP)
    return jnp.where(grp < SSD_REP, x, jnp.where(grp < 2 * SSD_REP, r1, r2)).astype(BF16)


SSD_HALO = 8


def _ssd_kernel(xs_ref, bm_ref, cm_ref, z_ref, dtr_ref, wx_ref, wb_ref, wc_ref, bx_ref, bb_ref, bc_ref,
                 dtb_ref, alog_ref, dsk_ref, rep_ref, selw_ref, expf_ref, expb_ref, *rest, seq, has_init):
    if has_init:
        s0_ref = rest[0]
        rest = rest[1:]
    (y_ref, sfin_ref, pad_ref, xa_ref, ba_ref, ca_ref, dt_ref, af_ref, ab_ref, st_ref, ex_ref, abw_ref,
     at_ref, cb_ref) = rest
    lc = SSD_CHUNK
    nc = seq // lc
    gw = R_C * P_C
    nslab = gw // LANES
    lane = _lane_iota((lc, LANES))
    lo = lane < P_C
    fwd_lane = (lane % SSD_REP) < R_C
    first3 = lane < 3 * SSD_REP
    a_neg = -jnp.exp(alog_ref[...])
    tri_l, tri_u = _tri(lc, True), _tri(lc, False)
    ri = lax.broadcasted_iota(jnp.int32, (lc, lc), 0)
    ci = lax.broadcasted_iota(jnp.int32, (lc, lc), 1)
    causal, anti = ci <= ri, ci >= ri
    nt = (((1,), (1,)), ((), ()))

    def chunk_rows(c):
        return pl.ds(pl.multiple_of(c * lc, lc), lc)

    halo = jnp.zeros((SSD_HALO, LANES), F32)
    for s in range(nslab + 2):
        pad_ref[s, 0:SSD_HALO, :] = halo
        pad_ref[s, SSD_HALO + seq:, :] = halo
        src = xs_ref[s] if s < nslab else (bm_ref[...] if s == nslab else cm_ref[...])
        pad_ref[s, SSD_HALO:SSD_HALO + seq, :] = src

    def conv_chunk(s, base, w, b):
        y = w[0:1] * pad_ref[s, pl.ds(base + SSD_HALO - 2, lc), :]
        y = y + w[1:2] * pad_ref[s, pl.ds(base + SSD_HALO - 1, lc), :]
        y = y + w[2:3] * pad_ref[s, pl.ds(base + SSD_HALO, lc), :]
        y = y + w[3:4] * pad_ref[s, pl.ds(base + SSD_HALO + 1, lc), :]
        return _silu(y + b)

    def prep_body(c, carry):
        base = pl.multiple_of(c * lc, lc)
        rows = pl.ds(base, lc)
        for s in range(nslab):
            cs = slice(s * LANES, (s + 1) * LANES)
            xa_ref[rows, cs] = conv_chunk(s, base, wx_ref[:, cs], bx_ref[:, cs])
        ba_ref[rows, :] = conv_chunk(nslab, base, wb_ref[...], bb_ref[...])
        ca_ref[rows, :] = conv_chunk(nslab + 1, base, wc_ref[...], bc_ref[...])
        dt = _softplus(_dot3_l(dtr_ref[rows, :], rep_ref[...]) + dtb_ref[...])
        dt_ref[rows, :] = dt
        da = dt * a_neg
        af_ref[rows, :] = _dot3_r(tri_l, da)
        ab_ref[rows, :] = _dot3_r(tri_u, da)
        return carry

    lax.fori_loop(0, nc, prep_body, 0, unroll=2)

    for d in range(2):
        for k in range(nslab):
            cs = slice(k * LANES, (k + 1) * LANES)
            st_ref[d, :, cs] = s0_ref[d, cs, :].T if has_init else jnp.zeros((N_C, LANES), F32)

    def expand(ea, te, e_ref):
        return _dot(_pack3(jnp.where(first3, ea, te), lane), e_ref[...])

    def fwd_a(c, slot):
        rows = chunk_rows(c)
        acum = af_ref[rows, :]
        a_end = acum[lc - 1:lc, :]
        ex_ref[slot] = expand(jnp.exp(acum), jnp.exp(a_end - acum) * dt_ref[rows, :], expf_ref)

    def fwd_b(c, slot):
        rows = chunk_rows(c)
        ea_x, te_x = ex_ref[slot, :, 0:gw], ex_ref[slot, :, gw:2 * gw]
        st = st_ref[0]
        y_ref[rows, :] = _dot(ca_ref[rows, :].astype(BF16), st.astype(BF16)) * ea_x
        bc_t = ba_ref[rows, :].T.astype(BF16)
        st_ref[0] = st * ea_x[lc - 1:lc, :] + _dot(bc_t, (xa_ref[rows, :] * te_x).astype(BF16))

    def bwd_a(c, slot):
        rows = chunk_rows(c)
        dt = dt_ref[rows, :]
        acum_b = ab_ref[rows, :]
        acum = jnp.where(fwd_lane, af_ref[rows, :], acum_b)
        at_ref[slot, 0] = acum.T
        at_ref[slot, 1] = dt.T
        abw_ref[slot] = _dot(_pack3(acum, lane), selw_ref[...])
        a_end = acum_b[0:1, :]
        ex_ref[slot] = expand(jnp.exp(acum_b), jnp.exp(a_end - acum_b) * dt, expb_ref)
        cb_ref[slot] = lax.dot_general(ca_ref[rows, :].astype(BF16), ba_ref[rows, :].astype(BF16), nt,
                                       preferred_element_type=F32)

    def bwd_b(c, slot):
        rows = chunk_rows(c)
        ea_x, te_x = ex_ref[slot, :, 0:gw], ex_ref[slot, :, gw:2 * gw]
        cb = cb_ref[slot]
        st = st_ref[1]
        xs = xa_ref[rows, :]
        y_inter = _dot(ca_ref[rows, :].astype(BF16), st.astype(BF16)) * ea_x
        for s in range(nslab):
            cs = slice(s * LANES, (s + 1) * LANES)
            ws = []
            for j in range(2):
                r = 2 * s + j
                rb = R_C + r
                aj_f, aj_b = at_ref[slot, 0, r:r + 1, :], at_ref[slot, 0, rb:rb + 1, :]
                dt_f, dt_b = at_ref[slot, 1, r:r + 1, :], at_ref[slot, 1, rb:rb + 1, :]
                wf = jnp.exp(jnp.where(causal, abw_ref[slot, :, r * LANES:(r + 1) * LANES] - aj_f, -jnp.inf))
                wb = jnp.exp(jnp.where(anti, abw_ref[slot, :, rb * LANES:(rb + 1) * LANES] - aj_b, -jnp.inf))
                ws.append((cb * (wf * dt_f + wb * dt_b)).astype(BF16))
            xsl = xs[:, cs]
            x2 = jnp.concatenate([jnp.where(lo, xsl, 0.0), jnp.where(lo, 0.0, xsl)], axis=0).astype(BF16)
            y = y_ref[rows, cs] + y_inter[:, cs] + dsk_ref[:, cs] * xsl + _dot(jnp.concatenate(ws, axis=1), x2)
            y_ref[rows, cs] = y * _silu(z_ref[s, rows, :])
        bc_t = ba_ref[rows, :].T.astype(BF16)
        st_ref[1] = st * ea_x[0:1, :] + _dot(bc_t, (xs * te_x).astype(BF16))

    def sweep(stage_a, stage_b, order):
        stage_a(order(0), 0)

        def body(k, carry):
            stage_b(order(2 * k), 0)
            stage_a(order(2 * k + 1), 1)
            stage_b(order(2 * k + 1), 1)
            stage_a(order(jnp.minimum(2 * k + 2, nc - 1)), 0)
            return carry

        lax.fori_loop(0, nc // 2, body, 0)

    sweep(fwd_a, fwd_b, lambda i: i)
    sweep(bwd_a, bwd_b, lambda i: nc - 1 - i)

    for d in range(2):
        for k in range(nslab):
            cs = slice(k * LANES, (k + 1) * LANES)
            sfin_ref[d, cs, :] = st_ref[d, :, cs].T


def ssd(proj, conv_w, conv_b, dtb, alog, dsk, s0, layer, nb, seq):
    gw = R_C * P_C
    nslab = gw // LANES
    slab = lambda base: pl.BlockSpec((None, seq, LANES), lambda b, g: (base + g, b, 0))
    slab4 = lambda base: pl.BlockSpec((nslab, seq, LANES), lambda b, g: (base // nslab + g, b, 0))
    xo, bo, co = 0, D_INNER // LANES, (D_INNER + G_C * N_C) // LANES
    cw = lambda rws, width, off: pl.BlockSpec((rws, width), lambda b, g: (0, off * LANES // width + g))
    vec = pl.BlockSpec((None, 1, LANES), lambda b, g: (g, 0, 0))
    full = lambda a: pl.BlockSpec(a.shape, lambda b, g: (0,) * a.ndim)
    st_in = pl.BlockSpec((None, None, 2, gw, N_C), lambda b, g: (b, layer, 0, g, 0))
    st_out = pl.BlockSpec((None, 2, gw, N_C), lambda b, g: (b, 0, g, 0))
    rep, selw, expf, expb = _ssd_constants()
    has_init = s0 is not None
    st_shape = jax.ShapeDtypeStruct((nb, 2, H_C * P_C, N_C), F32)
    vm = lambda *shape: pltpu.VMEM(shape, F32)
    return pl.pallas_call(
        functools.partial(_ssd_kernel, seq=seq, has_init=has_init),
        grid=(nb, G_C),
        in_specs=[slab4(OB_X), slab(OB_B), slab(OB_C), slab4(OB_Z),
                  pl.BlockSpec((None, seq, LANES), lambda b, g: (OB_DT, b, 0)),
                  cw(SSD_CONV, gw, xo), cw(SSD_CONV, LANES, bo), cw(SSD_CONV, LANES, co),
                  cw(1, gw, xo), cw(1, LANES, bo), cw(1, LANES, co),
                  vec, vec, pl.BlockSpec((1, gw), lambda b, g: (0, g)),
                  pl.BlockSpec((None, LANES, LANES), lambda b, g: (g, 0, 0)), full(selw), full(expf), full(expb)]
        + ([st_in] if has_init else []),
        out_specs=[pl.BlockSpec((seq, gw), lambda b, g: (b, g)), st_out],
        out_shape=[jax.ShapeDtypeStruct((nb * seq, D_INNER), F32), st_shape],
        scratch_shapes=[vm(nslab + 2, seq + 2 * SSD_HALO, LANES), vm(seq, gw), vm(seq, LANES), vm(seq, LANES),
                        vm(seq, LANES), vm(seq, LANES), vm(seq, LANES), vm(2, N_C, gw), vm(2, SSD_CHUNK, 2 * gw),
                        vm(2, SSD_CHUNK, SSD_REP * LANES), vm(2, 2, LANES, SSD_CHUNK), vm(2, SSD_CHUNK, SSD_CHUNK)],
        compiler_params=_cparams(("parallel", "parallel")),
        name="ssd",
    )(proj, proj, proj, proj, proj, conv_w, conv_w, conv_w, conv_b, conv_b, conv_b, dtb, alog, dsk,
      rep, selw, expf, expb, *((s0,) if has_init else ()))


def _ssd_constants():
    gw = R_C * P_C
    rep = np.zeros((G_C, LANES, LANES), np.float32)
    for g in range(G_C):
        for l in range(LANES):
            c = l % SSD_REP
            rep[g, (R_C * g + c) if c < R_C else (H_C + R_C * g + c - R_C), l] = 1.0
    selw = np.zeros((LANES, SSD_REP * LANES), np.float32)
    expf = np.zeros((LANES, 2 * gw), np.float32)
    expb = np.zeros((LANES, 2 * gw), np.float32)
    for l in range(3 * SSD_REP):
        c = l % SSD_REP
        selw[l, c * LANES:(c + 1) * LANES] = 1.0
    for l in range(6 * SSD_REP):
        c, q = l % SSD_REP, l // (3 * SSD_REP)
        if c < R_C:
            expf[l, q * gw + c * P_C:q * gw + (c + 1) * P_C] = 1.0
        else:
            expb[l, q * gw + (c - R_C) * P_C:q * gw + (c - R_C + 1) * P_C] = 1.0
    return tuple(jnp.asarray(a, BF16) for a in (rep, selw, expf, expb))


def _pair_layout_c(c):
    nb = c.shape[0]
    c = c.transpose(0, 2, 1, 4, 3).reshape(nb, H_A // 2, 2, 2, DK_A, DV_A)
    return c.transpose(0, 1, 3, 2, 4, 5).reshape(nb, H_A // 2, 2, 2 * DK_A, DV_A)


def _pair_layout_c_inv(c):
    nb = c.shape[0]
    c = c.reshape(nb, H_A // 2, 2, 2, DK_A, DV_A).transpose(0, 2, 1, 3, 5, 4)
    return c.reshape(nb, 2, H_A, DV_A, DK_A)


def _pair_layout_n(n):
    nb = n.shape[0]
    return n.reshape(nb, 2, H_A // 2, 2 * DK_A).transpose(0, 2, 1, 3).reshape(nb, H_A // 2, 2, 1, 2 * DK_A)


def _pair_layout_n_inv(n):
    nb = n.shape[0]
    return n.reshape(nb, H_A // 2, 2, 2 * DK_A).transpose(0, 2, 1, 3).reshape(nb, 2, H_A, DK_A)


def _pair_layout_m(m):
    return _pair_layout_n(jnp.repeat(m[..., None], DK_A, axis=-1))


def _pair_layout_m_inv(m):
    return _pair_layout_n_inv(m)[..., 0]


def _group_replicated(v):
    t = v.reshape(2, G_C, R_C).transpose(1, 0, 2).reshape(G_C, 1, SSD_REP)
    return jnp.tile(t, (1, 1, LANES // SSD_REP))


def _pad_cols(w, n):
    return jnp.pad(w, ((0, 0), (0, n - w.shape[1])))


def kernel(x_prompt, x_sample, cache_attn_k, cache_attn_v, state_mlstm_c, state_mlstm_n, state_mlstm_m, state_ssd,
           c, c_ctx, w_ada, b_ada, norm_mix, norm_ffn, w_gate, w_up, w_down, w_in_even, b_gate_mlstm, mlstm_norm,
           q_norm, k_norm, lambda_q1, lambda_k1, lambda_q2, lambda_k2, diff_norm, w_out_even, w_in_odd, conv_w,
           conv_b, dt_bias, a_log, d_skip, ssd_norm, w_out_odd):
    nb_c, seq_c, d = x_prompt.shape
    nb_d, seq_d, _ = x_sample.shape
    n_ctx, n_dec = nb_c * seq_c, nb_d * seq_d
    depth = w_ada.shape[0]

    xc, xd = x_prompt.reshape(n_ctx, d), x_sample.reshape(n_dec, d)
    cvecs = jnp.concatenate([c_ctx[None, :], c, jnp.zeros((8 - 1 - nb_d, d), F32)], axis=0)
    mods = ada_modulation(cvecs, w_ada, b_ada)

    def mod(l, k):
        return mods[l, :, k * d:(k + 1) * d].reshape(8, 1, d)

    cos, sin = rope_tables(seq_d)
    new_k, new_v, new_c, new_n, new_m, new_s = [], [], [], [], [], []
    for l in range(depth):
        i = l // 2
        sh1, sc1, g1, sh2, sc2, g2 = (mod(l, k) for k in range(6))
        if l % 2 == 0:
            w = w_in_even[i]
            gate0 = 2 * A_QK + 2 * A_V
            w = jnp.concatenate([w[:, :gate0], w[:, gate0 + 4 * H_A:], w[:, gate0:gate0 + 4 * H_A]], axis=1)
            wb16 = _pad_cols(w, E_COLS).astype(BF16)
            gmix = norm_mix[l][None]
            pc = in_projection(xc, gmix, sc1, sh1, wb16, E_TN, None)
            pd = in_projection(xd, gmix, sc1, sh1, wb16, E_TN, seq_d)
            bias = jnp.pad(b_gate_mlstm[i][None], ((0, 0), (0, LANES - 4 * H_A)))
            gain_a = mlstm_norm[i][None]
            zc = jnp.zeros((nb_c, H_A // 2, 2, LANES, LANES), F32)
            zv = jnp.zeros((nb_c, H_A // 2, 2, 1, LANES), F32)
            h_c, cf, nf, mf = mlstm(pc, bias, gain_a, zc, zv, zv, 0, nb_c, seq_c)
            h_d, _, _, _ = mlstm(pd, bias, gain_a, _pair_layout_c(state_mlstm_c[:, i]),
                                 _pair_layout_n(state_mlstm_n[:, i]), _pair_layout_m(state_mlstm_m[:, i]),
                                 0, nb_d, seq_d)
            new_c.append(_pair_layout_c_inv(cf))
            new_n.append(_pair_layout_n_inv(nf[:, :, :, 0]))
            new_m.append(_pair_layout_m_inv(mf[:, :, :, 0]))

            lam_init = 0.8 - 0.6 * math.exp(-0.3 * l)
            lam_p = jnp.stack([lambda_q1[i], lambda_k1[i], lambda_q2[i], lambda_k2[i]], axis=0)
            qg = jnp.tile(q_norm[i][None], (1, 2))
            kg = jnp.tile(k_norm[i][None], (1, 2))
            dg = diff_norm[i][None]
            o_c, kd, vd = diff_attention_ctx(pc, lam_p, qg, kg, dg, nb_c, seq_c, lam_init)
            past = cache_attn_k.shape[2]
            o_d = diff_attention_dec(pd, cache_attn_k[:, i].reshape(nb_d, past, B_QK),
                                     cache_attn_v[:, i].reshape(nb_d, past, B_V), lam_p, qg, kg, dg, cos, sin,
                                     0, nb_d, seq_d, lam_init)
            new_k.append(kd.reshape(nb_c, seq_c, H_B, 2, DB))
            new_v.append(vd.reshape(nb_c, seq_c, H_B, DVB))
            wo = w_out_even[i].astype(BF16)
            xc = out_projection_even(xc, g1, h_c, o_c, wo[:A_V], wo[A_V:], None)
            xd = out_projection_even(xd, g1, h_d, o_d, wo[:A_V], wo[A_V:], seq_d)
        else:
            wb16 = _pad_cols(w_in_odd[i], O_COLS).astype(BF16)
            gmix = norm_mix[l][None]
            pc = in_projection(xc, gmix, sc1, sh1, wb16, O_TN, None)
            pd = in_projection(xd, gmix, sc1, sh1, wb16, O_TN, seq_d)
            dtb, alog = _group_replicated(dt_bias[i]), _group_replicated(a_log[i])
            dsk = jnp.repeat(d_skip[i], P_C)[None]
            cw, cb = conv_w[i], conv_b[i][None]
            y_c, sfin = ssd(pc, cw, cb, dtb, alog, dsk, None, i, nb_c, seq_c)
            y_d, _ = ssd(pd, cw, cb, dtb, alog, dsk, state_ssd.reshape(nb_d, -1, 2, H_C * P_C, N_C), i, nb_d, seq_d)
            new_s.append(sfin.reshape(nb_c, 2, H_C, P_C, N_C))
            gs, wo = ssd_norm[i][None], w_out_odd[i].astype(BF16)
            xc = out_projection_odd(xc, g1, y_c, gs, wo, None)
            xd = out_projection_odd(xd, g1, y_d, gs, wo, seq_d)
        ffn_w = (w_gate[l].astype(BF16), w_up[l].astype(BF16), w_down[l].astype(BF16))
        xc = ffn(xc, norm_ffn[l][None], sc2, sh2, g2, *ffn_w, None)
        xd = ffn(xd, norm_ffn[l][None], sc2, sh2, g2, *ffn_w, seq_d)

    y_prompt = xc.reshape(nb_c, seq_c, d)
    y_sample = xd.reshape(nb_d, seq_d, d)
    return (y_prompt, y_sample, jnp.stack(new_k, axis=1), jnp.stack(new_v, axis=1), jnp.stack(new_c, axis=1),
            jnp.stack(new_n, axis=1), jnp.stack(new_m, axis=1), jnp.stack(new_s, axis=1))
```

```python
import functools
import math

import jax
import jax.numpy as jnp
import numpy as np
from jax import lax
from jax.experimental import pallas as pl
from jax.experimental.pallas import tpu as pltpu

F32 = jnp.float32
BF16 = jnp.bfloat16

D_MODEL = 2048
DEPTH = 4
GRID_W = 64
RMS_EPS = 1e-6
H_A, DK_A, DV_A, MLSTM_CHUNK = 8, 64, 128, 64
H_B, DB, DVB = 8, 64, 128
ROPE_THETA = 10000.0
ROPE_PAIRS = DB // 4
D_INNER = 2 * D_MODEL
P_C, N_C, G_C = 64, 128, 8
H_C = D_INNER // P_C
R_C = H_C // G_C
SSD_CONV = 4
SSD_CHUNK = 128
D_FF = -(-8 * D_MODEL // (3 * 256)) * 256
A_QK, A_V = H_A * DK_A, H_A * DV_A
B_QK, B_V = H_B * 2 * DB, H_B * DVB
CONV_CH = D_INNER + 2 * G_C * N_C

LANES = 128
VMEM_LIMIT = 56 * 1024 * 1024

TM = 512
INP_TM = 1024
FFN_TM = 512
ADA_TN = 1024
E_COLS = 6400
E_TN = 1280
O_COLS = 10752
O_TN = 1536
FF_TN = 512
OUT_TN = 512
ATT_TQ = 256
MLSTM_UNROLL = 2

EB_Q, EB_K, EB_V, EB_O = 0, 4, 8, 16
EB_BQ, EB_BK, EB_BV, EB_G = 24, 32, 40, 48
OB_Z, OB_X, OB_B, OB_C, OB_DT = 0, 32, 64, 72, 80


def _cparams(sem):
    return pltpu.CompilerParams(dimension_semantics=sem, vmem_limit_bytes=VMEM_LIMIT)


def _silu(x):
    return x * jax.nn.sigmoid(x)


def _split3(x):
    hi = x.astype(BF16)
    r1 = x - hi.astype(F32)
    mid = r1.astype(BF16)
    lo = (r1 - mid.astype(F32)).astype(BF16)
    return hi, mid, lo


def _dot(a, b):
    return jnp.dot(a, b, preferred_element_type=F32)


def _dot3_l(x, m):
    hi, mid, lo = _split3(x)
    return _dot(hi, m) + _dot(mid, m) + _dot(lo, m)


def _dot3_r(m, x):
    hi, mid, lo = _split3(x)
    return _dot(m, hi) + _dot(m, mid) + _dot(m, lo)


def _row_group(i, tm, mod_rows):
    return 0 if mod_rows is None else 1 + (i * tm) // mod_rows


def _ada_kernel(c_ref, w_ref, b_ref, o_ref):
    s = _silu(c_ref[...]).astype(BF16)
    o_ref[...] = _dot(s, w_ref[...].astype(BF16)) + b_ref[...]


def ada_modulation(cvecs, w_ada, b_ada):
    depth, d, n = w_ada.shape
    return pl.pallas_call(
        _ada_kernel,
        grid=(depth, n // ADA_TN),
        in_specs=[
            pl.BlockSpec((8, d), lambda l, j: (0, 0)),
            pl.BlockSpec((None, d, ADA_TN), lambda l, j: (l, 0, j)),
            pl.BlockSpec((None, 1, ADA_TN), lambda l, j: (l, 0, j)),
        ],
        out_specs=pl.BlockSpec((None, 8, ADA_TN), lambda l, j: (l, 0, j)),
        out_shape=jax.ShapeDtypeStruct((depth, 8, n), F32),
        compiler_params=_cparams(("parallel", "parallel")),
        name="ada_modulation",
    )(cvecs, w_ada, b_ada.reshape(depth, 1, n))


def _norm_mod(x, gain, sc, sh):
    ms = jnp.mean(x * x, axis=-1, keepdims=True)
    y = x * lax.rsqrt(ms + RMS_EPS) * gain
    return y * (1.0 + sc) + sh


def _inproj_kernel(x_ref, gain_ref, sc_ref, sh_ref, w_ref, o_ref, h_ref, *, nblk):
    @pl.when(pl.program_id(1) == 0)
    def _():
        h_ref[...] = _norm_mod(x_ref[...], gain_ref[...], sc_ref[...], sh_ref[...]).astype(BF16)

    acc = _dot(h_ref[...], w_ref[...])
    for k in range(nblk):
        o_ref[k] = acc[:, k * LANES:(k + 1) * LANES]


def in_projection(x, gain, sc, sh, w, layer, tn, mod_rows):
    m, d = x.shape
    n = w.shape[2]
    nblk = tn // LANES
    tm = INP_TM
    grp = lambda i, j: (_row_group(i, tm, mod_rows), 0, 0)
    return pl.pallas_call(
        functools.partial(_inproj_kernel, nblk=nblk),
        grid=(m // tm, n // tn),
        in_specs=[
            pl.BlockSpec((tm, d), lambda i, j: (i, 0)),
            pl.BlockSpec((1, d), lambda i, j: (0, 0)),
            pl.BlockSpec((None, 1, d), grp),
            pl.BlockSpec((None, 1, d), grp),
            pl.BlockSpec((None, d, tn), lambda i, j: (layer, 0, j)),
        ],
        out_specs=pl.BlockSpec((nblk, tm, LANES), lambda i, j: (j, i, 0)),
        out_shape=jax.ShapeDtypeStruct((n // LANES, m, LANES), F32),
        scratch_shapes=[pltpu.VMEM((tm, d), BF16)],
        compiler_params=_cparams(("parallel", "arbitrary")),
        name="in_projection",
    )(x, gain, sc, sh, w)


def _ffn_kernel(x_ref, gain_ref, sc_ref, sh_ref, g_ref, wg_ref, wu_ref, wd_ref, o_ref, h_ref, acc_ref):
    f = pl.program_id(1)

    @pl.when(f == 0)
    def _():
        h_ref[...] = _norm_mod(x_ref[...], gain_ref[...], sc_ref[...], sh_ref[...]).astype(BF16)
        acc_ref[...] = jnp.zeros_like(acc_ref)

    h = h_ref[...]
    gate = _dot(h, wg_ref[...])
    up = _dot(h, wu_ref[...])
    ff = (_silu(gate) * up).astype(BF16)
    acc_ref[...] += _dot(ff, wd_ref[...])

    @pl.when(f == pl.num_programs(1) - 1)
    def _():
        o_ref[...] = x_ref[...] + g_ref[...] * acc_ref[...]


def ffn(x, gain, sc, sh, g, wg, wu, wd, layer, mod_rows):
    m, d = x.shape
    dff = wg.shape[2]
    tm = FFN_TM
    grp = lambda i, f: (_row_group(i, tm, mod_rows), 0, 0)
    return pl.pallas_call(
        _ffn_kernel,
        grid=(m // tm, dff // FF_TN),
        in_specs=[
            pl.BlockSpec((tm, d), lambda i, f: (i, 0)),
            pl.BlockSpec((1, d), lambda i, f: (0, 0)),
            pl.BlockSpec((None, 1, d), grp),
            pl.BlockSpec((None, 1, d), grp),
            pl.BlockSpec((None, 1, d), grp),
            pl.BlockSpec((None, d, FF_TN), lambda i, f: (layer, 0, f)),
            pl.BlockSpec((None, d, FF_TN), lambda i, f: (layer, 0, f)),
            pl.BlockSpec((None, FF_TN, d), lambda i, f: (layer, f, 0)),
        ],
        out_specs=pl.BlockSpec((tm, d), lambda i, f: (i, 0)),
        out_shape=jax.ShapeDtypeStruct((m, d), F32),
        scratch_shapes=[pltpu.VMEM((tm, d), BF16), pltpu.VMEM((tm, d), F32)],
        compiler_params=_cparams(("parallel", "arbitrary")),
        name="ffn",
    )(x, gain, sc, sh, g, wg, wu, wd)


def _outproj_even_kernel(x_ref, g_ref, a_ref, b_ref, wa_ref, wb_ref, o_ref):
    acc = _dot(a_ref[...], wa_ref[...]) + _dot(b_ref[...], wb_ref[...])
    o_ref[...] = x_ref[...] + g_ref[...] * acc


def out_projection_even(x, g, ha, ob, w, layer, mod_rows):
    m, d = x.shape
    ka, kb = ha.shape[1], ob.shape[1]
    grp = lambda i: (_row_group(i, TM, mod_rows), 0, 0)
    return pl.pallas_call(
        _outproj_even_kernel,
        grid=(m // TM,),
        in_specs=[
            pl.BlockSpec((TM, d), lambda i: (i, 0)),
            pl.BlockSpec((None, 1, d), grp),
            pl.BlockSpec((TM, ka), lambda i: (i, 0)),
            pl.BlockSpec((TM, kb), lambda i: (i, 0)),
            pl.BlockSpec((None, ka, d), lambda i: (layer, 0, 0)),
            pl.BlockSpec((None, kb, d), lambda i: (layer, ka // kb, 0)),
        ],
        out_specs=pl.BlockSpec((TM, d), lambda i: (i, 0)),
        out_shape=jax.ShapeDtypeStruct((m, d), F32),
        compiler_params=_cparams(("parallel",)),
        name="out_projection_even",
    )(x, g, ha, ob, w, w)


def _outproj_odd_kernel(x_ref, g_ref, y_ref, gain_ref, w_ref, o_ref, yn_ref):
    @pl.when(pl.program_id(1) == 0)
    def _():
        y = y_ref[...]
        ms = jnp.mean(y * y, axis=-1, keepdims=True)
        yn_ref[...] = (y * lax.rsqrt(ms + RMS_EPS) * gain_ref[...]).astype(BF16)

    o_ref[...] = x_ref[...] + g_ref[...] * _dot(yn_ref[...], w_ref[...])


def out_projection_odd(x, g, y, gain, w, layer, mod_rows):
    m, d = x.shape
    k = y.shape[1]
    grp = lambda i, j: (_row_group(i, TM, mod_rows), 0, j)
    return pl.pallas_call(
        _outproj_odd_kernel,
        grid=(m // TM, d // OUT_TN),
        in_specs=[
            pl.BlockSpec((TM, OUT_TN), lambda i, j: (i, j)),
            pl.BlockSpec((None, 1, OUT_TN), grp),
            pl.BlockSpec((TM, k), lambda i, j: (i, 0)),
            pl.BlockSpec((1, k), lambda i, j: (0, 0)),
            pl.BlockSpec((None, k, OUT_TN), lambda i, j: (layer, 0, j)),
        ],
        out_specs=pl.BlockSpec((TM, OUT_TN), lambda i, j: (i, j)),
        out_shape=jax.ShapeDtypeStruct((m, d), F32),
        scratch_shapes=[pltpu.VMEM((TM, k), BF16)],
        compiler_params=_cparams(("parallel", "arbitrary")),
        name="out_projection_odd",
    )(x, g, y, gain, w)


def _lane_iota(shape):
    return lax.broadcasted_iota(jnp.int32, shape, len(shape) - 1)


def _half_rms_norm(x, gain):
    lo = _lane_iota(x.shape) < DB
    x2 = x * x
    s0 = jnp.sum(jnp.where(lo, x2, 0.0), axis=-1, keepdims=True)
    s1 = jnp.sum(jnp.where(lo, 0.0, x2), axis=-1, keepdims=True)
    r = jnp.where(lo, lax.rsqrt(s0 * (1.0 / DB) + RMS_EPS), lax.rsqrt(s1 * (1.0 / DB) + RMS_EPS))
    return x * r * gain


def _rope(x, cos, sin_signed):
    first = (_lane_iota(x.shape) % 32) < ROPE_PAIRS
    partner = jnp.where(first, pltpu.roll(x, LANES - ROPE_PAIRS, 1), pltpu.roll(x, ROPE_PAIRS, 1))
    return x * cos + partner * sin_signed


def _softmax_pv(s, vv):
    p = jnp.exp(s - jnp.max(s, axis=-1, keepdims=True))
    return _dot(p.astype(BF16), vv) * (1.0 / jnp.sum(p, axis=-1, keepdims=True))


def _diff_attn_core(q, kk, vv, lam, dgain, lam_init):
    lo = _lane_iota(q.shape) < DB
    qs = q * (DB ** -0.5)
    q1 = jnp.where(lo, qs, 0.0).astype(BF16)
    q2 = jnp.where(lo, 0.0, qs).astype(BF16)
    nt = (((1,), (1,)), ((), ()))
    s1 = lax.dot_general(q1, kk, nt, preferred_element_type=F32)
    s2 = lax.dot_general(q2, kk, nt, preferred_element_type=F32)
    o = _softmax_pv(s1, vv) - lam * _softmax_pv(s2, vv)
    ms = jnp.mean(o * o, axis=-1, keepdims=True)
    return o * lax.rsqrt(ms + RMS_EPS) * dgain * (1.0 - lam_init)


def _lambda_value(lam_ref, lam_init):
    lp = lam_ref[...]
    a = jnp.sum(lp[0:1] * lp[1:2], axis=-1, keepdims=True)
    b = jnp.sum(lp[2:3] * lp[3:4], axis=-1, keepdims=True)
    return jnp.exp(a) - jnp.exp(b) + lam_init


def _attn_ctx_kernel(lam_ref, q_ref, k_ref, v_ref, qg_ref, kg_ref, dg_ref, *rest, lam_init):
    o_ref, kd_ref, vd_ref = rest[-3:]
    lam = _lambda_value(lam_ref, lam_init)
    for h in range(H_B):
        cs = slice(h * LANES, (h + 1) * LANES)
        kd = _half_rms_norm(k_ref[h], kg_ref[...])
        kd_ref[:, cs] = kd
        v = v_ref[h]
        vd_ref[:, cs] = v
        q = _half_rms_norm(q_ref[h], qg_ref[...])
        o = _diff_attn_core(q, kd.astype(BF16), v.astype(BF16), lam, dg_ref[...], lam_init)
        o_ref[:, cs] = o.astype(BF16)


def diff_attention_ctx(proj, lam_p, qg, kg, dg, layer, n_layers, prev_kv, nb, seq, lam_init):
    m = nb * seq
    vec = pl.BlockSpec((1, LANES), lambda b: (0, 0))
    slab = lambda base: pl.BlockSpec((H_B, seq, LANES), lambda b: (base // H_B, b, 0))
    outb = pl.BlockSpec((seq, H_B * LANES), lambda b: (b, 0))
    kvb = pl.BlockSpec((None, None, seq, H_B * LANES), lambda b: (b, layer, 0, 0))
    kv_shape = jax.ShapeDtypeStruct((nb, n_layers, seq, H_B * LANES), F32)
    prev = () if prev_kv is None else tuple(prev_kv)
    return pl.pallas_call(
        functools.partial(_attn_ctx_kernel, lam_init=lam_init),
        grid=(nb,),
        in_specs=[pl.BlockSpec((4, DB), lambda b: (0, 0)), slab(EB_BQ), slab(EB_BK), slab(EB_BV), vec, vec, vec]
        + [pl.BlockSpec(memory_space=pl.ANY)] * len(prev),
        out_specs=[outb, kvb, kvb],
        out_shape=[jax.ShapeDtypeStruct((m, B_V), BF16), kv_shape, kv_shape],
        input_output_aliases={7: 1, 8: 2} if prev else {},
        compiler_params=_cparams(("parallel",)),
        name="diff_attention_ctx",
    )(lam_p, proj, proj, proj, qg, kg, dg, *prev)


def _attn_dec_kernel(lam_ref, q_ref, k_ref, v_ref, kc_ref, vc_ref, qg_ref, kg_ref, dg_ref, cq_ref, sq_ref, ck_ref,
                     sk_ref, o_ref, ks_ref, vs_ref, *, lam_init, past):
    @pl.when(pl.program_id(2) == 0)
    def _():
        ks_ref[0:past, :] = kc_ref[...].astype(BF16)
        vs_ref[0:past, :] = vc_ref[...].astype(BF16)
        kd = _rope(_half_rms_norm(k_ref[...], kg_ref[...]), ck_ref[...], sk_ref[...])
        ks_ref[past:, :] = kd.astype(BF16)
        vs_ref[past:, :] = v_ref[...].astype(BF16)

    q = _rope(_half_rms_norm(q_ref[...], qg_ref[...]), cq_ref[...], sq_ref[...])
    lam = _lambda_value(lam_ref, lam_init)
    o_ref[...] = _diff_attn_core(q, ks_ref[...], vs_ref[...], lam, dg_ref[...], lam_init).astype(BF16)


def diff_attention_dec(proj, cache_k, cache_v, lam_p, qg, kg, dg, cos, sin, row0, nb, seq, lam_init):
    past = cache_k.shape[1]
    nq = seq // ATT_TQ
    rq, rk = row0 // ATT_TQ, row0 // seq
    vec = pl.BlockSpec((1, LANES), lambda b, h, i: (0, 0))
    kslab = lambda base: pl.BlockSpec((None, seq, LANES), lambda b, h, i: (base + h, rk + b, 0))
    cslab = pl.BlockSpec((None, past, LANES), lambda b, h, i: (b, 0, h))
    tq = pl.BlockSpec((ATT_TQ, LANES), lambda b, h, i: (i, 0))
    tk = pl.BlockSpec((seq, LANES), lambda b, h, i: (0, 0))
    return pl.pallas_call(
        functools.partial(_attn_dec_kernel, lam_init=lam_init, past=past),
        grid=(nb, H_B, nq),
        in_specs=[pl.BlockSpec((4, DB), lambda b, h, i: (0, 0)),
                  pl.BlockSpec((None, ATT_TQ, LANES), lambda b, h, i: (EB_BQ + h, rq + b * nq + i, 0)),
                  kslab(EB_BK), kslab(EB_BV), cslab, cslab, vec, vec, vec, tq, tq, tk, tk],
        out_specs=pl.BlockSpec((ATT_TQ, LANES), lambda b, h, i: (b * nq + i, h)),
        out_shape=jax.ShapeDtypeStruct((nb * seq, B_V), BF16),
        scratch_shapes=[pltpu.VMEM((past + seq, LANES), BF16), pltpu.VMEM((past + seq, LANES), BF16)],
        compiler_params=_cparams(("parallel", "parallel", "arbitrary")),
        name="diff_attention_dec",
    )(lam_p, proj, proj, proj, cache_k, cache_v, qg, kg, dg, cos, sin, cos, sin)


def rope_tables(n_tokens):
    n_rows = n_tokens // GRID_W
    rows, cols = jnp.meshgrid(jnp.arange(n_rows, dtype=F32), jnp.arange(GRID_W, dtype=F32), indexing='ij')
    inv_freq = ROPE_THETA ** (-jnp.arange(ROPE_PAIRS, dtype=F32) / ROPE_PAIRS)
    ang = jnp.stack([rows.reshape(-1, 1) * inv_freq, cols.reshape(-1, 1) * inv_freq], axis=0)
    cos, sin = jnp.cos(ang), jnp.sin(ang)
    cos_half = jnp.concatenate([cos[0], cos[0], cos[1], cos[1]], axis=-1)
    sin_half = jnp.concatenate([-sin[0], sin[0], -sin[1], sin[1]], axis=-1)
    return jnp.tile(cos_half, (1, 2)), jnp.tile(sin_half, (1, 2))


def _tri(n, lower):
    r = lax.broadcasted_iota(jnp.int32, (n, n), 0)
    c = lax.broadcasted_iota(jnp.int32, (n, n), 1)
    return jnp.where((r >= c) if lower else (r <= c), 1.0, 0.0).astype(BF16)


def _mlstm_direction(d, rows, q_ref, k_ref, v_ref, gcol, grow, qk, cn_ref, m_prev, tri_l, tri_u):
    lc = MLSTM_CHUNK
    ri = lax.broadcasted_iota(jnp.int32, (lc, lc), 0)
    ci = lax.broadcasted_iota(jnp.int32, (lc, lc), 1)
    mask = (ci <= ri) if d == 0 else (ci >= ri)
    lo = _lane_iota((lc, LANES)) < DK_A
    fcol = jax.nn.log_sigmoid(gcol)
    frow = jax.nn.log_sigmoid(grow)
    bcol_all = _dot3_r(tri_l if d == 0 else tri_u, fcol)
    brow_all = _dot3_l(frow, tri_u if d == 0 else tri_l)
    q = q_ref[rows, :]
    k = k_ref[rows, :] * (DK_A ** -0.5)
    cn = cn_ref[d]
    cnb = cn.astype(BF16)
    ones = jnp.ones((lc, LANES), BF16)
    vs = [v_ref[j, rows, :].astype(BF16) for j in range(2)]
    hs, m_news, es, decays = [], [], [], []
    for j in range(2):
        li_lane, lf_lane = 8 * d + j, 16 + 8 * d + j
        b_all = jnp.broadcast_to(bcol_all[:, lf_lane:lf_lane + 1], (lc, LANES))
        li_all = jnp.broadcast_to(gcol[:, li_lane:li_lane + 1], (lc, LANES))
        b_row = brow_all[lf_lane:lf_lane + 1, :]
        li_row = grow[li_lane:li_lane + 1, :]
        m_p = m_prev[j]
        dmat = jnp.where(mask, b_all[:, :lc] - b_row + li_row, -jnp.inf)
        inter = b_all + m_p
        m_row = jnp.maximum(inter, jnp.max(dmat, axis=-1, keepdims=True))
        w_inter = jnp.exp(inter - m_row)
        s = qk[j] * jnp.exp(dmat - m_row[:, :lc])
        qm = jnp.where(lo, q, 0.0) if j == 0 else jnp.where(lo, 0.0, q)
        inter_nd = _dot(qm.astype(BF16), cnb)
        intra_nd = _dot(s.astype(BF16), jnp.concatenate([vs[j], ones], axis=1))
        num = w_inter * inter_nd[:, :LANES] + intra_nd[:, :LANES]
        den = w_inter * inter_nd[:, LANES:] + intra_nd[:, LANES:]
        hs.append(num * (1.0 / jnp.maximum(jnp.abs(den), jnp.exp(-m_row))))
        b_end = b_all[lc - 1:lc, :] if d == 0 else b_all[0:1, :]
        g_all = b_end - b_all + li_all
        m_new = jnp.maximum(b_end + m_p, jnp.max(g_all, axis=0, keepdims=True))
        decays.append(jnp.exp(b_end + m_p - m_new))
        es.append(jnp.exp(g_all - m_new))
        m_news.append(m_new)
    wkb = (k * jnp.where(lo, es[0], es[1])).astype(BF16)
    tn = (((0,), (0,)), ((), ()))
    u = lax.dot_general(wkb, jnp.concatenate(vs + [ones], axis=1), tn, preferred_element_type=F32)
    top = lax.broadcasted_iota(jnp.int32, (LANES, LANES), 0) < DK_A
    dec = jnp.where(top, decays[0], decays[1])
    c_new = dec * cn[:, :LANES] + jnp.where(top, u[:, :LANES], u[:, LANES:2 * LANES])
    n_new = dec * cn[:, LANES:] + u[:, 2 * LANES:]
    cn_ref[d] = jnp.concatenate([c_new, n_new], axis=1)
    return jnp.concatenate(hs, axis=-1), m_news


def _mlstm_kernel(q_ref, k_ref, v_ref, ao_ref, g_ref, bias_ref, gain_ref, c0_ref, n0_ref, m0_ref,
                  h_ref, cf_ref, nf_ref, mf_ref, gc_ref, gt_ref, hf_ref, hb_ref, cn_ref, *, seq):
    lc = MLSTM_CHUNK
    nc = seq // lc
    pair = pl.program_id(1)
    shift = (LANES - 2 * pair) % LANES
    bias = pltpu.roll(jnp.broadcast_to(bias_ref[...], (8, LANES)), shift, 1)[0:1]
    for blk in range(seq // LANES):
        r = slice(blk * LANES, (blk + 1) * LANES)
        gs = pltpu.roll(g_ref[r, :], shift, 1) + bias
        gc_ref[r, :] = gs
        gst = gs.T
        for half in range(LANES // lc):
            gt_ref[blk * (LANES // lc) + half] = gst[0:32, half * lc:(half + 1) * lc]
    for d in range(2):
        cn_ref[d, :, 0:LANES] = c0_ref[d]
        cn_ref[d, :, LANES:] = jnp.broadcast_to(n0_ref[d], (LANES, LANES)).T
    tri_l, tri_u = _tri(lc, True), _tri(lc, False)
    nt = (((1,), (1,)), ((), ()))
    lo = _lane_iota((lc, LANES)) < DK_A

    def chunk_qk(rows):
        q = q_ref[rows, :]
        kb = (k_ref[rows, :] * (DK_A ** -0.5)).astype(BF16)
        return [lax.dot_general(jnp.where(lo, q, 0.0).astype(BF16), kb, nt, preferred_element_type=F32),
                lax.dot_general(jnp.where(lo, 0.0, q).astype(BF16), kb, nt, preferred_element_type=F32)]

    def body(c, carry):
        m_f0, m_f1, m_b0, m_b1 = carry
        rf = pl.ds(pl.multiple_of(c * lc, lc), lc)
        rb = pl.ds(pl.multiple_of((nc - 1 - c) * lc, lc), lc)
        h_f, m_f = _mlstm_direction(0, rf, q_ref, k_ref, v_ref, gc_ref[rf, :], gt_ref[c], chunk_qk(rf),
                                    cn_ref, (m_f0, m_f1), tri_l, tri_u)
        hf_ref[rf, :] = h_f
        h_b, m_b = _mlstm_direction(1, rb, q_ref, k_ref, v_ref, gc_ref[rb, :], gt_ref[nc - 1 - c], chunk_qk(rb),
                                    cn_ref, (m_b0, m_b1), tri_l, tri_u)
        hb_ref[rb, :] = h_b
        return m_f[0], m_f[1], m_b[0], m_b[1]

    m0 = m0_ref[...]
    rep = lambda v: jnp.broadcast_to(v, (1, LANES))
    init = (rep(m0[0][:, 0:1]), rep(m0[0][:, DK_A:DK_A + 1]), rep(m0[1][:, 0:1]), rep(m0[1][:, DK_A:DK_A + 1]))
    m_f0, m_f1, m_b0, m_b1 = lax.fori_loop(0, nc, body, init, unroll=MLSTM_UNROLL)
    lane1 = _lane_iota((1, LANES)) < DK_A
    mf_ref[0] = jnp.where(lane1, m_f0, m_f1)
    mf_ref[1] = jnp.where(lane1, m_b0, m_b1)
    for d in range(2):
        cf_ref[d] = cn_ref[d, :, 0:LANES]
        nf_ref[d] = cn_ref[d, :, LANES:].T[0:1, :]

    gain = gain_ref[...]
    for blk in range(seq // LANES):
        r = slice(blk * LANES, (blk + 1) * LANES)
        outs = []
        for j in range(2):
            cs = slice(j * DV_A, (j + 1) * DV_A)
            hh = hf_ref[r, cs] + hb_ref[r, cs]
            ms = jnp.mean(hh * hh, axis=-1, keepdims=True)
            outs.append(hh * lax.rsqrt(ms + RMS_EPS) * gain[:, cs] * jax.nn.sigmoid(ao_ref[j, r, :]))
        h_ref[r, :] = jnp.concatenate(outs, axis=-1).astype(BF16)


def mlstm(proj, bias, gain, c0, n0, m0, row0, nb, seq):
    rb = row0 // seq
    npair = H_A // 2
    slab = lambda base: pl.BlockSpec((None, seq, LANES), lambda b, p: (base + p, rb + b, 0))
    slab2 = lambda base: pl.BlockSpec((2, seq, LANES), lambda b, p: (base // 2 + p, rb + b, 0))
    st_c = pl.BlockSpec((None, None, 2, LANES, LANES), lambda b, p: (b, p, 0, 0, 0))
    st_v = pl.BlockSpec((None, None, 2, 1, LANES), lambda b, p: (b, p, 0, 0, 0))
    return pl.pallas_call(
        functools.partial(_mlstm_kernel, seq=seq),
        grid=(nb, npair),
        in_specs=[slab(EB_Q), slab(EB_K), slab2(EB_V), slab2(EB_O),
                  pl.BlockSpec((None, seq, LANES), lambda b, p: (EB_G, rb + b, 0)),
                  pl.BlockSpec((1, LANES), lambda b, p: (0, 0)),
                  pl.BlockSpec((1, 2 * DV_A), lambda b, p: (0, p)),
                  st_c, st_v, st_v],
        out_specs=[pl.BlockSpec((seq, 2 * DV_A), lambda b, p: (b, p)), st_c, st_v, st_v],
        out_shape=[jax.ShapeDtypeStruct((nb * seq, A_V), BF16),
                   jax.ShapeDtypeStruct((nb, npair, 2, LANES, LANES), F32),
                   jax.ShapeDtypeStruct((nb, npair, 2, 1, LANES), F32),
                   jax.ShapeDtypeStruct((nb, npair, 2, 1, LANES), F32)],
        scratch_shapes=[pltpu.VMEM((seq, LANES), F32), pltpu.VMEM((seq // MLSTM_CHUNK, 32, MLSTM_CHUNK), F32),
                        pltpu.VMEM((seq, 2 * DV_A), F32),
                        pltpu.VMEM((seq, 2 * DV_A), F32), pltpu.VMEM((2, LANES, 2 * LANES), F32)],
        compiler_params=_cparams(("parallel", "parallel")),
        name="mlstm",
    )(proj, proj, proj, proj, proj, bias, gain, c0, n0, m0)


def _softplus(x):
    return jnp.maximum(x, 0.0) + jnp.log1p(jnp.exp(-jnp.abs(x)))


SSD_REP = 2 * R_C


def _pack3(x, lane):
    hi = x.astype(BF16).astype(F32)
    r1 = x - hi
    mid = r1.astype(BF16).astype(F32)
    r2 = r1 - mid
    grp = lane % (3 * SSD_REP)
    return jnp.where(grp < SSD_REP, x, jnp.where(grp < 2 * SSD_REP, r1, r2)).astype(BF16)


SSD_HALO = 8


def _ssd_kernel(xs_ref, bm_ref, cm_ref, z_ref, dtr_ref, wx_ref, wb_ref, wc_ref, bx_ref, bb_ref, bc_ref,
                 dtb_ref, alog_ref, dsk_ref, rep_ref, selw_ref, expf_ref, expb_ref, *rest, seq, has_init, has_prev):
    if has_init:
        s0_ref = rest[0]
        rest = rest[1:]
    if has_prev:
        rest = rest[1:]
    (y_ref, sfin_ref, pad_ref, xa_ref, ba_ref, ca_ref, dt_ref, af_ref, ab_ref, st_ref, ex_ref, abw_ref,
     at_ref, cb_ref) = rest
    lc = SSD_CHUNK
    nc = seq // lc
    gw = R_C * P_C
    nslab = gw // LANES
    lane = _lane_iota((lc, LANES))
    lo = lane < P_C
    fwd_lane = (lane % SSD_REP) < R_C
    first3 = lane < 3 * SSD_REP
    a_neg = -jnp.exp(alog_ref[...])
    tri_l, tri_u = _tri(lc, True), _tri(lc, False)
    ri = lax.broadcasted_iota(jnp.int32, (lc, lc), 0)
    ci = lax.broadcasted_iota(jnp.int32, (lc, lc), 1)
    causal, anti = ci <= ri, ci >= ri
    nt = (((1,), (1,)), ((), ()))

    def chunk_rows(c):
        return pl.ds(pl.multiple_of(c * lc, lc), lc)

    halo = jnp.zeros((SSD_HALO, LANES), F32)
    for s in range(nslab + 2):
        pad_ref[s, 0:SSD_HALO, :] = halo
        pad_ref[s, SSD_HALO + seq:, :] = halo
        src = xs_ref[s] if s < nslab else (bm_ref[...] if s == nslab else cm_ref[...])
        pad_ref[s, SSD_HALO:SSD_HALO + seq, :] = src

    def conv_chunk(s, base, w, b):
        y = w[0:1] * pad_ref[s, pl.ds(base + SSD_HALO - 2, lc), :]
        y = y + w[1:2] * pad_ref[s, pl.ds(base + SSD_HALO - 1, lc), :]
        y = y + w[2:3] * pad_ref[s, pl.ds(base + SSD_HALO, lc), :]
        y = y + w[3:4] * pad_ref[s, pl.ds(base + SSD_HALO + 1, lc), :]
        return _silu(y + b)

    def prep_body(c, carry):
        base = pl.multiple_of(c * lc, lc)
        rows = pl.ds(base, lc)
        for s in range(nslab):
            cs = slice(s * LANES, (s + 1) * LANES)
            xa_ref[rows, cs] = conv_chunk(s, base, wx_ref[:, cs], bx_ref[:, cs])
        ba_ref[rows, :] = conv_chunk(nslab, base, wb_ref[...], bb_ref[...])
        ca_ref[rows, :] = conv_chunk(nslab + 1, base, wc_ref[...], bc_ref[...])
        dt = _softplus(_dot3_l(dtr_ref[rows, :], rep_ref[...]) + dtb_ref[...])
        dt_ref[rows, :] = dt
        da = dt * a_neg
        af_ref[rows, :] = _dot3_r(tri_l, da)
        ab_ref[rows, :] = _dot3_r(tri_u, da)
        return carry

    lax.fori_loop(0, nc, prep_body, 0, unroll=2)

    for d in range(2):
        for k in range(nslab):
            cs = slice(k * LANES, (k + 1) * LANES)
            st_ref[d, :, cs] = s0_ref[d, cs, :].T if has_init else jnp.zeros((N_C, LANES), F32)

    def expand(ea, te, e_ref):
        return _dot(_pack3(jnp.where(first3, ea, te), lane), e_ref[...])

    def fwd_a(c, slot):
        rows = chunk_rows(c)
        acum = af_ref[rows, :]
        a_end = acum[lc - 1:lc, :]
        ex_ref[slot] = expand(jnp.exp(acum), jnp.exp(a_end - acum) * dt_ref[rows, :], expf_ref)

    def fwd_b(c, slot):
        rows = chunk_rows(c)
        ea_x, te_x = ex_ref[slot, :, 0:gw], ex_ref[slot, :, gw:2 * gw]
        st = st_ref[0]
        y_ref[rows, :] = _dot(ca_ref[rows, :].astype(BF16), st.astype(BF16)) * ea_x
        bc_t = ba_ref[rows, :].T.astype(BF16)
        st_ref[0] = st * ea_x[lc - 1:lc, :] + _dot(bc_t, (xa_ref[rows, :] * te_x).astype(BF16))

    def bwd_a(c, slot):
        rows = chunk_rows(c)
        dt = dt_ref[rows, :]
        acum_b = ab_ref[rows, :]
        acum = jnp.where(fwd_lane, af_ref[rows, :], acum_b)
        at_ref[slot, 0] = acum.T
        at_ref[slot, 1] = dt.T
        abw_ref[slot] = _dot(_pack3(acum, lane), selw_ref[...])
        a_end = acum_b[0:1, :]
        ex_ref[slot] = expand(jnp.exp(acum_b), jnp.exp(a_end - acum_b) * dt, expb_ref)
        cb_ref[slot] = lax.dot_general(ca_ref[rows, :].astype(BF16), ba_ref[rows, :].astype(BF16), nt,
                                       preferred_element_type=F32)

    def bwd_b(c, slot):
        rows = chunk_rows(c)
        ea_x, te_x = ex_ref[slot, :, 0:gw], ex_ref[slot, :, gw:2 * gw]
        cb = cb_ref[slot]
        st = st_ref[1]
        xs = xa_ref[rows, :]
        y_inter = _dot(ca_ref[rows, :].astype(BF16), st.astype(BF16)) * ea_x
        for s in range(nslab):
            cs = slice(s * LANES, (s + 1) * LANES)
            ws = []
            for j in range(2):
                r = 2 * s + j
                rb = R_C + r
                aj_f, aj_b = at_ref[slot, 0, r:r + 1, :], at_ref[slot, 0, rb:rb + 1, :]
                dt_f, dt_b = at_ref[slot, 1, r:r + 1, :], at_ref[slot, 1, rb:rb + 1, :]
                wf = jnp.exp(jnp.where(causal, abw_ref[slot, :, r * LANES:(r + 1) * LANES] - aj_f, -jnp.inf))
                wb = jnp.exp(jnp.where(anti, abw_ref[slot, :, rb * LANES:(rb + 1) * LANES] - aj_b, -jnp.inf))
                ws.append((cb * (wf * dt_f + wb * dt_b)).astype(BF16))
            xsl = xs[:, cs]
            x2 = jnp.concatenate([jnp.where(lo, xsl, 0.0), jnp.where(lo, 0.0, xsl)], axis=0).astype(BF16)
            y = y_ref[rows, cs] + y_inter[:, cs] + dsk_ref[:, cs] * xsl + _dot(jnp.concatenate(ws, axis=1), x2)
            y_ref[rows, cs] = y * _silu(z_ref[s, rows, :])
        bc_t = ba_ref[rows, :].T.astype(BF16)
        st_ref[1] = st * ea_x[0:1, :] + _dot(bc_t, (xs * te_x).astype(BF16))

    def sweep(stage_a, stage_b, order):
        stage_a(order(0), 0)

        def body(k, carry):
            stage_b(order(2 * k), 0)
            stage_a(order(2 * k + 1), 1)
            stage_b(order(2 * k + 1), 1)
            stage_a(order(jnp.minimum(2 * k + 2, nc - 1)), 0)
            return carry

        lax.fori_loop(0, nc // 2, body, 0)

    sweep(fwd_a, fwd_b, lambda i: i)
    sweep(bwd_a, bwd_b, lambda i: nc - 1 - i)

    for d in range(2):
        for k in range(nslab):
            cs = slice(k * LANES, (k + 1) * LANES)
            sfin_ref[d, cs, :] = st_ref[d, :, cs].T


def ssd(proj, conv_w, conv_b, dtb, alog, dsk, s0, layer, n_layers, prev_states, nb, seq):
    gw = R_C * P_C
    nslab = gw // LANES
    slab = lambda base: pl.BlockSpec((None, seq, LANES), lambda b, g: (base + g, b, 0))
    slab4 = lambda base: pl.BlockSpec((nslab, seq, LANES), lambda b, g: (base // nslab + g, b, 0))
    xo, bo, co = 0, D_INNER // LANES, (D_INNER + G_C * N_C) // LANES
    cw = lambda rws, width, off: pl.BlockSpec((rws, width), lambda b, g: (0, off * LANES // width + g))
    vec = pl.BlockSpec((None, 1, LANES), lambda b, g: (g, 0, 0))
    full = lambda a: pl.BlockSpec(a.shape, lambda b, g: (0,) * a.ndim)
    st_in = pl.BlockSpec((None, None, 2, gw, N_C), lambda b, g: (b, layer, 0, g, 0))
    st_out = pl.BlockSpec((None, None, 2, gw, N_C), lambda b, g: (b, layer, 0, g, 0))
    rep, selw, expf, expb = _ssd_constants()
    has_init, has_prev = s0 is not None, prev_states is not None
    st_shape = jax.ShapeDtypeStruct((nb, n_layers, 2, H_C * P_C, N_C), F32)
    vm = lambda *shape: pltpu.VMEM(shape, F32)
    n_in = 18 + has_init
    return pl.pallas_call(
        functools.partial(_ssd_kernel, seq=seq, has_init=has_init, has_prev=has_prev),
        grid=(nb, G_C),
        in_specs=[slab4(OB_X), slab(OB_B), slab(OB_C), slab4(OB_Z),
                  pl.BlockSpec((None, seq, LANES), lambda b, g: (OB_DT, b, 0)),
                  cw(SSD_CONV, gw, xo), cw(SSD_CONV, LANES, bo), cw(SSD_CONV, LANES, co),
                  cw(1, gw, xo), cw(1, LANES, bo), cw(1, LANES, co),
                  vec, vec, pl.BlockSpec((1, gw), lambda b, g: (0, g)),
                  pl.BlockSpec((None, LANES, LANES), lambda b, g: (g, 0, 0)), full(selw), full(expf), full(expb)]
        + ([st_in] if has_init else []) + ([pl.BlockSpec(memory_space=pl.ANY)] if has_prev else []),
        out_specs=[pl.BlockSpec((seq, gw), lambda b, g: (b, g)), st_out],
        out_shape=[jax.ShapeDtypeStruct((nb * seq, D_INNER), F32), st_shape],
        input_output_aliases={n_in: 1} if has_prev else {},
        scratch_shapes=[vm(nslab + 2, seq + 2 * SSD_HALO, LANES), vm(seq, gw), vm(seq, LANES), vm(seq, LANES),
                        vm(seq, LANES), vm(seq, LANES), vm(seq, LANES), vm(2, N_C, gw), vm(2, SSD_CHUNK, 2 * gw),
                        vm(2, SSD_CHUNK, SSD_REP * LANES), vm(2, 2, LANES, SSD_CHUNK), vm(2, SSD_CHUNK, SSD_CHUNK)],
        compiler_params=_cparams(("parallel", "parallel")),
        name="ssd",
    )(proj, proj, proj, proj, proj, conv_w, conv_w, conv_w, conv_b, conv_b, conv_b, dtb, alog, dsk,
      rep, selw, expf, expb, *((s0,) if has_init else ()), *((prev_states,) if has_prev else ()))


def _ssd_constants():
    gw = R_C * P_C
    rep = np.zeros((G_C, LANES, LANES), np.float32)
    for g in range(G_C):
        for l in range(LANES):
            c = l % SSD_REP
            rep[g, (R_C * g + c) if c < R_C else (H_C + R_C * g + c - R_C), l] = 1.0
    selw = np.zeros((LANES, SSD_REP * LANES), np.float32)
    expf = np.zeros((LANES, 2 * gw), np.float32)
    expb = np.zeros((LANES, 2 * gw), np.float32)
    for l in range(3 * SSD_REP):
        c = l % SSD_REP
        selw[l, c * LANES:(c + 1) * LANES] = 1.0
    for l in range(6 * SSD_REP):
        c, q = l % SSD_REP, l // (3 * SSD_REP)
        if c < R_C:
            expf[l, q * gw + c * P_C:q * gw + (c + 1) * P_C] = 1.0
        else:
            expb[l, q * gw + (c - R_C) * P_C:q * gw + (c - R_C + 1) * P_C] = 1.0
    return tuple(jnp.asarray(a, BF16) for a in (rep, selw, expf, expb))


def _pair_layout_c(c):
    nb = c.shape[0]
    c = c.transpose(0, 2, 1, 4, 3).reshape(nb, H_A // 2, 2, 2, DK_A, DV_A)
    return c.transpose(0, 1, 3, 2, 4, 5).reshape(nb, H_A // 2, 2, 2 * DK_A, DV_A)


def _pair_layout_c_inv(c):
    nb = c.shape[0]
    c = c.reshape(nb, H_A // 2, 2, 2, DK_A, DV_A).transpose(0, 2, 1, 3, 5, 4)
    return c.reshape(nb, 2, H_A, DV_A, DK_A)


def _pair_layout_n(n):
    nb = n.shape[0]
    return n.reshape(nb, 2, H_A // 2, 2 * DK_A).transpose(0, 2, 1, 3).reshape(nb, H_A // 2, 2, 1, 2 * DK_A)


def _pair_layout_n_inv(n):
    nb = n.shape[0]
    return n.reshape(nb, H_A // 2, 2, 2 * DK_A).transpose(0, 2, 1, 3).reshape(nb, 2, H_A, DK_A)


def _pair_layout_m(m):
    return _pair_layout_n(jnp.repeat(m[..., None], DK_A, axis=-1))


def _pair_layout_m_inv(m):
    return _pair_layout_n_inv(m)[..., 0]


def _group_replicated(v):
    t = v.reshape(2, G_C, R_C).transpose(1, 0, 2).reshape(G_C, 1, SSD_REP)
    return jnp.tile(t, (1, 1, LANES // SSD_REP))


def kernel(x_prompt, x_sample, cache_attn_k, cache_attn_v, state_mlstm_c, state_mlstm_n, state_mlstm_m, state_ssd,
           c, c_ctx, w_ada, b_ada, norm_mix, norm_ffn, w_gate, w_up, w_down, w_in_even, b_gate_mlstm, mlstm_norm,
           q_norm, k_norm, lambda_q1, lambda_k1, lambda_q2, lambda_k2, diff_norm, w_out_even, w_in_odd, conv_w,
           conv_b, dt_bias, a_log, d_skip, ssd_norm, w_out_odd):
    nb_c, seq_c, d = x_prompt.shape
    nb_d, seq_d, _ = x_sample.shape
    n_ctx, n_dec = nb_c * seq_c, nb_d * seq_d
    depth = w_ada.shape[0]

    xc, xd = x_prompt.reshape(n_ctx, d), x_sample.reshape(n_dec, d)
    cvecs = jnp.concatenate([c_ctx[None, :], c, jnp.zeros((8 - 1 - nb_d, d), F32)], axis=0)
    mods = ada_modulation(cvecs, w_ada, b_ada)

    def mod(l, k):
        return mods[l, :, k * d:(k + 1) * d].reshape(8, 1, d)

    cos, sin = rope_tables(seq_d)
    n_even, n_odd = w_in_even.shape[0], w_in_odd.shape[0]
    gate0 = 2 * A_QK + 2 * A_V
    w_even = jnp.concatenate([w_in_even[:, :, :gate0], w_in_even[:, :, gate0 + 4 * H_A:],
                              w_in_even[:, :, gate0:gate0 + 4 * H_A]], axis=2)
    w_even = jnp.pad(w_even, ((0, 0), (0, 0), (0, E_COLS - w_even.shape[2]))).astype(BF16)
    w_odd = jnp.pad(w_in_odd, ((0, 0), (0, 0), (0, O_COLS - w_in_odd.shape[2]))).astype(BF16)
    wo_even, wo_odd = w_out_even.astype(BF16), w_out_odd.astype(BF16)
    ffn_w = (w_gate.astype(BF16), w_up.astype(BF16), w_down.astype(BF16))

    new_c, new_n, new_m = [], [], []
    new_kv, new_s = None, None
    for l in range(depth):
        i = l // 2
        sh1, sc1, g1, sh2, sc2, g2 = (mod(l, k) for k in range(6))
        if l % 2 == 0:
            gmix = norm_mix[l][None]
            pc = in_projection(xc, gmix, sc1, sh1, w_even, i, E_TN, None)
            pd = in_projection(xd, gmix, sc1, sh1, w_even, i, E_TN, seq_d)
            bias = jnp.pad(b_gate_mlstm[i][None], ((0, 0), (0, LANES - 4 * H_A)))
            gain_a = mlstm_norm[i][None]
            zc = jnp.zeros((nb_c, H_A // 2, 2, LANES, LANES), F32)
            zv = jnp.zeros((nb_c, H_A // 2, 2, 1, LANES), F32)
            h_c, cf, nf, mf = mlstm(pc, bias, gain_a, zc, zv, zv, 0, nb_c, seq_c)
            h_d, _, _, _ = mlstm(pd, bias, gain_a, _pair_layout_c(state_mlstm_c[:, i]),
                                 _pair_layout_n(state_mlstm_n[:, i]), _pair_layout_m(state_mlstm_m[:, i]),
                                 0, nb_d, seq_d)
            new_c.append(_pair_layout_c_inv(cf))
            new_n.append(_pair_layout_n_inv(nf[:, :, :, 0]))
            new_m.append(_pair_layout_m_inv(mf[:, :, :, 0]))

            lam_init = 0.8 - 0.6 * math.exp(-0.3 * l)
            lam_p = jnp.stack([lambda_q1[i], lambda_k1[i], lambda_q2[i], lambda_k2[i]], axis=0)
            qg = jnp.tile(q_norm[i][None], (1, 2))
            kg = jnp.tile(k_norm[i][None], (1, 2))
            dg = diff_norm[i][None]
            o_c, *new_kv = diff_attention_ctx(pc, lam_p, qg, kg, dg, i, n_even, new_kv, nb_c, seq_c, lam_init)
            past = cache_attn_k.shape[2]
            o_d = diff_attention_dec(pd, cache_attn_k[:, i].reshape(nb_d, past, B_QK),
                                     cache_attn_v[:, i].reshape(nb_d, past, B_V), lam_p, qg, kg, dg, cos, sin,
                                     0, nb_d, seq_d, lam_init)
            xc = out_projection_even(xc, g1, h_c, o_c, wo_even, i, None)
            xd = out_projection_even(xd, g1, h_d, o_d, wo_even, i, seq_d)
        else:
            gmix = norm_mix[l][None]
            pc = in_projection(xc, gmix, sc1, sh1, w_odd, i, O_TN, None)
            pd = in_projection(xd, gmix, sc1, sh1, w_odd, i, O_TN, seq_d)
            dtb, alog = _group_replicated(dt_bias[i]), _group_replicated(a_log[i])
            dsk = jnp.repeat(d_skip[i], P_C)[None]
            cw, cb = conv_w[i], conv_b[i][None]
            y_c, new_s = ssd(pc, cw, cb, dtb, alog, dsk, None, i, n_odd, new_s, nb_c, seq_c)
            y_d, _ = ssd(pd, cw, cb, dtb, alog, dsk, state_ssd.reshape(nb_d, n_odd, 2, H_C * P_C, N_C), i, n_odd, None,
                         nb_d, seq_d)
            gs = ssd_norm[i][None]
            xc = out_projection_odd(xc, g1, y_c, gs, wo_odd, i, None)
            xd = out_projection_odd(xd, g1, y_d, gs, wo_odd, i, seq_d)
        xc = ffn(xc, norm_ffn[l][None], sc2, sh2, g2, *ffn_w, l, None)
        xd = ffn(xd, norm_ffn[l][None], sc2, sh2, g2, *ffn_w, l, seq_d)

    y_prompt = xc.reshape(nb_c, seq_c, d)
    y_sample = xd.reshape(nb_d, seq_d, d)
    new_k = new_kv[0].reshape(nb_c, n_even, seq_c, H_B, 2, DB)
    new_v = new_kv[1].reshape(nb_c, n_even, seq_c, H_B, DVB)
    return (y_prompt, y_sample, new_k, new_v, jnp.stack(new_c, axis=1), jnp.stack(new_n, axis=1),
            jnp.stack(new_m, axis=1), new_s.reshape(nb_c, n_odd, 2, H_C, P_C, N_C))
```

```python
import functools
import itertools
import math

import jax
import jax.numpy as jnp
import numpy as np
from jax import lax
from jax.experimental import pallas as pl
from jax.experimental.pallas import tpu as pltpu

F32 = jnp.float32
BF16 = jnp.bfloat16

D_MODEL = 2048
DEPTH = 4
GRID_W = 64
RMS_EPS = 1e-6
H_A, DK_A, DV_A, MLSTM_CHUNK = 8, 64, 128, 64
H_B, DB, DVB = 8, 64, 128
ROPE_THETA = 10000.0
ROPE_PAIRS = DB // 4
D_INNER = 2 * D_MODEL
P_C, N_C, G_C = 64, 128, 8
H_C = D_INNER // P_C
R_C = H_C // G_C
SSD_CONV = 4
SSD_CHUNK = 128
D_FF = -(-8 * D_MODEL // (3 * 256)) * 256
A_QK, A_V = H_A * DK_A, H_A * DV_A
B_QK, B_V = H_B * 2 * DB, H_B * DVB
CONV_CH = D_INNER + 2 * G_C * N_C

LANES = 128
VMEM_LIMIT = 56 * 1024 * 1024

TM = 512
INP_TM = 1024
FFN_TM = 512
ADA_TN = 1024
E_COLS = 6400
E_TN = 1280
O_COLS = 10752
O_TN = 1536
FF_TN = 512
OUT_TN = 512
ATT_TQ = 256
ATT_CTX_LOCKSTEP = 4

EB_Q, EB_K, EB_V, EB_O = 0, 4, 8, 16
EB_BQ, EB_BK, EB_BV, EB_G = 24, 32, 40, 48
OB_Z, OB_X, OB_B, OB_C, OB_DT = 0, 32, 64, 72, 80


def _cparams(sem):
    return pltpu.CompilerParams(dimension_semantics=sem, vmem_limit_bytes=VMEM_LIMIT)


def _silu(x):
    return x * jax.nn.sigmoid(x)


def _split3(x):
    hi = x.astype(BF16)
    r1 = x - hi.astype(F32)
    mid = r1.astype(BF16)
    lo = (r1 - mid.astype(F32)).astype(BF16)
    return hi, mid, lo


def _dot(a, b):
    return jnp.dot(a, b, preferred_element_type=F32)


def _dot3_l(x, m):
    hi, mid, lo = _split3(x)
    return _dot(hi, m) + _dot(mid, m) + _dot(lo, m)


def _dot3_r(m, x):
    hi, mid, lo = _split3(x)
    return _dot(m, hi) + _dot(m, mid) + _dot(m, lo)


def _row_group(i, tm, mod_rows):
    return 0 if mod_rows is None else 1 + (i * tm) // mod_rows


def _ada_kernel(c_ref, w_ref, b_ref, o_ref):
    s = _silu(c_ref[...]).astype(BF16)
    o_ref[...] = _dot(s, w_ref[...].astype(BF16)) + b_ref[...]


def ada_modulation(cvecs, w_ada, b_ada):
    depth, d, n = w_ada.shape
    return pl.pallas_call(
        _ada_kernel,
        grid=(depth, n // ADA_TN),
        in_specs=[
            pl.BlockSpec((8, d), lambda l, j: (0, 0)),
            pl.BlockSpec((None, d, ADA_TN), lambda l, j: (l, 0, j)),
            pl.BlockSpec((None, 1, ADA_TN), lambda l, j: (l, 0, j)),
        ],
        out_specs=pl.BlockSpec((None, 8, ADA_TN), lambda l, j: (l, 0, j)),
        out_shape=jax.ShapeDtypeStruct((depth, 8, n), F32),
        compiler_params=_cparams(("parallel", "parallel")),
        name="ada_modulation",
    )(cvecs, w_ada, b_ada.reshape(depth, 1, n))


def _norm_mod(x, gain, sc, sh):
    ms = jnp.mean(x * x, axis=-1, keepdims=True)
    y = x * lax.rsqrt(ms + RMS_EPS) * gain
    return y * (1.0 + sc) + sh


def _inproj_kernel(x_ref, gain_ref, sc_ref, sh_ref, w_ref, o_ref, h_ref, *, nblk):
    @pl.when(pl.program_id(1) == 0)
    def _():
        h_ref[...] = _norm_mod(x_ref[...], gain_ref[...], sc_ref[...], sh_ref[...]).astype(BF16)

    acc = _dot(h_ref[...], w_ref[...])
    for k in range(nblk):
        o_ref[k] = acc[:, k * LANES:(k + 1) * LANES]


def in_projection(x, gain, sc, sh, w, layer, tn, mod_rows):
    m, d = x.shape
    n = w.shape[2]
    nblk = tn // LANES
    tm = INP_TM
    grp = lambda i, j: (_row_group(i, tm, mod_rows), 0, 0)
    return pl.pallas_call(
        functools.partial(_inproj_kernel, nblk=nblk),
        grid=(m // tm, n // tn),
        in_specs=[
            pl.BlockSpec((tm, d), lambda i, j: (i, 0)),
            pl.BlockSpec((1, d), lambda i, j: (0, 0)),
            pl.BlockSpec((None, 1, d), grp),
            pl.BlockSpec((None, 1, d), grp),
            pl.BlockSpec((None, d, tn), lambda i, j: (layer, 0, j)),
        ],
        out_specs=pl.BlockSpec((nblk, tm, LANES), lambda i, j: (j, i, 0)),
        out_shape=jax.ShapeDtypeStruct((n // LANES, m, LANES), F32),
        scratch_shapes=[pltpu.VMEM((tm, d), BF16)],
        compiler_params=_cparams(("parallel", "arbitrary")),
        name="in_projection",
    )(x, gain, sc, sh, w)


def _ffn_kernel(x_ref, gain_ref, sc_ref, sh_ref, g_ref, wg_ref, wu_ref, wd_ref, o_ref, h_ref, acc_ref):
    f = pl.program_id(1)

    @pl.when(f == 0)
    def _():
        h_ref[...] = _norm_mod(x_ref[...], gain_ref[...], sc_ref[...], sh_ref[...]).astype(BF16)
        acc_ref[...] = jnp.zeros_like(acc_ref)

    h = h_ref[...]
    gate = _dot(h, wg_ref[...])
    up = _dot(h, wu_ref[...])
    ff = (_silu(gate) * up).astype(BF16)
    acc_ref[...] += _dot(ff, wd_ref[...])

    @pl.when(f == pl.num_programs(1) - 1)
    def _():
        o_ref[...] = x_ref[...] + g_ref[...] * acc_ref[...]


def ffn(x, gain, sc, sh, g, wg, wu, wd, layer, mod_rows):
    m, d = x.shape
    dff = wg.shape[2]
    tm = FFN_TM
    grp = lambda i, f: (_row_group(i, tm, mod_rows), 0, 0)
    return pl.pallas_call(
        _ffn_kernel,
        grid=(m // tm, dff // FF_TN),
        in_specs=[
            pl.BlockSpec((tm, d), lambda i, f: (i, 0)),
            pl.BlockSpec((1, d), lambda i, f: (0, 0)),
            pl.BlockSpec((None, 1, d), grp),
            pl.BlockSpec((None, 1, d), grp),
            pl.BlockSpec((None, 1, d), grp),
            pl.BlockSpec((None, d, FF_TN), lambda i, f: (layer, 0, f)),
            pl.BlockSpec((None, d, FF_TN), lambda i, f: (layer, 0, f)),
            pl.BlockSpec((None, FF_TN, d), lambda i, f: (layer, f, 0)),
        ],
        out_specs=pl.BlockSpec((tm, d), lambda i, f: (i, 0)),
        out_shape=jax.ShapeDtypeStruct((m, d), F32),
        scratch_shapes=[pltpu.VMEM((tm, d), BF16), pltpu.VMEM((tm, d), F32)],
        compiler_params=_cparams(("parallel", "arbitrary")),
        name="ffn",
    )(x, gain, sc, sh, g, wg, wu, wd)


def _outproj_even_kernel(x_ref, g_ref, a_ref, b_ref, wa_ref, wb_ref, o_ref):
    acc = _dot(a_ref[...], wa_ref[...]) + _dot(b_ref[...], wb_ref[...])
    o_ref[...] = x_ref[...] + g_ref[...] * acc


def out_projection_even(x, g, ha, ob, w, layer, mod_rows):
    m, d = x.shape
    ka, kb = ha.shape[1], ob.shape[1]
    grp = lambda i: (_row_group(i, TM, mod_rows), 0, 0)
    return pl.pallas_call(
        _outproj_even_kernel,
        grid=(m // TM,),
        in_specs=[
            pl.BlockSpec((TM, d), lambda i: (i, 0)),
            pl.BlockSpec((None, 1, d), grp),
            pl.BlockSpec((TM, ka), lambda i: (i, 0)),
            pl.BlockSpec((TM, kb), lambda i: (i, 0)),
            pl.BlockSpec((None, ka, d), lambda i: (layer, 0, 0)),
            pl.BlockSpec((None, kb, d), lambda i: (layer, ka // kb, 0)),
        ],
        out_specs=pl.BlockSpec((TM, d), lambda i: (i, 0)),
        out_shape=jax.ShapeDtypeStruct((m, d), F32),
        compiler_params=_cparams(("parallel",)),
        name="out_projection_even",
    )(x, g, ha, ob, w, w)


def _outproj_odd_kernel(x_ref, g_ref, y_ref, gain_ref, w_ref, o_ref, yn_ref):
    @pl.when(pl.program_id(1) == 0)
    def _():
        y = y_ref[...]
        ms = jnp.mean(y * y, axis=-1, keepdims=True)
        yn_ref[...] = (y * lax.rsqrt(ms + RMS_EPS) * gain_ref[...]).astype(BF16)

    o_ref[...] = x_ref[...] + g_ref[...] * _dot(yn_ref[...], w_ref[...])


def out_projection_odd(x, g, y, gain, w, layer, mod_rows):
    m, d = x.shape
    k = y.shape[1]
    grp = lambda i, j: (_row_group(i, TM, mod_rows), 0, j)
    return pl.pallas_call(
        _outproj_odd_kernel,
        grid=(m // TM, d // OUT_TN),
        in_specs=[
            pl.BlockSpec((TM, OUT_TN), lambda i, j: (i, j)),
            pl.BlockSpec((None, 1, OUT_TN), grp),
            pl.BlockSpec((TM, k), lambda i, j: (i, 0)),
            pl.BlockSpec((1, k), lambda i, j: (0, 0)),
            pl.BlockSpec((None, k, OUT_TN), lambda i, j: (layer, 0, j)),
        ],
        out_specs=pl.BlockSpec((TM, OUT_TN), lambda i, j: (i, j)),
        out_shape=jax.ShapeDtypeStruct((m, d), F32),
        scratch_shapes=[pltpu.VMEM((TM, k), BF16)],
        compiler_params=_cparams(("parallel", "arbitrary")),
        name="out_projection_odd",
    )(x, g, y, gain, w)


def _lane_iota(shape):
    return lax.broadcasted_iota(jnp.int32, shape, len(shape) - 1)


def _half_rms_norm(x, gain):
    lo = _lane_iota(x.shape) < DB
    x2 = x * x
    s0 = jnp.sum(jnp.where(lo, x2, 0.0), axis=-1, keepdims=True)
    s1 = jnp.sum(jnp.where(lo, 0.0, x2), axis=-1, keepdims=True)
    r = jnp.where(lo, lax.rsqrt(s0 * (1.0 / DB) + RMS_EPS), lax.rsqrt(s1 * (1.0 / DB) + RMS_EPS))
    return x * r * gain


def _rope(x, cos, sin_signed):
    first = (_lane_iota(x.shape) % 32) < ROPE_PAIRS
    partner = jnp.where(first, pltpu.roll(x, LANES - ROPE_PAIRS, 1), pltpu.roll(x, ROPE_PAIRS, 1))
    return x * cos + partner * sin_signed


def _diff_attn_core(q, kk, vv, lam, dgain, lam_init):
    lo = _lane_iota(q.shape) < DB
    qs = q * (DB ** -0.5)
    nt = (((1,), (1,)), ((), ()))
    s = [lax.dot_general(jnp.where(lo, qs, 0.0).astype(BF16), kk, nt, preferred_element_type=F32),
         lax.dot_general(jnp.where(lo, 0.0, qs).astype(BF16), kk, nt, preferred_element_type=F32)]
    p = [jnp.exp(si - jnp.max(si, axis=-1, keepdims=True)) for si in s]
    pv = [_dot(pi.astype(BF16), vv) * (1.0 / jnp.sum(pi, axis=-1, keepdims=True)) for pi in p]
    o = pv[0] - lam * pv[1]
    ms = jnp.mean(o * o, axis=-1, keepdims=True)
    return o * lax.rsqrt(ms + RMS_EPS) * dgain * (1.0 - lam_init)


def _lambda_value(lam_ref, lam_init):
    lp = lam_ref[...]
    a = jnp.sum(lp[0:1] * lp[1:2], axis=-1, keepdims=True)
    b = jnp.sum(lp[2:3] * lp[3:4], axis=-1, keepdims=True)
    return jnp.exp(a) - jnp.exp(b) + lam_init


def _attn_ctx_kernel(lam_ref, q_ref, k_ref, v_ref, qg_ref, kg_ref, dg_ref, *rest, lam_init):
    o_ref, kd_ref, vd_ref = rest[-3:]
    lam = _lambda_value(lam_ref, lam_init)
    nt = (((1,), (1,)), ((), ()))

    def head(h):
        cs = slice(h * LANES, (h + 1) * LANES)
        kd = _half_rms_norm(k_ref[h], kg_ref[...])
        kd_ref[:, cs] = kd
        v = v_ref[h]
        vd_ref[:, cs] = v
        q = _half_rms_norm(q_ref[h], qg_ref[...]) * (DB ** -0.5)
        yield
        lo = _lane_iota(q.shape) < DB
        kk, vv = kd.astype(BF16), v.astype(BF16)
        s = [lax.dot_general(jnp.where(lo, q, 0.0).astype(BF16), kk, nt, preferred_element_type=F32),
             lax.dot_general(jnp.where(lo, 0.0, q).astype(BF16), kk, nt, preferred_element_type=F32)]
        yield
        p = [jnp.exp(si - jnp.max(si, axis=-1, keepdims=True)) for si in s]
        yield
        pv = [_dot(pi.astype(BF16), vv) * (1.0 / jnp.sum(pi, axis=-1, keepdims=True)) for pi in p]
        yield
        o = pv[0] - lam * pv[1]
        ms = jnp.mean(o * o, axis=-1, keepdims=True)
        o_ref[:, cs] = (o * lax.rsqrt(ms + RMS_EPS) * dg_ref[...] * (1.0 - lam_init)).astype(BF16)
        yield

    for h0 in range(0, H_B, ATT_CTX_LOCKSTEP):
        for _ in zip(*[head(h) for h in range(h0, h0 + ATT_CTX_LOCKSTEP)]):
            pass


def diff_attention_ctx(proj, lam_p, qg, kg, dg, layer, n_layers, prev_kv, nb, seq, lam_init):
    m = nb * seq
    vec = pl.BlockSpec((1, LANES), lambda b: (0, 0))
    slab = lambda base: pl.BlockSpec((H_B, seq, LANES), lambda b: (base // H_B, b, 0))
    outb = pl.BlockSpec((seq, H_B * LANES), lambda b: (b, 0))
    kvb = pl.BlockSpec((None, None, seq, H_B * LANES), lambda b: (b, layer, 0, 0))
    kv_shape = jax.ShapeDtypeStruct((nb, n_layers, seq, H_B * LANES), F32)
    prev = () if prev_kv is None else tuple(prev_kv)
    return pl.pallas_call(
        functools.partial(_attn_ctx_kernel, lam_init=lam_init),
        grid=(nb,),
        in_specs=[pl.BlockSpec((4, DB), lambda b: (0, 0)), slab(EB_BQ), slab(EB_BK), slab(EB_BV), vec, vec, vec]
        + [pl.BlockSpec(memory_space=pl.ANY)] * len(prev),
        out_specs=[outb, kvb, kvb],
        out_shape=[jax.ShapeDtypeStruct((m, B_V), BF16), kv_shape, kv_shape],
        input_output_aliases={7: 1, 8: 2} if prev else {},
        compiler_params=_cparams(("parallel",)),
        name="diff_attention_ctx",
    )(lam_p, proj, proj, proj, qg, kg, dg, *prev)


def _attn_dec_kernel(lam_ref, q_ref, k_ref, v_ref, kc_ref, vc_ref, qg_ref, kg_ref, dg_ref, cq_ref, sq_ref, ck_ref,
                     sk_ref, o_ref, ks_ref, vs_ref, *, lam_init, past):
    @pl.when(pl.program_id(2) == 0)
    def _():
        ks_ref[0:past, :] = kc_ref[...].astype(BF16)
        vs_ref[0:past, :] = vc_ref[...].astype(BF16)
        kd = _rope(_half_rms_norm(k_ref[...], kg_ref[...]), ck_ref[...], sk_ref[...])
        ks_ref[past:, :] = kd.astype(BF16)
        vs_ref[past:, :] = v_ref[...].astype(BF16)

    q = _rope(_half_rms_norm(q_ref[...], qg_ref[...]), cq_ref[...], sq_ref[...])
    lam = _lambda_value(lam_ref, lam_init)
    o_ref[...] = _diff_attn_core(q, ks_ref[...], vs_ref[...], lam, dg_ref[...], lam_init).astype(BF16)


def diff_attention_dec(proj, cache_k, cache_v, lam_p, qg, kg, dg, cos, sin, row0, nb, seq, lam_init):
    past = cache_k.shape[1]
    nq = seq // ATT_TQ
    rq, rk = row0 // ATT_TQ, row0 // seq
    vec = pl.BlockSpec((1, LANES), lambda b, h, i: (0, 0))
    kslab = lambda base: pl.BlockSpec((None, seq, LANES), lambda b, h, i: (base + h, rk + b, 0))
    cslab = pl.BlockSpec((None, past, LANES), lambda b, h, i: (b, 0, h))
    tq = pl.BlockSpec((ATT_TQ, LANES), lambda b, h, i: (i, 0))
    tk = pl.BlockSpec((seq, LANES), lambda b, h, i: (0, 0))
    return pl.pallas_call(
        functools.partial(_attn_dec_kernel, lam_init=lam_init, past=past),
        grid=(nb, H_B, nq),
        in_specs=[pl.BlockSpec((4, DB), lambda b, h, i: (0, 0)),
                  pl.BlockSpec((None, ATT_TQ, LANES), lambda b, h, i: (EB_BQ + h, rq + b * nq + i, 0)),
                  kslab(EB_BK), kslab(EB_BV), cslab, cslab, vec, vec, vec, tq, tq, tk, tk],
        out_specs=pl.BlockSpec((ATT_TQ, LANES), lambda b, h, i: (b * nq + i, h)),
        out_shape=jax.ShapeDtypeStruct((nb * seq, B_V), BF16),
        scratch_shapes=[pltpu.VMEM((past + seq, LANES), BF16), pltpu.VMEM((past + seq, LANES), BF16)],
        compiler_params=_cparams(("parallel", "parallel", "arbitrary")),
        name="diff_attention_dec",
    )(lam_p, proj, proj, proj, cache_k, cache_v, qg, kg, dg, cos, sin, cos, sin)


def rope_tables(n_tokens):
    n_rows = n_tokens // GRID_W
    rows, cols = jnp.meshgrid(jnp.arange(n_rows, dtype=F32), jnp.arange(GRID_W, dtype=F32), indexing='ij')
    inv_freq = ROPE_THETA ** (-jnp.arange(ROPE_PAIRS, dtype=F32) / ROPE_PAIRS)
    ang = jnp.stack([rows.reshape(-1, 1) * inv_freq, cols.reshape(-1, 1) * inv_freq], axis=0)
    cos, sin = jnp.cos(ang), jnp.sin(ang)
    cos_half = jnp.concatenate([cos[0], cos[0], cos[1], cos[1]], axis=-1)
    sin_half = jnp.concatenate([-sin[0], sin[0], -sin[1], sin[1]], axis=-1)
    return jnp.tile(cos_half, (1, 2)), jnp.tile(sin_half, (1, 2))


def _tri(n, lower):
    r = lax.broadcasted_iota(jnp.int32, (n, n), 0)
    c = lax.broadcasted_iota(jnp.int32, (n, n), 1)
    return jnp.where((r >= c) if lower else (r <= c), 1.0, 0.0).astype(BF16)


def _mlstm_stage_a(d, rows, slot, q_ref, k_ref, v_ref, gcol, grow, tri_l, tri_u, ia_ref, ba_ref, mi_ref, ua_ref,
                   sc_ref):
    lc = MLSTM_CHUNK
    ri = lax.broadcasted_iota(jnp.int32, (lc, lc), 0)
    ci = lax.broadcasted_iota(jnp.int32, (lc, lc), 1)
    mask = (ci <= ri) if d == 0 else (ci >= ri)
    lo = _lane_iota((lc, LANES)) < DK_A
    nt = (((1,), (1,)), ((), ()))
    bcol_all = _dot3_r(tri_l if d == 0 else tri_u, jax.nn.log_sigmoid(gcol))
    brow_all = _dot3_l(jax.nn.log_sigmoid(grow), tri_u if d == 0 else tri_l)
    yield
    q = q_ref[rows, :]
    k = k_ref[rows, :] * (DK_A ** -0.5)
    kb = k.astype(BF16)
    ones = jnp.ones((lc, LANES), BF16)
    vs = [v_ref[j, rows, :].astype(BF16) for j in range(2)]
    heads = (0, 1)
    qk = [lax.dot_general((jnp.where(lo, q, 0.0) if j == 0 else jnp.where(lo, 0.0, q)).astype(BF16), kb, nt,
                          preferred_element_type=F32) for j in heads]
    b_all = [jnp.broadcast_to(bcol_all[:, 16 + 8 * d + j:16 + 8 * d + j + 1], (lc, LANES)) for j in heads]
    li_all = [jnp.broadcast_to(gcol[:, 8 * d + j:8 * d + j + 1], (lc, LANES)) for j in heads]
    yield
    dmat = [jnp.where(mask, b_all[j][:, :lc] - brow_all[16 + 8 * d + j:16 + 8 * d + j + 1, :]
                      + grow[8 * d + j:8 * d + j + 1, :], -jnp.inf) for j in heads]
    m_intra = [jnp.max(dmat[j], axis=-1, keepdims=True) for j in heads]
    b_end = [b_all[j][lc - 1:lc, :] if d == 0 else b_all[j][0:1, :] for j in heads]
    g_all = [b_end[j] - b_all[j] + li_all[j] for j in heads]
    g_max = [jnp.max(g_all[j], axis=0, keepdims=True) for j in heads]
    yield
    s = [(qk[j] * jnp.exp(dmat[j] - m_intra[j])).astype(BF16) for j in heads]
    wkb = (k * jnp.where(lo, jnp.exp(g_all[0] - g_max[0]), jnp.exp(g_all[1] - g_max[1]))).astype(BF16)
    yield
    for j in heads:
        ia_ref[slot, d, j] = _dot(s[j], jnp.concatenate([vs[j], ones], axis=1))
        ba_ref[slot, d, j] = b_all[j]
        mi_ref[slot, d, j] = jnp.broadcast_to(m_intra[j], (lc, LANES))
        sc_ref[slot, d, j, 0:1, :] = g_max[j]
        sc_ref[slot, d, j, 1:2, :] = b_end[j]
    tn = (((0,), (0,)), ((), ()))
    ua_ref[slot, d] = lax.dot_general(wkb, jnp.concatenate(vs + [ones], axis=1), tn, preferred_element_type=F32)
    yield


def _mlstm_stage_b(d, rows, slot, q_ref, m_prev, cn_ref, h_out, m_out, ia_ref, ba_ref, mi_ref, ua_ref, sc_ref):
    lc = MLSTM_CHUNK
    lo = _lane_iota((lc, LANES)) < DK_A
    heads = (0, 1)
    q = q_ref[rows, :]
    cn = cn_ref[d]
    cnb = cn.astype(BF16)
    inter_nd = [_dot((jnp.where(lo, q, 0.0) if j == 0 else jnp.where(lo, 0.0, q)).astype(BF16), cnb) for j in heads]
    yield
    inter = [ba_ref[slot, d, j] + m_prev[j] for j in heads]
    m_row = [jnp.maximum(inter[j], mi_ref[slot, d, j]) for j in heads]
    w_inter = [jnp.exp(inter[j] - m_row[j]) for j in heads]
    w_intra = [jnp.exp(mi_ref[slot, d, j] - m_row[j]) for j in heads]
    g_max = [sc_ref[slot, d, j, 0:1, :] for j in heads]
    b_end = [sc_ref[slot, d, j, 1:2, :] for j in heads]
    m_new = [jnp.maximum(b_end[j] + m_prev[j], g_max[j]) for j in heads]
    decay = [jnp.exp(b_end[j] + m_prev[j] - m_new[j]) for j in heads]
    fac = [jnp.exp(g_max[j] - m_new[j]) for j in heads]
    m_out.extend(m_new)
    yield
    u = ua_ref[slot, d]
    top = lax.broadcasted_iota(jnp.int32, (LANES, LANES), 0) < DK_A
    dec = jnp.where(top, decay[0], decay[1])
    f = jnp.where(top, fac[0], fac[1])
    c_new = dec * cn[:, :LANES] + f * jnp.where(top, u[:, :LANES], u[:, LANES:2 * LANES])
    n_new = dec * cn[:, LANES:] + f * u[:, 2 * LANES:]
    cn_ref[d] = jnp.concatenate([c_new, n_new], axis=1)
    yield
    hs = []
    for j in heads:
        intra_nd = ia_ref[slot, d, j]
        num = w_inter[j] * inter_nd[j][:, :LANES] + w_intra[j] * intra_nd[:, :LANES]
        den = w_inter[j] * inter_nd[j][:, LANES:] + w_intra[j] * intra_nd[:, LANES:]
        hs.append(num * (1.0 / jnp.maximum(jnp.abs(den), jnp.exp(-m_row[j]))))
    h_out[rows, :] = jnp.concatenate(hs, axis=-1)
    yield


def _mlstm_kernel(q_ref, k_ref, v_ref, ao_ref, g_ref, bias_ref, gain_ref, c0_ref, n0_ref, m0_ref,
                  h_ref, cf_ref, nf_ref, mf_ref, gc_ref, gt_ref, hf_ref, hb_ref, cn_ref, ia_ref, ba_ref, mi_ref,
                  ua_ref, sc_ref, *, seq):
    lc = MLSTM_CHUNK
    nc = seq // lc
    pair = pl.program_id(1)
    shift = (LANES - 2 * pair) % LANES
    bias = pltpu.roll(jnp.broadcast_to(bias_ref[...], (8, LANES)), shift, 1)[0:1]
    for blk in range(seq // LANES):
        r = slice(blk * LANES, (blk + 1) * LANES)
        gs = pltpu.roll(g_ref[r, :], shift, 1) + bias
        gc_ref[r, :] = gs
        gst = gs.T
        for half in range(LANES // lc):
            gt_ref[blk * (LANES // lc) + half] = gst[0:32, half * lc:(half + 1) * lc]
    for d in range(2):
        cn_ref[d, :, 0:LANES] = c0_ref[d]
        cn_ref[d, :, LANES:] = jnp.broadcast_to(n0_ref[d], (LANES, LANES)).T
    tri_l, tri_u = _tri(lc, True), _tri(lc, False)
    stage = (ia_ref, ba_ref, mi_ref, ua_ref, sc_ref)

    def chunk_rows(c):
        return pl.ds(pl.multiple_of(c * lc, lc), lc)

    def stage_a(i, slot):
        gens = []
        for d, c in ((0, i), (1, nc - 1 - i)):
            rows = chunk_rows(c)
            gens.append(_mlstm_stage_a(d, rows, slot, q_ref, k_ref, v_ref, gc_ref[rows, :], gt_ref[c], tri_l, tri_u,
                                       *stage))
        return gens

    def stage_b(i, slot, ms, outs):
        return [_mlstm_stage_b(d, chunk_rows(c), slot, q_ref, ms[2 * d:2 * d + 2], cn_ref, h_out, outs[d], *stage)
                for d, c, h_out in ((0, i, hf_ref), (1, nc - 1 - i, hb_ref))]

    def run(gens):
        for _ in itertools.zip_longest(*gens):
            pass

    def body(kk, ms):
        outs = ([], [])
        run(stage_b(2 * kk, 0, ms, outs) + stage_a(2 * kk + 1, 1))
        ms = tuple(outs[0] + outs[1])
        outs = ([], [])
        run(stage_b(2 * kk + 1, 1, ms, outs) + stage_a(jnp.minimum(2 * kk + 2, nc - 1), 0))
        return tuple(outs[0] + outs[1])

    m0 = m0_ref[...]
    rep = lambda v: jnp.broadcast_to(v, (1, LANES))
    init = (rep(m0[0][:, 0:1]), rep(m0[0][:, DK_A:DK_A + 1]), rep(m0[1][:, 0:1]), rep(m0[1][:, DK_A:DK_A + 1]))
    run(stage_a(0, 0))
    m_f0, m_f1, m_b0, m_b1 = lax.fori_loop(0, nc // 2, body, init)
    lane1 = _lane_iota((1, LANES)) < DK_A
    mf_ref[0] = jnp.where(lane1, m_f0, m_f1)
    mf_ref[1] = jnp.where(lane1, m_b0, m_b1)
    for d in range(2):
        cf_ref[d] = cn_ref[d, :, 0:LANES]
        nf_ref[d] = cn_ref[d, :, LANES:].T[0:1, :]

    gain = gain_ref[...]
    for blk in range(seq // LANES):
        r = slice(blk * LANES, (blk + 1) * LANES)
        outs = []
        for j in range(2):
            cs = slice(j * DV_A, (j + 1) * DV_A)
            hh = hf_ref[r, cs] + hb_ref[r, cs]
            ms = jnp.mean(hh * hh, axis=-1, keepdims=True)
            outs.append(hh * lax.rsqrt(ms + RMS_EPS) * gain[:, cs] * jax.nn.sigmoid(ao_ref[j, r, :]))
        h_ref[r, :] = jnp.concatenate(outs, axis=-1).astype(BF16)


def mlstm(proj, bias, gain, c0, n0, m0, row0, nb, seq):
    rb = row0 // seq
    npair = H_A // 2
    slab = lambda base: pl.BlockSpec((None, seq, LANES), lambda b, p: (base + p, rb + b, 0))
    slab2 = lambda base: pl.BlockSpec((2, seq, LANES), lambda b, p: (base // 2 + p, rb + b, 0))
    st_c = pl.BlockSpec((None, None, 2, LANES, LANES), lambda b, p: (b, p, 0, 0, 0))
    st_v = pl.BlockSpec((None, None, 2, 1, LANES), lambda b, p: (b, p, 0, 0, 0))
    return pl.pallas_call(
        functools.partial(_mlstm_kernel, seq=seq),
        grid=(nb, npair),
        in_specs=[slab(EB_Q), slab(EB_K), slab2(EB_V), slab2(EB_O),
                  pl.BlockSpec((None, seq, LANES), lambda b, p: (EB_G, rb + b, 0)),
                  pl.BlockSpec((1, LANES), lambda b, p: (0, 0)),
                  pl.BlockSpec((1, 2 * DV_A), lambda b, p: (0, p)),
                  st_c, st_v, st_v],
        out_specs=[pl.BlockSpec((seq, 2 * DV_A), lambda b, p: (b, p)), st_c, st_v, st_v],
        out_shape=[jax.ShapeDtypeStruct((nb * seq, A_V), BF16),
                   jax.ShapeDtypeStruct((nb, npair, 2, LANES, LANES), F32),
                   jax.ShapeDtypeStruct((nb, npair, 2, 1, LANES), F32),
                   jax.ShapeDtypeStruct((nb, npair, 2, 1, LANES), F32)],
        scratch_shapes=[pltpu.VMEM((seq, LANES), F32), pltpu.VMEM((seq // MLSTM_CHUNK, 32, MLSTM_CHUNK), F32),
                        pltpu.VMEM((seq, 2 * DV_A), F32),
                        pltpu.VMEM((seq, 2 * DV_A), F32), pltpu.VMEM((2, LANES, 2 * LANES), F32),
                        pltpu.VMEM((2, 2, 2, MLSTM_CHUNK, 2 * LANES), F32), pltpu.VMEM((2, 2, 2, MLSTM_CHUNK, LANES), F32),
                        pltpu.VMEM((2, 2, 2, MLSTM_CHUNK, LANES), F32), pltpu.VMEM((2, 2, LANES, 3 * LANES), F32),
                        pltpu.VMEM((2, 2, 2, 8, LANES), F32)],
        compiler_params=_cparams(("parallel", "parallel")),
        name="mlstm",
    )(proj, proj, proj, proj, proj, bias, gain, c0, n0, m0)


def _softplus(x):
    return jnp.maximum(x, 0.0) + jnp.log1p(jnp.exp(-jnp.abs(x)))


SSD_REP = 2 * R_C


def _pack3(x, lane):
    hi = x.astype(BF16).astype(F32)
    r1 = x - hi
    mid = r1.astype(BF16).astype(F32)
    r2 = r1 - mid
    grp = lane % (3 * SSD_REP)
    return jnp.where(grp < SSD_REP, x, jnp.where(grp < 2 * SSD_REP, r1, r2)).astype(BF16)


SSD_HALO = 8


def _ssd_kernel(xs_ref, bm_ref, cm_ref, z_ref, dtr_ref, wx_ref, wb_ref, wc_ref, bx_ref, bb_ref, bc_ref,
                 dtb_ref, alog_ref, dsk_ref, rep_ref, selw_ref, expf_ref, expb_ref, *rest, seq, has_init, has_prev):
    if has_init:
        s0_ref = rest[0]
        rest = rest[1:]
    if has_prev:
        rest = rest[1:]
    (y_ref, sfin_ref, pad_ref, xa_ref, ba_ref, ca_ref, dt_ref, af_ref, ab_ref, st_ref, ex_ref, abw_ref,
     at_ref, cb_ref) = rest
    lc = SSD_CHUNK
    nc = seq // lc
    gw = R_C * P_C
    nslab = gw // LANES
    lane = _lane_iota((lc, LANES))
    lo = lane < P_C
    fwd_lane = (lane % SSD_REP) < R_C
    first3 = lane < 3 * SSD_REP
    a_neg = -jnp.exp(alog_ref[...])
    tri_l, tri_u = _tri(lc, True), _tri(lc, False)
    ri = lax.broadcasted_iota(jnp.int32, (lc, lc), 0)
    ci = lax.broadcasted_iota(jnp.int32, (lc, lc), 1)
    causal, anti = ci <= ri, ci >= ri
    nt = (((1,), (1,)), ((), ()))

    def chunk_rows(c):
        return pl.ds(pl.multiple_of(c * lc, lc), lc)

    halo = jnp.zeros((SSD_HALO, LANES), F32)
    for s in range(nslab + 2):
        pad_ref[s, 0:SSD_HALO, :] = halo
        pad_ref[s, SSD_HALO + seq:, :] = halo
        src = xs_ref[s] if s < nslab else (bm_ref[...] if s == nslab else cm_ref[...])
        pad_ref[s, SSD_HALO:SSD_HALO + seq, :] = src

    def conv_chunk(s, base, w, b):
        y = w[0:1] * pad_ref[s, pl.ds(base + SSD_HALO - 2, lc), :]
        y = y + w[1:2] * pad_ref[s, pl.ds(base + SSD_HALO - 1, lc), :]
        y = y + w[2:3] * pad_ref[s, pl.ds(base + SSD_HALO, lc), :]
        y = y + w[3:4] * pad_ref[s, pl.ds(base + SSD_HALO + 1, lc), :]
        return _silu(y + b)

    def prep_body(c, carry):
        base = pl.multiple_of(c * lc, lc)
        rows = pl.ds(base, lc)
        dt_pre = _dot3_l(dtr_ref[rows, :], rep_ref[...])
        for s in range(nslab // 2):
            cs = slice(s * LANES, (s + 1) * LANES)
            xa_ref[rows, cs] = conv_chunk(s, base, wx_ref[:, cs], bx_ref[:, cs])
        dt = _softplus(dt_pre + dtb_ref[...])
        dt_ref[rows, :] = dt
        da = dt * a_neg
        acum_f, acum_b = _dot3_r(tri_l, da), _dot3_r(tri_u, da)
        for s in range(nslab // 2, nslab):
            cs = slice(s * LANES, (s + 1) * LANES)
            xa_ref[rows, cs] = conv_chunk(s, base, wx_ref[:, cs], bx_ref[:, cs])
        ba_ref[rows, :] = conv_chunk(nslab, base, wb_ref[...], bb_ref[...])
        ca_ref[rows, :] = conv_chunk(nslab + 1, base, wc_ref[...], bc_ref[...])
        af_ref[rows, :] = acum_f
        ab_ref[rows, :] = acum_b
        return carry

    lax.fori_loop(0, nc, prep_body, 0, unroll=2)

    for d in range(2):
        for k in range(nslab):
            cs = slice(k * LANES, (k + 1) * LANES)
            st_ref[d, :, cs] = s0_ref[d, cs, :].T if has_init else jnp.zeros((N_C, LANES), F32)

    def expand(ea, te, e_ref):
        return _dot(_pack3(jnp.where(first3, ea, te), lane), e_ref[...])

    def fwd_a(c, slot):
        rows = chunk_rows(c)
        acum = af_ref[rows, :]
        a_end = acum[lc - 1:lc, :]
        ea, te = jnp.exp(acum), jnp.exp(a_end - acum) * dt_ref[rows, :]
        yield
        ex_ref[slot] = expand(ea, te, expf_ref)
        yield

    def fwd_b(c, slot):
        rows = chunk_rows(c)
        ea_x, te_x = ex_ref[slot, :, 0:gw], ex_ref[slot, :, gw:2 * gw]
        st = st_ref[0]
        cs_ = _dot(ca_ref[rows, :].astype(BF16), st.astype(BF16))
        bc_t = ba_ref[rows, :].T.astype(BF16)
        yield
        upd = _dot(bc_t, (xa_ref[rows, :] * te_x).astype(BF16))
        yield
        y_ref[rows, :] = cs_ * ea_x
        st_ref[0] = st * ea_x[lc - 1:lc, :] + upd
        yield

    def bwd_a(c, slot):
        rows = chunk_rows(c)
        dt = dt_ref[rows, :]
        acum_b = ab_ref[rows, :]
        acum = jnp.where(fwd_lane, af_ref[rows, :], acum_b)
        at_ref[slot, 0] = acum.T
        at_ref[slot, 1] = dt.T
        yield
        abw_ref[slot] = _dot(_pack3(acum, lane), selw_ref[...])
        yield
        a_end = acum_b[0:1, :]
        ex_ref[slot] = expand(jnp.exp(acum_b), jnp.exp(a_end - acum_b) * dt, expb_ref)
        yield
        cb_ref[slot] = lax.dot_general(ca_ref[rows, :].astype(BF16), ba_ref[rows, :].astype(BF16), nt,
                                       preferred_element_type=F32)
        yield

    def bwd_b(c, slot):
        rows = chunk_rows(c)
        ea_x, te_x = ex_ref[slot, :, 0:gw], ex_ref[slot, :, gw:2 * gw]
        cb = cb_ref[slot]
        st = st_ref[1]
        xs = xa_ref[rows, :]
        y_inter = _dot(ca_ref[rows, :].astype(BF16), st.astype(BF16)) * ea_x
        bc_t = ba_ref[rows, :].T.astype(BF16)
        upd = _dot(bc_t, (xs * te_x).astype(BF16))
        yield
        for s in range(nslab):
            cs = slice(s * LANES, (s + 1) * LANES)
            ws = []
            for j in range(2):
                r = 2 * s + j
                rb = R_C + r
                aj_f, aj_b = at_ref[slot, 0, r:r + 1, :], at_ref[slot, 0, rb:rb + 1, :]
                dt_f, dt_b = at_ref[slot, 1, r:r + 1, :], at_ref[slot, 1, rb:rb + 1, :]
                wf = jnp.exp(jnp.where(causal, abw_ref[slot, :, r * LANES:(r + 1) * LANES] - aj_f, -jnp.inf))
                wb = jnp.exp(jnp.where(anti, abw_ref[slot, :, rb * LANES:(rb + 1) * LANES] - aj_b, -jnp.inf))
                ws.append((cb * (wf * dt_f + wb * dt_b)).astype(BF16))
            xsl = xs[:, cs]
            x2 = jnp.concatenate([jnp.where(lo, xsl, 0.0), jnp.where(lo, 0.0, xsl)], axis=0).astype(BF16)
            y = y_ref[rows, cs] + y_inter[:, cs] + dsk_ref[:, cs] * xsl + _dot(jnp.concatenate(ws, axis=1), x2)
            y_ref[rows, cs] = y * _silu(z_ref[s, rows, :])
            yield
        st_ref[1] = st * ea_x[0:1, :] + upd
        yield

    def run(*gens):
        for _ in itertools.zip_longest(*gens):
            pass

    def sweep(stage_a, stage_b, order):
        run(stage_a(order(0), 0))

        def body(k, carry):
            run(stage_b(order(2 * k), 0), stage_a(order(2 * k + 1), 1))
            run(stage_b(order(2 * k + 1), 1), stage_a(order(jnp.minimum(2 * k + 2, nc - 1)), 0))
            return carry

        lax.fori_loop(0, nc // 2, body, 0)

    sweep(fwd_a, fwd_b, lambda i: i)
    sweep(bwd_a, bwd_b, lambda i: nc - 1 - i)

    for d in range(2):
        for k in range(nslab):
            cs = slice(k * LANES, (k + 1) * LANES)
            sfin_ref[d, cs, :] = st_ref[d, :, cs].T


def ssd(proj, conv_w, conv_b, dtb, alog, dsk, s0, layer, n_layers, prev_states, nb, seq):
    gw = R_C * P_C
    nslab = gw // LANES
    slab = lambda base: pl.BlockSpec((None, seq, LANES), lambda b, g: (base + g, b, 0))
    slab4 = lambda base: pl.BlockSpec((nslab, seq, LANES), lambda b, g: (base // nslab + g, b, 0))
    xo, bo, co = 0, D_INNER // LANES, (D_INNER + G_C * N_C) // LANES
    cw = lambda rws, width, off: pl.BlockSpec((rws, width), lambda b, g: (0, off * LANES // width + g))
    vec = pl.BlockSpec((None, 1, LANES), lambda b, g: (g, 0, 0))
    full = lambda a: pl.BlockSpec(a.shape, lambda b, g: (0,) * a.ndim)
    st_in = pl.BlockSpec((None, None, 2, gw, N_C), lambda b, g: (b, layer, 0, g, 0))
    st_out = pl.BlockSpec((None, None, 2, gw, N_C), lambda b, g: (b, layer, 0, g, 0))
    rep, selw, expf, expb = _ssd_constants()
    has_init, has_prev = s0 is not None, prev_states is not None
    st_shape = jax.ShapeDtypeStruct((nb, n_layers, 2, H_C * P_C, N_C), F32)
    vm = lambda *shape: pltpu.VMEM(shape, F32)
    n_in = 18 + has_init
    return pl.pallas_call(
        functools.partial(_ssd_kernel, seq=seq, has_init=has_init, has_prev=has_prev),
        grid=(nb, G_C),
        in_specs=[slab4(OB_X), slab(OB_B), slab(OB_C), slab4(OB_Z),
                  pl.BlockSpec((None, seq, LANES), lambda b, g: (OB_DT, b, 0)),
                  cw(SSD_CONV, gw, xo), cw(SSD_CONV, LANES, bo), cw(SSD_CONV, LANES, co),
                  cw(1, gw, xo), cw(1, LANES, bo), cw(1, LANES, co),
                  vec, vec, pl.BlockSpec((1, gw), lambda b, g: (0, g)),
                  pl.BlockSpec((None, LANES, LANES), lambda b, g: (g, 0, 0)), full(selw), full(expf), full(expb)]
        + ([st_in] if has_init else []) + ([pl.BlockSpec(memory_space=pl.ANY)] if has_prev else []),
        out_specs=[pl.BlockSpec((seq, gw), lambda b, g: (b, g)), st_out],
        out_shape=[jax.ShapeDtypeStruct((nb * seq, D_INNER), F32), st_shape],
        input_output_aliases={n_in: 1} if has_prev else {},
        scratch_shapes=[vm(nslab + 2, seq + 2 * SSD_HALO, LANES), vm(seq, gw), vm(seq, LANES), vm(seq, LANES),
                        vm(seq, LANES), vm(seq, LANES), vm(seq, LANES), vm(2, N_C, gw), vm(2, SSD_CHUNK, 2 * gw),
                        vm(2, SSD_CHUNK, SSD_REP * LANES), vm(2, 2, LANES, SSD_CHUNK), vm(2, SSD_CHUNK, SSD_CHUNK)],
        compiler_params=_cparams(("parallel", "parallel")),
        name="ssd",
    )(proj, proj, proj, proj, proj, conv_w, conv_w, conv_w, conv_b, conv_b, conv_b, dtb, alog, dsk,
      rep, selw, expf, expb, *((s0,) if has_init else ()), *((prev_states,) if has_prev else ()))


def _ssd_constants():
    gw = R_C * P_C
    rep = np.zeros((G_C, LANES, LANES), np.float32)
    for g in range(G_C):
        for l in range(LANES):
            c = l % SSD_REP
            rep[g, (R_C * g + c) if c < R_C else (H_C + R_C * g + c - R_C), l] = 1.0
    selw = np.zeros((LANES, SSD_REP * LANES), np.float32)
    expf = np.zeros((LANES, 2 * gw), np.float32)
    expb = np.zeros((LANES, 2 * gw), np.float32)
    for l in range(3 * SSD_REP):
        c = l % SSD_REP
        selw[l, c * LANES:(c + 1) * LANES] = 1.0
    for l in range(6 * SSD_REP):
        c, q = l % SSD_REP, l // (3 * SSD_REP)
        if c < R_C:
            expf[l, q * gw + c * P_C:q * gw + (c + 1) * P_C] = 1.0
        else:
            expb[l, q * gw + (c - R_C) * P_C:q * gw + (c - R_C + 1) * P_C] = 1.0
    return tuple(jnp.asarray(a, BF16) for a in (rep, selw, expf, expb))


def _pair_layout_c(c):
    nb = c.shape[0]
    c = c.transpose(0, 2, 1, 4, 3).reshape(nb, H_A // 2, 2, 2, DK_A, DV_A)
    return c.transpose(0, 1, 3, 2, 4, 5).reshape(nb, H_A // 2, 2, 2 * DK_A, DV_A)


def _pair_layout_c_inv(c):
    nb = c.shape[0]
    c = c.reshape(nb, H_A // 2, 2, 2, DK_A, DV_A).transpose(0, 2, 1, 3, 5, 4)
    return c.reshape(nb, 2, H_A, DV_A, DK_A)


def _pair_layout_n(n):
    nb = n.shape[0]
    return n.reshape(nb, 2, H_A // 2, 2 * DK_A).transpose(0, 2, 1, 3).reshape(nb, H_A // 2, 2, 1, 2 * DK_A)


def _pair_layout_n_inv(n):
    nb = n.shape[0]
    return n.reshape(nb, H_A // 2, 2, 2 * DK_A).transpose(0, 2, 1, 3).reshape(nb, 2, H_A, DK_A)


def _pair_layout_m(m):
    return _pair_layout_n(jnp.repeat(m[..., None], DK_A, axis=-1))


def _pair_layout_m_inv(m):
    return _pair_layout_n_inv(m)[..., 0]


def _group_replicated(v):
    t = v.reshape(2, G_C, R_C).transpose(1, 0, 2).reshape(G_C, 1, SSD_REP)
    return jnp.tile(t, (1, 1, LANES // SSD_REP))


def kernel(x_prompt, x_sample, cache_attn_k, cache_attn_v, state_mlstm_c, state_mlstm_n, state_mlstm_m, state_ssd,
           c, c_ctx, w_ada, b_ada, norm_mix, norm_ffn, w_gate, w_up, w_down, w_in_even, b_gate_mlstm, mlstm_norm,
           q_norm, k_norm, lambda_q1, lambda_k1, lambda_q2, lambda_k2, diff_norm, w_out_even, w_in_odd, conv_w,
           conv_b, dt_bias, a_log, d_skip, ssd_norm, w_out_odd):
    nb_c, seq_c, d = x_prompt.shape
    nb_d, seq_d, _ = x_sample.shape
    n_ctx, n_dec = nb_c * seq_c, nb_d * seq_d
    depth = w_ada.shape[0]

    xc, xd = x_prompt.reshape(n_ctx, d), x_sample.reshape(n_dec, d)
    cvecs = jnp.concatenate([c_ctx[None, :], c, jnp.zeros((8 - 1 - nb_d, d), F32)], axis=0)
    mods = ada_modulation(cvecs, w_ada, b_ada)

    def mod(l, k):
        return mods[l, :, k * d:(k + 1) * d].reshape(8, 1, d)

    cos, sin = rope_tables(seq_d)
    n_even, n_odd = w_in_even.shape[0], w_in_odd.shape[0]
    gate0 = 2 * A_QK + 2 * A_V
    w_even = jnp.concatenate([w_in_even[:, :, :gate0], w_in_even[:, :, gate0 + 4 * H_A:],
                              w_in_even[:, :, gate0:gate0 + 4 * H_A]], axis=2)
    w_even = jnp.pad(w_even, ((0, 0), (0, 0), (0, E_COLS - w_even.shape[2]))).astype(BF16)
    w_odd = jnp.pad(w_in_odd, ((0, 0), (0, 0), (0, O_COLS - w_in_odd.shape[2]))).astype(BF16)
    wo_even, wo_odd = w_out_even.astype(BF16), w_out_odd.astype(BF16)
    ffn_w = (w_gate.astype(BF16), w_up.astype(BF16), w_down.astype(BF16))

    new_c, new_n, new_m = [], [], []
    new_kv, new_s = None, None
    for l in range(depth):
        i = l // 2
        sh1, sc1, g1, sh2, sc2, g2 = (mod(l, k) for k in range(6))
        if l % 2 == 0:
            gmix = norm_mix[l][None]
            pc = in_projection(xc, gmix, sc1, sh1, w_even, i, E_TN, None)
            pd = in_projection(xd, gmix, sc1, sh1, w_even, i, E_TN, seq_d)
            bias = jnp.pad(b_gate_mlstm[i][None], ((0, 0), (0, LANES - 4 * H_A)))
            gain_a = mlstm_norm[i][None]
            zc = jnp.zeros((nb_c, H_A // 2, 2, LANES, LANES), F32)
            zv = jnp.zeros((nb_c, H_A // 2, 2, 1, LANES), F32)
            h_c, cf, nf, mf = mlstm(pc, bias, gain_a, zc, zv, zv, 0, nb_c, seq_c)
            h_d, _, _, _ = mlstm(pd, bias, gain_a, _pair_layout_c(state_mlstm_c[:, i]),
                                 _pair_layout_n(state_mlstm_n[:, i]), _pair_layout_m(state_mlstm_m[:, i]),
                                 0, nb_d, seq_d)
            new_c.append(_pair_layout_c_inv(cf))
            new_n.append(_pair_layout_n_inv(nf[:, :, :, 0]))
            new_m.append(_pair_layout_m_inv(mf[:, :, :, 0]))

            lam_init = 0.8 - 0.6 * math.exp(-0.3 * l)
            lam_p = jnp.stack([lambda_q1[i], lambda_k1[i], lambda_q2[i], lambda_k2[i]], axis=0)
            qg = jnp.tile(q_norm[i][None], (1, 2))
            kg = jnp.tile(k_norm[i][None], (1, 2))
            dg = diff_norm[i][None]
            o_c, *new_kv = diff_attention_ctx(pc, lam_p, qg, kg, dg, i, n_even, new_kv, nb_c, seq_c, lam_init)
            past = cache_attn_k.shape[2]
            o_d = diff_attention_dec(pd, cache_attn_k[:, i].reshape(nb_d, past, B_QK),
                                     cache_attn_v[:, i].reshape(nb_d, past, B_V), lam_p, qg, kg, dg, cos, sin,
                                     0, nb_d, seq_d, lam_init)
            xc = out_projection_even(xc, g1, h_c, o_c, wo_even, i, None)
            xd = out_projection_even(xd, g1, h_d, o_d, wo_even, i, seq_d)
        else:
            gmix = norm_mix[l][None]
            pc = in_projection(xc, gmix, sc1, sh1, w_odd, i, O_TN, None)
            pd = in_projection(xd, gmix, sc1, sh1, w_odd, i, O_TN, seq_d)
            dtb, alog = _group_replicated(dt_bias[i]), _group_replicated(a_log[i])
            dsk = jnp.repeat(d_skip[i], P_C)[None]
            cw, cb = conv_w[i], conv_b[i][None]
            y_c, new_s = ssd(pc, cw, cb, dtb, alog, dsk, None, i, n_odd, new_s, nb_c, seq_c)
            y_d, _ = ssd(pd, cw, cb, dtb, alog, dsk, state_ssd.reshape(nb_d, n_odd, 2, H_C * P_C, N_C), i, n_odd, None,
                         nb_d, seq_d)
            gs = ssd_norm[i][None]
            xc = out_projection_odd(xc, g1, y_c, gs, wo_odd, i, None)
            xd = out_projection_odd(xd, g1, y_d, gs, wo_odd, i, seq_d)
        xc = ffn(xc, norm_ffn[l][None], sc2, sh2, g2, *ffn_w, l, None)
        xd = ffn(xd, norm_ffn[l][None], sc2, sh2, g2, *ffn_w, l, seq_d)

    y_prompt = xc.reshape(nb_c, seq_c, d)
    y_sample = xd.reshape(nb_d, seq_d, d)
    new_k = new_kv[0].reshape(nb_c, n_even, seq_c, H_B, 2, DB)
    new_v = new_kv[1].reshape(nb_c, n_even, seq_c, H_B, DVB)
    return (y_prompt, y_sample, new_k, new_v, jnp.stack(new_c, axis=1), jnp.stack(new_n, axis=1),
            jnp.stack(new_m, axis=1), new_s.reshape(nb_c, n_odd, 2, H_C, P_C, N_C))
```

```python
import functools
import itertools
import math

import jax
import jax.numpy as jnp
import numpy as np
from jax import lax
from jax.experimental import pallas as pl
from jax.experimental.pallas import tpu as pltpu

F32 = jnp.float32
BF16 = jnp.bfloat16

D_MODEL = 2048
DEPTH = 4
GRID_W = 64
RMS_EPS = 1e-6
H_A, DK_A, DV_A, MLSTM_CHUNK = 8, 64, 128, 64
H_B, DB, DVB = 8, 64, 128
ROPE_THETA = 10000.0
ROPE_PAIRS = DB // 4
D_INNER = 2 * D_MODEL
P_C, N_C, G_C = 64, 128, 8
H_C = D_INNER // P_C
R_C = H_C // G_C
SSD_CONV = 4
SSD_CHUNK = 128
D_FF = -(-8 * D_MODEL // (3 * 256)) * 256
A_QK, A_V = H_A * DK_A, H_A * DV_A
B_QK, B_V = H_B * 2 * DB, H_B * DVB
CONV_CH = D_INNER + 2 * G_C * N_C

LANES = 128
VMEM_LIMIT = 56 * 1024 * 1024

TM = 512
INP_TM = 1024
FFN_TM = 512
ADA_TN = 1024
E_COLS = 6400
E_TN = 1280
O_COLS = 10752
O_TN = 1536
FF_TN = 512
OUT_TN = 1024
ATT_TQ = 256
ATT_CTX_LOCKSTEP = 4

EB_Q, EB_K, EB_V, EB_O = 0, 4, 8, 16
EB_BQ, EB_BK, EB_BV, EB_G = 24, 32, 40, 48
OB_Z, OB_X, OB_B, OB_C, OB_DT = 0, 32, 64, 72, 80


def _cparams(sem):
    return pltpu.CompilerParams(dimension_semantics=sem, vmem_limit_bytes=VMEM_LIMIT)


def _silu(x):
    return x * jax.nn.sigmoid(x)


def _split3(x):
    hi = x.astype(BF16)
    r1 = x - hi.astype(F32)
    mid = r1.astype(BF16)
    lo = (r1 - mid.astype(F32)).astype(BF16)
    return hi, mid, lo


def _dot(a, b):
    return jnp.dot(a, b, preferred_element_type=F32)


def _dot3_l(x, m):
    hi, mid, lo = _split3(x)
    return _dot(hi, m) + _dot(mid, m) + _dot(lo, m)


def _dot3_r(m, x):
    hi, mid, lo = _split3(x)
    return _dot(m, hi) + _dot(m, mid) + _dot(m, lo)


def _row_group(i, tm, mod_rows):
    return 0 if mod_rows is None else 1 + (i * tm) // mod_rows


def _ada_kernel(c_ref, w_ref, b_ref, o_ref):
    s = _silu(c_ref[...]).astype(BF16)
    o_ref[...] = _dot(s, w_ref[...].astype(BF16)) + b_ref[...]


def ada_modulation(cvecs, w_ada, b_ada):
    depth, d, n = w_ada.shape
    return pl.pallas_call(
        _ada_kernel,
        grid=(depth, n // ADA_TN),
        in_specs=[
            pl.BlockSpec((8, d), lambda l, j: (0, 0)),
            pl.BlockSpec((None, d, ADA_TN), lambda l, j: (l, 0, j)),
            pl.BlockSpec((None, 1, ADA_TN), lambda l, j: (l, 0, j)),
        ],
        out_specs=pl.BlockSpec((None, 8, ADA_TN), lambda l, j: (l, 0, j)),
        out_shape=jax.ShapeDtypeStruct((depth, 8, n), F32),
        compiler_params=_cparams(("parallel", "parallel")),
        name="ada_modulation",
    )(cvecs, w_ada, b_ada.reshape(depth, 1, n))


def _norm_mod(x, gain, sc, sh):
    ms = jnp.mean(x * x, axis=-1, keepdims=True)
    y = x * lax.rsqrt(ms + RMS_EPS) * gain
    return y * (1.0 + sc) + sh


def _inproj_kernel(x_ref, gain_ref, sc_ref, sh_ref, w_ref, o_ref, h_ref, *, nblk):
    @pl.when(pl.program_id(1) == 0)
    def _():
        h_ref[...] = _norm_mod(x_ref[...], gain_ref[...], sc_ref[...], sh_ref[...]).astype(BF16)

    acc = _dot(h_ref[...], w_ref[...])
    for k in range(nblk):
        o_ref[k] = acc[:, k * LANES:(k + 1) * LANES]


def in_projection(x, gain, sc, sh, w, layer, tn, mod_rows):
    m, d = x.shape
    n = w.shape[2]
    nblk = tn // LANES
    tm = INP_TM
    grp = lambda i, j: (_row_group(i, tm, mod_rows), 0, 0)
    return pl.pallas_call(
        functools.partial(_inproj_kernel, nblk=nblk),
        grid=(m // tm, n // tn),
        in_specs=[
            pl.BlockSpec((tm, d), lambda i, j: (i, 0)),
            pl.BlockSpec((1, d), lambda i, j: (0, 0)),
            pl.BlockSpec((None, 1, d), grp),
            pl.BlockSpec((None, 1, d), grp),
            pl.BlockSpec((None, d, tn), lambda i, j: (layer, 0, j)),
        ],
        out_specs=pl.BlockSpec((nblk, tm, LANES), lambda i, j: (j, i, 0)),
        out_shape=jax.ShapeDtypeStruct((n // LANES, m, LANES), F32),
        scratch_shapes=[pltpu.VMEM((tm, d), BF16)],
        compiler_params=_cparams(("parallel", "arbitrary")),
        name="in_projection",
    )(x, gain, sc, sh, w)


def _ffn_kernel(x_ref, gain_ref, sc_ref, sh_ref, g_ref, wg_ref, wu_ref, wd_ref, o_ref, h_ref, acc_ref):
    f = pl.program_id(1)

    @pl.when(f == 0)
    def _():
        h_ref[...] = _norm_mod(x_ref[...], gain_ref[...], sc_ref[...], sh_ref[...]).astype(BF16)
        acc_ref[...] = jnp.zeros_like(acc_ref)

    h = h_ref[...]
    gate = _dot(h, wg_ref[...])
    up = _dot(h, wu_ref[...])
    ff = (_silu(gate) * up).astype(BF16)
    acc_ref[...] += _dot(ff, wd_ref[...])

    @pl.when(f == pl.num_programs(1) - 1)
    def _():
        o_ref[...] = x_ref[...] + g_ref[...] * acc_ref[...]


def ffn(x, gain, sc, sh, g, wg, wu, wd, layer, mod_rows):
    m, d = x.shape
    dff = wg.shape[2]
    tm = FFN_TM
    grp = lambda i, f: (_row_group(i, tm, mod_rows), 0, 0)
    return pl.pallas_call(
        _ffn_kernel,
        grid=(m // tm, dff // FF_TN),
        in_specs=[
            pl.BlockSpec((tm, d), lambda i, f: (i, 0)),
            pl.BlockSpec((1, d), lambda i, f: (0, 0)),
            pl.BlockSpec((None, 1, d), grp),
            pl.BlockSpec((None, 1, d), grp),
            pl.BlockSpec((None, 1, d), grp),
            pl.BlockSpec((None, d, FF_TN), lambda i, f: (layer, 0, f)),
            pl.BlockSpec((None, d, FF_TN), lambda i, f: (layer, 0, f)),
            pl.BlockSpec((None, FF_TN, d), lambda i, f: (layer, f, 0)),
        ],
        out_specs=pl.BlockSpec((tm, d), lambda i, f: (i, 0)),
        out_shape=jax.ShapeDtypeStruct((m, d), F32),
        scratch_shapes=[pltpu.VMEM((tm, d), BF16), pltpu.VMEM((tm, d), F32)],
        compiler_params=_cparams(("parallel", "arbitrary")),
        name="ffn",
    )(x, gain, sc, sh, g, wg, wu, wd)


def _outproj_even_kernel(x_ref, g_ref, a_ref, b_ref, wa_ref, wb_ref, o_ref):
    acc = _dot(a_ref[...], wa_ref[...]) + _dot(b_ref[...], wb_ref[...])
    o_ref[...] = x_ref[...] + g_ref[...] * acc


def out_projection_even(x, g, ha, ob, w, layer, mod_rows):
    m, d = x.shape
    ka, kb = ha.shape[1], ob.shape[1]
    grp = lambda i: (_row_group(i, TM, mod_rows), 0, 0)
    return pl.pallas_call(
        _outproj_even_kernel,
        grid=(m // TM,),
        in_specs=[
            pl.BlockSpec((TM, d), lambda i: (i, 0)),
            pl.BlockSpec((None, 1, d), grp),
            pl.BlockSpec((TM, ka), lambda i: (i, 0)),
            pl.BlockSpec((TM, kb), lambda i: (i, 0)),
            pl.BlockSpec((None, ka, d), lambda i: (layer, 0, 0)),
            pl.BlockSpec((None, kb, d), lambda i: (layer, ka // kb, 0)),
        ],
        out_specs=pl.BlockSpec((TM, d), lambda i: (i, 0)),
        out_shape=jax.ShapeDtypeStruct((m, d), F32),
        compiler_params=_cparams(("parallel",)),
        name="out_projection_even",
    )(x, g, ha, ob, w, w)


def _outproj_odd_kernel(x_ref, g_ref, y_ref, gain_ref, w_ref, o_ref, yn_ref):
    @pl.when(pl.program_id(1) == 0)
    def _():
        y = y_ref[...]
        ms = jnp.mean(y * y, axis=-1, keepdims=True)
        yn_ref[...] = (y * lax.rsqrt(ms + RMS_EPS) * gain_ref[...]).astype(BF16)

    o_ref[...] = x_ref[...] + g_ref[...] * _dot(yn_ref[...], w_ref[...])


def out_projection_odd(x, g, y, gain, w, layer, mod_rows):
    m, d = x.shape
    k = y.shape[1]
    grp = lambda i, j: (_row_group(i, TM, mod_rows), 0, j)
    return pl.pallas_call(
        _outproj_odd_kernel,
        grid=(m // TM, d // OUT_TN),
        in_specs=[
            pl.BlockSpec((TM, OUT_TN), lambda i, j: (i, j)),
            pl.BlockSpec((None, 1, OUT_TN), grp),
            pl.BlockSpec((TM, k), lambda i, j: (i, 0)),
            pl.BlockSpec((1, k), lambda i, j: (0, 0)),
            pl.BlockSpec((None, k, OUT_TN), lambda i, j: (layer, 0, j)),
        ],
        out_specs=pl.BlockSpec((TM, OUT_TN), lambda i, j: (i, j)),
        out_shape=jax.ShapeDtypeStruct((m, d), F32),
        scratch_shapes=[pltpu.VMEM((TM, k), BF16)],
        compiler_params=_cparams(("parallel", "arbitrary")),
        name="out_projection_odd",
    )(x, g, y, gain, w)


def _lane_iota(shape):
    return lax.broadcasted_iota(jnp.int32, shape, len(shape) - 1)


def _half_rms_norm(x, gain):
    lo = _lane_iota(x.shape) < DB
    x2 = x * x
    s0 = jnp.sum(jnp.where(lo, x2, 0.0), axis=-1, keepdims=True)
    s1 = jnp.sum(jnp.where(lo, 0.0, x2), axis=-1, keepdims=True)
    r = jnp.where(lo, lax.rsqrt(s0 * (1.0 / DB) + RMS_EPS), lax.rsqrt(s1 * (1.0 / DB) + RMS_EPS))
    return x * r * gain


def _rope(x, cos, sin_signed):
    first = (_lane_iota(x.shape) % 32) < ROPE_PAIRS
    partner = jnp.where(first, pltpu.roll(x, LANES - ROPE_PAIRS, 1), pltpu.roll(x, ROPE_PAIRS, 1))
    return x * cos + partner * sin_signed


def _diff_attn_core(q, kk, vv, lam, dgain, lam_init):
    lo = _lane_iota(q.shape) < DB
    qs = q * (DB ** -0.5)
    nt = (((1,), (1,)), ((), ()))
    s = [lax.dot_general(jnp.where(lo, qs, 0.0).astype(BF16), kk, nt, preferred_element_type=F32),
         lax.dot_general(jnp.where(lo, 0.0, qs).astype(BF16), kk, nt, preferred_element_type=F32)]
    p = [jnp.exp(si - jnp.max(si, axis=-1, keepdims=True)) for si in s]
    pv = [_dot(pi.astype(BF16), vv) * (1.0 / jnp.sum(pi, axis=-1, keepdims=True)) for pi in p]
    o = pv[0] - lam * pv[1]
    ms = jnp.mean(o * o, axis=-1, keepdims=True)
    return o * lax.rsqrt(ms + RMS_EPS) * dgain * (1.0 - lam_init)


def _lambda_value(lam_ref, lam_init):
    lp = lam_ref[...]
    a = jnp.sum(lp[0:1] * lp[1:2], axis=-1, keepdims=True)
    b = jnp.sum(lp[2:3] * lp[3:4], axis=-1, keepdims=True)
    return jnp.exp(a) - jnp.exp(b) + lam_init


def _attn_ctx_kernel(lam_ref, q_ref, k_ref, v_ref, qg_ref, kg_ref, dg_ref, *rest, lam_init, plane):
    o_ref, kd_ref, vd_ref = rest[-3:]
    for p in range(kd_ref.shape[0]):
        if p != plane:
            kd_ref[p] = jnp.zeros(kd_ref.shape[1:], F32)
            vd_ref[p] = jnp.zeros(vd_ref.shape[1:], F32)
    lam = _lambda_value(lam_ref, lam_init)
    nt = (((1,), (1,)), ((), ()))

    def head(h):
        cs = slice(h * LANES, (h + 1) * LANES)
        kd = _half_rms_norm(k_ref[h], kg_ref[...])
        kd_ref[plane, :, cs] = kd
        v = v_ref[h]
        vd_ref[plane, :, cs] = v
        q = _half_rms_norm(q_ref[h], qg_ref[...]) * (DB ** -0.5)
        yield
        lo = _lane_iota(q.shape) < DB
        kk, vv = kd.astype(BF16), v.astype(BF16)
        s = [lax.dot_general(jnp.where(lo, q, 0.0).astype(BF16), kk, nt, preferred_element_type=F32),
             lax.dot_general(jnp.where(lo, 0.0, q).astype(BF16), kk, nt, preferred_element_type=F32)]
        yield
        p = [jnp.exp(si - jnp.max(si, axis=-1, keepdims=True)) for si in s]
        yield
        pv = [_dot(pi.astype(BF16), vv) * (1.0 / jnp.sum(pi, axis=-1, keepdims=True)) for pi in p]
        yield
        o = pv[0] - lam * pv[1]
        ms = jnp.mean(o * o, axis=-1, keepdims=True)
        o_ref[:, cs] = (o * lax.rsqrt(ms + RMS_EPS) * dg_ref[...] * (1.0 - lam_init)).astype(BF16)
        yield

    for h0 in range(0, H_B, ATT_CTX_LOCKSTEP):
        for _ in zip(*[head(h) for h in range(h0, h0 + ATT_CTX_LOCKSTEP)]):
            pass


def diff_attention_ctx(proj, lam_p, qg, kg, dg, layer, n_layers, prev_kv, nb, seq, lam_init):
    m = nb * seq
    vec = pl.BlockSpec((1, LANES), lambda b: (0, 0))
    slab = lambda base: pl.BlockSpec((H_B, seq, LANES), lambda b: (base // H_B, b, 0))
    outb = pl.BlockSpec((seq, H_B * LANES), lambda b: (b, 0))
    prev = () if prev_kv is None else tuple(prev_kv)
    n_planes, plane = (1, 0) if prev else (n_layers, layer)
    kvb = pl.BlockSpec((None, n_planes, seq, H_B * LANES), lambda b: (b, layer if prev else 0, 0, 0))
    kv_shape = jax.ShapeDtypeStruct((nb, n_layers, seq, H_B * LANES), F32)
    return pl.pallas_call(
        functools.partial(_attn_ctx_kernel, lam_init=lam_init, plane=plane),
        grid=(nb,),
        in_specs=[pl.BlockSpec((4, DB), lambda b: (0, 0)), slab(EB_BQ), slab(EB_BK), slab(EB_BV), vec, vec, vec]
        + [pl.BlockSpec(memory_space=pl.ANY)] * len(prev),
        out_specs=[outb, kvb, kvb],
        out_shape=[jax.ShapeDtypeStruct((m, B_V), BF16), kv_shape, kv_shape],
        input_output_aliases={7: 1, 8: 2} if prev else {},
        compiler_params=_cparams(("parallel",)),
        name="diff_attention_ctx",
    )(lam_p, proj, proj, proj, qg, kg, dg, *prev)


def _attn_dec_kernel(lam_ref, q_ref, k_ref, v_ref, kc_ref, vc_ref, qg_ref, kg_ref, dg_ref, cq_ref, sq_ref, ck_ref,
                     sk_ref, o_ref, ks_ref, vs_ref, *, lam_init, past):
    @pl.when(pl.program_id(2) == 0)
    def _():
        ks_ref[0:past, :] = kc_ref[...].astype(BF16)
        vs_ref[0:past, :] = vc_ref[...].astype(BF16)
        kd = _rope(_half_rms_norm(k_ref[...], kg_ref[...]), ck_ref[...], sk_ref[...])
        ks_ref[past:, :] = kd.astype(BF16)
        vs_ref[past:, :] = v_ref[...].astype(BF16)

    q = _rope(_half_rms_norm(q_ref[...], qg_ref[...]), cq_ref[...], sq_ref[...])
    lam = _lambda_value(lam_ref, lam_init)
    o_ref[...] = _diff_attn_core(q, ks_ref[...], vs_ref[...], lam, dg_ref[...], lam_init).astype(BF16)


def diff_attention_dec(proj, cache_k, cache_v, lam_p, qg, kg, dg, cos, sin, row0, nb, seq, lam_init):
    past = cache_k.shape[1]
    nq = seq // ATT_TQ
    rq, rk = row0 // ATT_TQ, row0 // seq
    vec = pl.BlockSpec((1, LANES), lambda b, h, i: (0, 0))
    kslab = lambda base: pl.BlockSpec((None, seq, LANES), lambda b, h, i: (base + h, rk + b, 0))
    cslab = pl.BlockSpec((None, past, LANES), lambda b, h, i: (b, 0, h))
    tq = pl.BlockSpec((ATT_TQ, LANES), lambda b, h, i: (i, 0))
    tk = pl.BlockSpec((seq, LANES), lambda b, h, i: (0, 0))
    return pl.pallas_call(
        functools.partial(_attn_dec_kernel, lam_init=lam_init, past=past),
        grid=(nb, H_B, nq),
        in_specs=[pl.BlockSpec((4, DB), lambda b, h, i: (0, 0)),
                  pl.BlockSpec((None, ATT_TQ, LANES), lambda b, h, i: (EB_BQ + h, rq + b * nq + i, 0)),
                  kslab(EB_BK), kslab(EB_BV), cslab, cslab, vec, vec, vec, tq, tq, tk, tk],
        out_specs=pl.BlockSpec((ATT_TQ, LANES), lambda b, h, i: (b * nq + i, h)),
        out_shape=jax.ShapeDtypeStruct((nb * seq, B_V), BF16),
        scratch_shapes=[pltpu.VMEM((past + seq, LANES), BF16), pltpu.VMEM((past + seq, LANES), BF16)],
        compiler_params=_cparams(("parallel", "parallel", "arbitrary")),
        name="diff_attention_dec",
    )(lam_p, proj, proj, proj, cache_k, cache_v, qg, kg, dg, cos, sin, cos, sin)


def rope_tables(n_tokens):
    n_rows = n_tokens // GRID_W
    rows, cols = jnp.meshgrid(jnp.arange(n_rows, dtype=F32), jnp.arange(GRID_W, dtype=F32), indexing='ij')
    inv_freq = ROPE_THETA ** (-jnp.arange(ROPE_PAIRS, dtype=F32) / ROPE_PAIRS)
    ang = jnp.stack([rows.reshape(-1, 1) * inv_freq, cols.reshape(-1, 1) * inv_freq], axis=0)
    cos, sin = jnp.cos(ang), jnp.sin(ang)
    cos_half = jnp.concatenate([cos[0], cos[0], cos[1], cos[1]], axis=-1)
    sin_half = jnp.concatenate([-sin[0], sin[0], -sin[1], sin[1]], axis=-1)
    return jnp.tile(cos_half, (1, 2)), jnp.tile(sin_half, (1, 2))


def _tri(n, lower):
    r = lax.broadcasted_iota(jnp.int32, (n, n), 0)
    c = lax.broadcasted_iota(jnp.int32, (n, n), 1)
    return jnp.where((r >= c) if lower else (r <= c), 1.0, 0.0).astype(BF16)


def _mlstm_stage_a(d, rows, slot, q_ref, k_ref, v_ref, gcol, grow, tri_l, tri_u, ia_ref, ba_ref, mi_ref, ua_ref,
                   sc_ref):
    lc = MLSTM_CHUNK
    ri = lax.broadcasted_iota(jnp.int32, (lc, lc), 0)
    ci = lax.broadcasted_iota(jnp.int32, (lc, lc), 1)
    mask = (ci <= ri) if d == 0 else (ci >= ri)
    lo = _lane_iota((lc, LANES)) < DK_A
    nt = (((1,), (1,)), ((), ()))
    bcol_all = _dot3_r(tri_l if d == 0 else tri_u, jax.nn.log_sigmoid(gcol))
    brow_all = _dot3_l(jax.nn.log_sigmoid(grow), tri_u if d == 0 else tri_l)
    yield
    q = q_ref[rows, :]
    k = k_ref[rows, :] * (DK_A ** -0.5)
    kb = k.astype(BF16)
    ones = jnp.ones((lc, LANES), BF16)
    vs = [v_ref[j, rows, :].astype(BF16) for j in range(2)]
    heads = (0, 1)
    qk = [lax.dot_general((jnp.where(lo, q, 0.0) if j == 0 else jnp.where(lo, 0.0, q)).astype(BF16), kb, nt,
                          preferred_element_type=F32) for j in heads]
    b_all = [jnp.broadcast_to(bcol_all[:, 16 + 8 * d + j:16 + 8 * d + j + 1], (lc, LANES)) for j in heads]
    li_all = [jnp.broadcast_to(gcol[:, 8 * d + j:8 * d + j + 1], (lc, LANES)) for j in heads]
    yield
    dmat = [jnp.where(mask, b_all[j][:, :lc] - brow_all[16 + 8 * d + j:16 + 8 * d + j + 1, :]
                      + grow[8 * d + j:8 * d + j + 1, :], -jnp.inf) for j in heads]
    m_intra = [jnp.max(dmat[j], axis=-1, keepdims=True) for j in heads]
    b_end = [b_all[j][lc - 1:lc, :] if d == 0 else b_all[j][0:1, :] for j in heads]
    g_all = [b_end[j] - b_all[j] + li_all[j] for j in heads]
    g_max = [jnp.max(g_all[j], axis=0, keepdims=True) for j in heads]
    yield
    s = [(qk[j] * jnp.exp(dmat[j] - m_intra[j])).astype(BF16) for j in heads]
    wkb = (k * jnp.where(lo, jnp.exp(g_all[0] - g_max[0]), jnp.exp(g_all[1] - g_max[1]))).astype(BF16)
    yield
    for j in heads:
        ia_ref[slot, d, j] = _dot(s[j], jnp.concatenate([vs[j], ones], axis=1))
        ba_ref[slot, d, j] = b_all[j]
        mi_ref[slot, d, j] = jnp.broadcast_to(m_intra[j], (lc, LANES))
        sc_ref[slot, d, j, 0:1, :] = g_max[j]
        sc_ref[slot, d, j, 1:2, :] = b_end[j]
    tn = (((0,), (0,)), ((), ()))
    ua_ref[slot, d] = lax.dot_general(wkb, jnp.concatenate(vs + [ones], axis=1), tn, preferred_element_type=F32)
    yield


def _mlstm_stage_b(d, rows, slot, q_ref, m_prev, cn_ref, h_out, m_out, ia_ref, ba_ref, mi_ref, ua_ref, sc_ref):
    lc = MLSTM_CHUNK
    lo = _lane_iota((lc, LANES)) < DK_A
    heads = (0, 1)
    q = q_ref[rows, :]
    cn = cn_ref[d]
    cnb = cn.astype(BF16)
    inter_nd = [_dot((jnp.where(lo, q, 0.0) if j == 0 else jnp.where(lo, 0.0, q)).astype(BF16), cnb) for j in heads]
    yield
    inter = [ba_ref[slot, d, j] + m_prev[j] for j in heads]
    m_row = [jnp.maximum(inter[j], mi_ref[slot, d, j]) for j in heads]
    w_inter = [jnp.exp(inter[j] - m_row[j]) for j in heads]
    w_intra = [jnp.exp(mi_ref[slot, d, j] - m_row[j]) for j in heads]
    g_max = [sc_ref[slot, d, j, 0:1, :] for j in heads]
    b_end = [sc_ref[slot, d, j, 1:2, :] for j in heads]
    m_new = [jnp.maximum(b_end[j] + m_prev[j], g_max[j]) for j in heads]
    decay = [jnp.exp(b_end[j] + m_prev[j] - m_new[j]) for j in heads]
    fac = [jnp.exp(g_max[j] - m_new[j]) for j in heads]
    m_out.extend(m_new)
    yield
    u = ua_ref[slot, d]
    top = lax.broadcasted_iota(jnp.int32, (LANES, LANES), 0) < DK_A
    dec = jnp.where(top, decay[0], decay[1])
    f = jnp.where(top, fac[0], fac[1])
    c_new = dec * cn[:, :LANES] + f * jnp.where(top, u[:, :LANES], u[:, LANES:2 * LANES])
    n_new = dec * cn[:, LANES:] + f * u[:, 2 * LANES:]
    cn_ref[d] = jnp.concatenate([c_new, n_new], axis=1)
    yield
    hs = []
    for j in heads:
        intra_nd = ia_ref[slot, d, j]
        num = w_inter[j] * inter_nd[j][:, :LANES] + w_intra[j] * intra_nd[:, :LANES]
        den = w_inter[j] * inter_nd[j][:, LANES:] + w_intra[j] * intra_nd[:, LANES:]
        hs.append(num * (1.0 / jnp.maximum(jnp.abs(den), jnp.exp(-m_row[j]))))
    h_out[rows, :] = jnp.concatenate(hs, axis=-1)
    yield


def _mlstm_kernel(q_ref, k_ref, v_ref, ao_ref, g_ref, bias_ref, gain_ref, c0_ref, n0_ref, m0_ref,
                  h_ref, cf_ref, nf_ref, mf_ref, gc_ref, gt_ref, hf_ref, hb_ref, cn_ref, ia_ref, ba_ref, mi_ref,
                  ua_ref, sc_ref, *, seq):
    lc = MLSTM_CHUNK
    nc = seq // lc
    pair = pl.program_id(1)
    shift = (LANES - 2 * pair) % LANES
    bias = pltpu.roll(jnp.broadcast_to(bias_ref[...], (8, LANES)), shift, 1)[0:1]
    for blk in range(seq // LANES):
        r = slice(blk * LANES, (blk + 1) * LANES)
        gs = pltpu.roll(g_ref[r, :], shift, 1) + bias
        gc_ref[r, :] = gs
        gst = gs.T
        for half in range(LANES // lc):
            gt_ref[blk * (LANES // lc) + half] = gst[0:32, half * lc:(half + 1) * lc]
    for d in range(2):
        cn_ref[d, :, 0:LANES] = c0_ref[d]
        cn_ref[d, :, LANES:] = jnp.broadcast_to(n0_ref[d], (LANES, LANES)).T
    tri_l, tri_u = _tri(lc, True), _tri(lc, False)
    stage = (ia_ref, ba_ref, mi_ref, ua_ref, sc_ref)

    def chunk_rows(c):
        return pl.ds(pl.multiple_of(c * lc, lc), lc)

    def stage_a(i, slot):
        gens = []
        for d, c in ((0, i), (1, nc - 1 - i)):
            rows = chunk_rows(c)
            gens.append(_mlstm_stage_a(d, rows, slot, q_ref, k_ref, v_ref, gc_ref[rows, :], gt_ref[c], tri_l, tri_u,
                                       *stage))
        return gens

    def stage_b(i, slot, ms, outs):
        return [_mlstm_stage_b(d, chunk_rows(c), slot, q_ref, ms[2 * d:2 * d + 2], cn_ref, h_out, outs[d], *stage)
                for d, c, h_out in ((0, i, hf_ref), (1, nc - 1 - i, hb_ref))]

    def run(gens):
        for _ in itertools.zip_longest(*gens):
            pass

    def body(kk, ms):
        outs = ([], [])
        run(stage_b(2 * kk, 0, ms, outs) + stage_a(2 * kk + 1, 1))
        ms = tuple(outs[0] + outs[1])
        outs = ([], [])
        run(stage_b(2 * kk + 1, 1, ms, outs) + stage_a(jnp.minimum(2 * kk + 2, nc - 1), 0))
        return tuple(outs[0] + outs[1])

    m0 = m0_ref[...]
    rep = lambda v: jnp.broadcast_to(v, (1, LANES))
    init = (rep(m0[0][:, 0:1]), rep(m0[0][:, DK_A:DK_A + 1]), rep(m0[1][:, 0:1]), rep(m0[1][:, DK_A:DK_A + 1]))
    run(stage_a(0, 0))
    m_f0, m_f1, m_b0, m_b1 = lax.fori_loop(0, nc // 2, body, init)
    lane1 = _lane_iota((1, LANES)) < DK_A
    mf_ref[0] = jnp.where(lane1, m_f0, m_f1)
    mf_ref[1] = jnp.where(lane1, m_b0, m_b1)
    for d in range(2):
        cf_ref[d] = cn_ref[d, :, 0:LANES]
        nf_ref[d] = cn_ref[d, :, LANES:].T[0:1, :]

    gain = gain_ref[...]
    for blk in range(seq // LANES):
        r = slice(blk * LANES, (blk + 1) * LANES)
        outs = []
        for j in range(2):
            cs = slice(j * DV_A, (j + 1) * DV_A)
            hh = hf_ref[r, cs] + hb_ref[r, cs]
            ms = jnp.mean(hh * hh, axis=-1, keepdims=True)
            outs.append(hh * lax.rsqrt(ms + RMS_EPS) * gain[:, cs] * jax.nn.sigmoid(ao_ref[j, r, :]))
        h_ref[r, :] = jnp.concatenate(outs, axis=-1).astype(BF16)


def mlstm(proj, bias, gain, c0, n0, m0, row0, nb, seq):
    rb = row0 // seq
    npair = H_A // 2
    slab = lambda base: pl.BlockSpec((None, seq, LANES), lambda b, p: (base + p, rb + b, 0))
    slab2 = lambda base: pl.BlockSpec((2, seq, LANES), lambda b, p: (base // 2 + p, rb + b, 0))
    st_c = pl.BlockSpec((None, None, 2, LANES, LANES), lambda b, p: (b, p, 0, 0, 0))
    st_v = pl.BlockSpec((None, None, 2, 1, LANES), lambda b, p: (b, p, 0, 0, 0))
    return pl.pallas_call(
        functools.partial(_mlstm_kernel, seq=seq),
        grid=(nb, npair),
        in_specs=[slab(EB_Q), slab(EB_K), slab2(EB_V), slab2(EB_O),
                  pl.BlockSpec((None, seq, LANES), lambda b, p: (EB_G, rb + b, 0)),
                  pl.BlockSpec((1, LANES), lambda b, p: (0, 0)),
                  pl.BlockSpec((1, 2 * DV_A), lambda b, p: (0, p)),
                  st_c, st_v, st_v],
        out_specs=[pl.BlockSpec((seq, 2 * DV_A), lambda b, p: (b, p)), st_c, st_v, st_v],
        out_shape=[jax.ShapeDtypeStruct((nb * seq, A_V), BF16),
                   jax.ShapeDtypeStruct((nb, npair, 2, LANES, LANES), F32),
                   jax.ShapeDtypeStruct((nb, npair, 2, 1, LANES), F32),
                   jax.ShapeDtypeStruct((nb, npair, 2, 1, LANES), F32)],
        scratch_shapes=[pltpu.VMEM((seq, LANES), F32), pltpu.VMEM((seq // MLSTM_CHUNK, 32, MLSTM_CHUNK), F32),
                        pltpu.VMEM((seq, 2 * DV_A), F32),
                        pltpu.VMEM((seq, 2 * DV_A), F32), pltpu.VMEM((2, LANES, 2 * LANES), F32),
                        pltpu.VMEM((2, 2, 2, MLSTM_CHUNK, 2 * LANES), F32), pltpu.VMEM((2, 2, 2, MLSTM_CHUNK, LANES), F32),
                        pltpu.VMEM((2, 2, 2, MLSTM_CHUNK, LANES), F32), pltpu.VMEM((2, 2, LANES, 3 * LANES), F32),
                        pltpu.VMEM((2, 2, 2, 8, LANES), F32)],
        compiler_params=_cparams(("parallel", "parallel")),
        name="mlstm",
    )(proj, proj, proj, proj, proj, bias, gain, c0, n0, m0)


def _softplus(x):
    return jnp.maximum(x, 0.0) + jnp.log1p(jnp.exp(-jnp.abs(x)))


SSD_REP = 2 * R_C


def _pack3(x, lane):
    hi = x.astype(BF16).astype(F32)
    r1 = x - hi
    mid = r1.astype(BF16).astype(F32)
    r2 = r1 - mid
    grp = lane % (3 * SSD_REP)
    return jnp.where(grp < SSD_REP, x, jnp.where(grp < 2 * SSD_REP, r1, r2)).astype(BF16)


SSD_HALO = 8


def _ssd_kernel(xs_ref, bm_ref, cm_ref, z_ref, dtr_ref, wx_ref, wb_ref, wc_ref, bx_ref, bb_ref, bc_ref,
                 dtb_ref, alog_ref, dsk_ref, rep_ref, selw_ref, expf_ref, expb_ref, *rest, seq, has_init, has_prev,
                 plane):
    if has_init:
        s0_ref = rest[0]
        rest = rest[1:]
    if has_prev:
        rest = rest[1:]
    (y_ref, sfin_ref, pad_ref, xa_ref, ba_ref, ca_ref, dt_ref, af_ref, ab_ref, st_ref, ex_ref, abw_ref,
     at_ref, cb_ref) = rest
    lc = SSD_CHUNK
    nc = seq // lc
    gw = R_C * P_C
    nslab = gw // LANES
    lane = _lane_iota((lc, LANES))
    lo = lane < P_C
    fwd_lane = (lane % SSD_REP) < R_C
    first3 = lane < 3 * SSD_REP
    a_neg = -jnp.exp(alog_ref[...])
    tri_l, tri_u = _tri(lc, True), _tri(lc, False)
    ri = lax.broadcasted_iota(jnp.int32, (lc, lc), 0)
    ci = lax.broadcasted_iota(jnp.int32, (lc, lc), 1)
    causal, anti = ci <= ri, ci >= ri
    nt = (((1,), (1,)), ((), ()))

    def chunk_rows(c):
        return pl.ds(pl.multiple_of(c * lc, lc), lc)

    halo = jnp.zeros((SSD_HALO, LANES), F32)
    for s in range(nslab + 2):
        pad_ref[s, 0:SSD_HALO, :] = halo
        pad_ref[s, SSD_HALO + seq:, :] = halo
        src = xs_ref[s] if s < nslab else (bm_ref[...] if s == nslab else cm_ref[...])
        pad_ref[s, SSD_HALO:SSD_HALO + seq, :] = src

    def conv_chunk(s, base, w, b):
        y = w[0:1] * pad_ref[s, pl.ds(base + SSD_HALO - 2, lc), :]
        y = y + w[1:2] * pad_ref[s, pl.ds(base + SSD_HALO - 1, lc), :]
        y = y + w[2:3] * pad_ref[s, pl.ds(base + SSD_HALO, lc), :]
        y = y + w[3:4] * pad_ref[s, pl.ds(base + SSD_HALO + 1, lc), :]
        return _silu(y + b)

    def prep_body(c, carry):
        base = pl.multiple_of(c * lc, lc)
        rows = pl.ds(base, lc)
        dt_pre = _dot3_l(dtr_ref[rows, :], rep_ref[...])
        for s in range(nslab // 2):
            cs = slice(s * LANES, (s + 1) * LANES)
            xa_ref[rows, cs] = conv_chunk(s, base, wx_ref[:, cs], bx_ref[:, cs])
        dt = _softplus(dt_pre + dtb_ref[...])
        dt_ref[rows, :] = dt
        da = dt * a_neg
        acum_f, acum_b = _dot3_r(tri_l, da), _dot3_r(tri_u, da)
        for s in range(nslab // 2, nslab):
            cs = slice(s * LANES, (s + 1) * LANES)
            xa_ref[rows, cs] = conv_chunk(s, base, wx_ref[:, cs], bx_ref[:, cs])
        ba_ref[rows, :] = conv_chunk(nslab, base, wb_ref[...], bb_ref[...])
        ca_ref[rows, :] = conv_chunk(nslab + 1, base, wc_ref[...], bc_ref[...])
        af_ref[rows, :] = acum_f
        ab_ref[rows, :] = acum_b
        return carry

    lax.fori_loop(0, nc, prep_body, 0, unroll=2)

    for d in range(2):
        for k in range(nslab):
            cs = slice(k * LANES, (k + 1) * LANES)
            st_ref[d, :, cs] = s0_ref[d, cs, :].T if has_init else jnp.zeros((N_C, LANES), F32)

    def expand(ea, te, e_ref):
        return _dot(_pack3(jnp.where(first3, ea, te), lane), e_ref[...])

    def fwd_a(c, slot):
        rows = chunk_rows(c)
        acum = af_ref[rows, :]
        a_end = acum[lc - 1:lc, :]
        ea, te = jnp.exp(acum), jnp.exp(a_end - acum) * dt_ref[rows, :]
        yield
        ex_ref[slot] = expand(ea, te, expf_ref)
        yield

    def fwd_b(c, slot):
        rows = chunk_rows(c)
        ea_x, te_x = ex_ref[slot, :, 0:gw], ex_ref[slot, :, gw:2 * gw]
        st = st_ref[0]
        cs_ = _dot(ca_ref[rows, :].astype(BF16), st.astype(BF16))
        bc_t = ba_ref[rows, :].T.astype(BF16)
        yield
        upd = _dot(bc_t, (xa_ref[rows, :] * te_x).astype(BF16))
        yield
        y_ref[rows, :] = cs_ * ea_x
        st_ref[0] = st * ea_x[lc - 1:lc, :] + upd
        yield

    def bwd_a(c, slot):
        rows = chunk_rows(c)
        dt = dt_ref[rows, :]
        acum_b = ab_ref[rows, :]
        acum = jnp.where(fwd_lane, af_ref[rows, :], acum_b)
        at_ref[slot, 0] = acum.T
        at_ref[slot, 1] = dt.T
        yield
        abw_ref[slot] = _dot(_pack3(acum, lane), selw_ref[...])
        yield
        a_end = acum_b[0:1, :]
        ex_ref[slot] = expand(jnp.exp(acum_b), jnp.exp(a_end - acum_b) * dt, expb_ref)
        yield
        cb_ref[slot] = lax.dot_general(ca_ref[rows, :].astype(BF16), ba_ref[rows, :].astype(BF16), nt,
                                       preferred_element_type=F32)
        yield

    def bwd_b(c, slot):
        rows = chunk_rows(c)
        ea_x, te_x = ex_ref[slot, :, 0:gw], ex_ref[slot, :, gw:2 * gw]
        cb = cb_ref[slot]
        st = st_ref[1]
        xs = xa_ref[rows, :]
        y_inter = _dot(ca_ref[rows, :].astype(BF16), st.astype(BF16)) * ea_x
        bc_t = ba_ref[rows, :].T.astype(BF16)
        upd = _dot(bc_t, (xs * te_x).astype(BF16))
        yield
        for s in range(nslab):
            cs = slice(s * LANES, (s + 1) * LANES)
            ws = []
            for j in range(2):
                r = 2 * s + j
                rb = R_C + r
                aj_f, aj_b = at_ref[slot, 0, r:r + 1, :], at_ref[slot, 0, rb:rb + 1, :]
                dt_f, dt_b = at_ref[slot, 1, r:r + 1, :], at_ref[slot, 1, rb:rb + 1, :]
                wf = jnp.exp(jnp.where(causal, abw_ref[slot, :, r * LANES:(r + 1) * LANES] - aj_f, -jnp.inf))
                wb = jnp.exp(jnp.where(anti, abw_ref[slot, :, rb * LANES:(rb + 1) * LANES] - aj_b, -jnp.inf))
                ws.append((cb * (wf * dt_f + wb * dt_b)).astype(BF16))
            xsl = xs[:, cs]
            x2 = jnp.concatenate([jnp.where(lo, xsl, 0.0), jnp.where(lo, 0.0, xsl)], axis=0).astype(BF16)
            y = y_ref[rows, cs] + y_inter[:, cs] + dsk_ref[:, cs] * xsl + _dot(jnp.concatenate(ws, axis=1), x2)
            y_ref[rows, cs] = y * _silu(z_ref[s, rows, :])
            yield
        st_ref[1] = st * ea_x[0:1, :] + upd
        yield

    def run(*gens):
        for _ in itertools.zip_longest(*gens):
            pass

    def sweep(stage_a, stage_b, order):
        run(stage_a(order(0), 0))

        def body(k, carry):
            run(stage_b(order(2 * k), 0), stage_a(order(2 * k + 1), 1))
            run(stage_b(order(2 * k + 1), 1), stage_a(order(jnp.minimum(2 * k + 2, nc - 1)), 0))
            return carry

        lax.fori_loop(0, nc // 2, body, 0)

    sweep(fwd_a, fwd_b, lambda i: i)
    sweep(bwd_a, bwd_b, lambda i: nc - 1 - i)

    for d in range(2):
        for k in range(nslab):
            cs = slice(k * LANES, (k + 1) * LANES)
            sfin_ref[plane, d, cs, :] = st_ref[d, :, cs].T
    for p in range(sfin_ref.shape[0]):
        if p != plane:
            sfin_ref[p] = jnp.zeros(sfin_ref.shape[1:], F32)


def ssd(proj, conv_w, conv_b, dtb, alog, dsk, s0, layer, n_layers, out_plane, prev_states, nb, seq):
    gw = R_C * P_C
    nslab = gw // LANES
    slab = lambda base: pl.BlockSpec((None, seq, LANES), lambda b, g: (base + g, b, 0))
    slab4 = lambda base: pl.BlockSpec((nslab, seq, LANES), lambda b, g: (base // nslab + g, b, 0))
    xo, bo, co = 0, D_INNER // LANES, (D_INNER + G_C * N_C) // LANES
    cw = lambda rws, width, off: pl.BlockSpec((rws, width), lambda b, g: (0, off * LANES // width + g))
    vec = pl.BlockSpec((None, 1, LANES), lambda b, g: (g, 0, 0))
    full = lambda a: pl.BlockSpec(a.shape, lambda b, g: (0,) * a.ndim)
    st_in = pl.BlockSpec((None, None, 2, gw, N_C), lambda b, g: (b, layer, 0, g, 0))
    rep, selw, expf, expb = _ssd_constants()
    has_init, has_prev = s0 is not None, prev_states is not None
    n_planes, plane = (1, 0) if has_prev else (n_layers, out_plane)
    st_out = pl.BlockSpec((None, n_planes, 2, gw, N_C), lambda b, g: (b, out_plane if has_prev else 0, 0, g, 0))
    st_shape = jax.ShapeDtypeStruct((nb, n_layers, 2, H_C * P_C, N_C), F32)
    vm = lambda *shape: pltpu.VMEM(shape, F32)
    n_in = 18 + has_init
    return pl.pallas_call(
        functools.partial(_ssd_kernel, seq=seq, has_init=has_init, has_prev=has_prev, plane=plane),
        grid=(nb, G_C),
        in_specs=[slab4(OB_X), slab(OB_B), slab(OB_C), slab4(OB_Z),
                  pl.BlockSpec((None, seq, LANES), lambda b, g: (OB_DT, b, 0)),
                  cw(SSD_CONV, gw, xo), cw(SSD_CONV, LANES, bo), cw(SSD_CONV, LANES, co),
                  cw(1, gw, xo), cw(1, LANES, bo), cw(1, LANES, co),
                  vec, vec, pl.BlockSpec((1, gw), lambda b, g: (0, g)),
                  pl.BlockSpec((None, LANES, LANES), lambda b, g: (g, 0, 0)), full(selw), full(expf), full(expb)]
        + ([st_in] if has_init else []) + ([pl.BlockSpec(memory_space=pl.ANY)] if has_prev else []),
        out_specs=[pl.BlockSpec((seq, gw), lambda b, g: (b, g)), st_out],
        out_shape=[jax.ShapeDtypeStruct((nb * seq, D_INNER), F32), st_shape],
        input_output_aliases={n_in: 1} if has_prev else {},
        scratch_shapes=[vm(nslab + 2, seq + 2 * SSD_HALO, LANES), vm(seq, gw), vm(seq, LANES), vm(seq, LANES),
                        vm(seq, LANES), vm(seq, LANES), vm(seq, LANES), vm(2, N_C, gw), vm(2, SSD_CHUNK, 2 * gw),
                        vm(2, SSD_CHUNK, SSD_REP * LANES), vm(2, 2, LANES, SSD_CHUNK), vm(2, SSD_CHUNK, SSD_CHUNK)],
        compiler_params=_cparams(("parallel", "parallel")),
        name="ssd",
    )(proj, proj, proj, proj, proj, conv_w, conv_w, conv_w, conv_b, conv_b, conv_b, dtb, alog, dsk,
      rep, selw, expf, expb, *((s0,) if has_init else ()), *((prev_states,) if has_prev else ()))


def _ssd_constants():
    gw = R_C * P_C
    rep = np.zeros((G_C, LANES, LANES), np.float32)
    for g in range(G_C):
        for l in range(LANES):
            c = l % SSD_REP
            rep[g, (R_C * g + c) if c < R_C else (H_C + R_C * g + c - R_C), l] = 1.0
    selw = np.zeros((LANES, SSD_REP * LANES), np.float32)
    expf = np.zeros((LANES, 2 * gw), np.float32)
    expb = np.zeros((LANES, 2 * gw), np.float32)
    for l in range(3 * SSD_REP):
        c = l % SSD_REP
        selw[l, c * LANES:(c + 1) * LANES] = 1.0
    for l in range(6 * SSD_REP):
        c, q = l % SSD_REP, l // (3 * SSD_REP)
        if c < R_C:
            expf[l, q * gw + c * P_C:q * gw + (c + 1) * P_C] = 1.0
        else:
            expb[l, q * gw + (c - R_C) * P_C:q * gw + (c - R_C + 1) * P_C] = 1.0
    return tuple(jnp.asarray(a, BF16) for a in (rep, selw, expf, expb))


def _pair_layout_c(c):
    nb = c.shape[0]
    c = c.transpose(0, 2, 1, 4, 3).reshape(nb, H_A // 2, 2, 2, DK_A, DV_A)
    return c.transpose(0, 1, 3, 2, 4, 5).reshape(nb, H_A // 2, 2, 2 * DK_A, DV_A)


def _pair_layout_c_inv(c):
    nb = c.shape[0]
    c = c.reshape(nb, H_A // 2, 2, 2, DK_A, DV_A).transpose(0, 2, 1, 3, 5, 4)
    return c.reshape(nb, 2, H_A, DV_A, DK_A)


def _pair_layout_n(n):
    nb = n.shape[0]
    return n.reshape(nb, 2, H_A // 2, 2 * DK_A).transpose(0, 2, 1, 3).reshape(nb, H_A // 2, 2, 1, 2 * DK_A)


def _pair_layout_n_inv(n):
    nb = n.shape[0]
    return n.reshape(nb, H_A // 2, 2, 2 * DK_A).transpose(0, 2, 1, 3).reshape(nb, 2, H_A, DK_A)


def _pair_layout_m(m):
    return _pair_layout_n(jnp.repeat(m[..., None], DK_A, axis=-1))


def _pair_layout_m_inv(m):
    return _pair_layout_n_inv(m)[..., 0]


def _group_replicated(v):
    t = v.reshape(2, G_C, R_C).transpose(1, 0, 2).reshape(G_C, 1, SSD_REP)
    return jnp.tile(t, (1, 1, LANES // SSD_REP))


def kernel(x_prompt, x_sample, cache_attn_k, cache_attn_v, state_mlstm_c, state_mlstm_n, state_mlstm_m, state_ssd,
           c, c_ctx, w_ada, b_ada, norm_mix, norm_ffn, w_gate, w_up, w_down, w_in_even, b_gate_mlstm, mlstm_norm,
           q_norm, k_norm, lambda_q1, lambda_k1, lambda_q2, lambda_k2, diff_norm, w_out_even, w_in_odd, conv_w,
           conv_b, dt_bias, a_log, d_skip, ssd_norm, w_out_odd):
    nb_c, seq_c, d = x_prompt.shape
    nb_d, seq_d, _ = x_sample.shape
    n_ctx, n_dec = nb_c * seq_c, nb_d * seq_d
    depth = w_ada.shape[0]

    xc, xd = x_prompt.reshape(n_ctx, d), x_sample.reshape(n_dec, d)
    cvecs = jnp.concatenate([c_ctx[None, :], c, jnp.zeros((8 - 1 - nb_d, d), F32)], axis=0)
    mods = ada_modulation(cvecs, w_ada, b_ada)

    def mod(l, k):
        return mods[l, :, k * d:(k + 1) * d].reshape(8, 1, d)

    cos, sin = rope_tables(seq_d)
    n_even, n_odd = w_in_even.shape[0], w_in_odd.shape[0]
    gate0 = 2 * A_QK + 2 * A_V
    w_even = jnp.concatenate([w_in_even[:, :, :gate0], w_in_even[:, :, gate0 + 4 * H_A:],
                              w_in_even[:, :, gate0:gate0 + 4 * H_A]], axis=2)
    w_even = jnp.pad(w_even, ((0, 0), (0, 0), (0, E_COLS - w_even.shape[2]))).astype(BF16)
    w_odd = jnp.pad(w_in_odd, ((0, 0), (0, 0), (0, O_COLS - w_in_odd.shape[2]))).astype(BF16)
    wo_even, wo_odd = w_out_even.astype(BF16), w_out_odd.astype(BF16)
    ffn_w = (w_gate.astype(BF16), w_up.astype(BF16), w_down.astype(BF16))

    new_c, new_n, new_m = [], [], []
    new_kv, new_s = None, None
    for l in range(depth):
        i = l // 2
        sh1, sc1, g1, sh2, sc2, g2 = (mod(l, k) for k in range(6))
        if l % 2 == 0:
            gmix = norm_mix[l][None]
            pc = in_projection(xc, gmix, sc1, sh1, w_even, i, E_TN, None)
            pd = in_projection(xd, gmix, sc1, sh1, w_even, i, E_TN, seq_d)
            bias = jnp.pad(b_gate_mlstm[i][None], ((0, 0), (0, LANES - 4 * H_A)))
            gain_a = mlstm_norm[i][None]
            zc = jnp.zeros((nb_c, H_A // 2, 2, LANES, LANES), F32)
            zv = jnp.zeros((nb_c, H_A // 2, 2, 1, LANES), F32)
            h_c, cf, nf, mf = mlstm(pc, bias, gain_a, zc, zv, zv, 0, nb_c, seq_c)
            h_d, _, _, _ = mlstm(pd, bias, gain_a, _pair_layout_c(state_mlstm_c[:, i]),
                                 _pair_layout_n(state_mlstm_n[:, i]), _pair_layout_m(state_mlstm_m[:, i]),
                                 0, nb_d, seq_d)
            new_c.append(_pair_layout_c_inv(cf))
            new_n.append(_pair_layout_n_inv(nf[:, :, :, 0]))
            new_m.append(_pair_layout_m_inv(mf[:, :, :, 0]))

            lam_init = 0.8 - 0.6 * math.exp(-0.3 * l)
            lam_p = jnp.stack([lambda_q1[i], lambda_k1[i], lambda_q2[i], lambda_k2[i]], axis=0)
            qg = jnp.tile(q_norm[i][None], (1, 2))
            kg = jnp.tile(k_norm[i][None], (1, 2))
            dg = diff_norm[i][None]
            o_c, *new_kv = diff_attention_ctx(pc, lam_p, qg, kg, dg, i, n_even, new_kv, nb_c, seq_c, lam_init)
            past = cache_attn_k.shape[2]
            o_d = diff_attention_dec(pd, cache_attn_k[:, i].reshape(nb_d, past, B_QK),
                                     cache_attn_v[:, i].reshape(nb_d, past, B_V), lam_p, qg, kg, dg, cos, sin,
                                     0, nb_d, seq_d, lam_init)
            xc = out_projection_even(xc, g1, h_c, o_c, wo_even, i, None)
            xd = out_projection_even(xd, g1, h_d, o_d, wo_even, i, seq_d)
        else:
            gmix = norm_mix[l][None]
            pc = in_projection(xc, gmix, sc1, sh1, w_odd, i, O_TN, None)
            pd = in_projection(xd, gmix, sc1, sh1, w_odd, i, O_TN, seq_d)
            dtb, alog = _group_replicated(dt_bias[i]), _group_replicated(a_log[i])
            dsk = jnp.repeat(d_skip[i], P_C)[None]
            cw, cb = conv_w[i], conv_b[i][None]
            y_c, new_s = ssd(pc, cw, cb, dtb, alog, dsk, None, i, n_odd, i, new_s, nb_c, seq_c)
            y_d, _ = ssd(pd, cw, cb, dtb, alog, dsk, state_ssd.reshape(nb_d, n_odd, 2, H_C * P_C, N_C), i, 1, 0, None,
                         nb_d, seq_d)
            gs = ssd_norm[i][None]
            xc = out_projection_odd(xc, g1, y_c, gs, wo_odd, i, None)
            xd = out_projection_odd(xd, g1, y_d, gs, wo_odd, i, seq_d)
        xc = ffn(xc, norm_ffn[l][None], sc2, sh2, g2, *ffn_w, l, None)
        xd = ffn(xd, norm_ffn[l][None], sc2, sh2, g2, *ffn_w, l, seq_d)

    y_prompt = xc.reshape(nb_c, seq_c, d)
    y_sample = xd.reshape(nb_d, seq_d, d)
    new_k = new_kv[0].reshape(nb_c, n_even, seq_c, H_B, 2, DB)
    new_v = new_kv[1].reshape(nb_c, n_even, seq_c, H_B, DVB)
    return (y_prompt, y_sample, new_k, new_v, jnp.stack(new_c, axis=1), jnp.stack(new_n, axis=1),
            jnp.stack(new_m, axis=1), new_s.reshape(nb_c, n_odd, 2, H_C, P_C, N_C))
```

```python
import functools
import itertools
import math

import jax
import jax.numpy as jnp
import numpy as np
from jax import lax
from jax.experimental import pallas as pl
from jax.experimental.pallas import tpu as pltpu

F32 = jnp.float32
BF16 = jnp.bfloat16

D_MODEL = 2048
DEPTH = 4
GRID_W = 64
RMS_EPS = 1e-6
H_A, DK_A, DV_A, MLSTM_CHUNK = 8, 64, 128, 64
H_B, DB, DVB = 8, 64, 128
ROPE_THETA = 10000.0
ROPE_PAIRS = DB // 4
D_INNER = 2 * D_MODEL
P_C, N_C, G_C = 64, 128, 8
H_C = D_INNER // P_C
R_C = H_C // G_C
SSD_CONV = 4
SSD_CHUNK = 128
D_FF = -(-8 * D_MODEL // (3 * 256)) * 256
A_QK, A_V = H_A * DK_A, H_A * DV_A
B_QK, B_V = H_B * 2 * DB, H_B * DVB
CONV_CH = D_INNER + 2 * G_C * N_C

LANES = 128
VMEM_LIMIT = 56 * 1024 * 1024

TM = 512
INP_TM = 1024
FFN_TM = 512
ADA_TN = 1024
E_COLS = 6400
E_TN = 1280
O_COLS = 10752
O_TN = 1536
FF_TN = 512
OUT_TN = 1024
ATT_TQ = 256
ATT_CTX_LOCKSTEP = 4

EB_Q, EB_K, EB_V, EB_O = 0, 4, 8, 16
EB_BQ, EB_BK, EB_BV, EB_G = 24, 32, 40, 48
OB_Z, OB_X, OB_B, OB_C, OB_DT = 0, 32, 64, 72, 80


def _cparams(sem):
    return pltpu.CompilerParams(dimension_semantics=sem, vmem_limit_bytes=VMEM_LIMIT)


def _silu(x):
    return x * jax.nn.sigmoid(x)


def _split3(x):
    hi = x.astype(BF16)
    r1 = x - hi.astype(F32)
    mid = r1.astype(BF16)
    lo = (r1 - mid.astype(F32)).astype(BF16)
    return hi, mid, lo


def _dot(a, b):
    return jnp.dot(a, b, preferred_element_type=F32)


def _dot3_l(x, m):
    hi, mid, lo = _split3(x)
    return _dot(hi, m) + _dot(mid, m) + _dot(lo, m)


def _dot3_r(m, x):
    hi, mid, lo = _split3(x)
    return _dot(m, hi) + _dot(m, mid) + _dot(m, lo)


def _row_group(i, tm, mod_rows):
    return 0 if mod_rows is None else 1 + (i * tm) // mod_rows


def _ada_kernel(c_ref, w_ref, b_ref, o_ref):
    s = _silu(c_ref[...]).astype(BF16)
    o_ref[...] = _dot(s, w_ref[...].astype(BF16)) + b_ref[...]


def ada_modulation(cvecs, w_ada, b_ada):
    depth, d, n = w_ada.shape
    return pl.pallas_call(
        _ada_kernel,
        grid=(depth, n // ADA_TN),
        in_specs=[
            pl.BlockSpec((8, d), lambda l, j: (0, 0)),
            pl.BlockSpec((None, d, ADA_TN), lambda l, j: (l, 0, j)),
            pl.BlockSpec((None, 1, ADA_TN), lambda l, j: (l, 0, j)),
        ],
        out_specs=pl.BlockSpec((None, 8, ADA_TN), lambda l, j: (l, 0, j)),
        out_shape=jax.ShapeDtypeStruct((depth, 8, n), F32),
        compiler_params=_cparams(("parallel", "parallel")),
        name="ada_modulation",
    )(cvecs, w_ada, b_ada.reshape(depth, 1, n))


def _norm_mod(x, gain, sc, sh):
    ms = jnp.mean(x * x, axis=-1, keepdims=True)
    y = x * lax.rsqrt(ms + RMS_EPS) * gain
    return y * (1.0 + sc) + sh


def _inproj_kernel(x_ref, gain_ref, sc_ref, sh_ref, w_ref, o_ref, h_ref, *, nblk):
    @pl.when(pl.program_id(1) == 0)
    def _():
        h_ref[...] = _norm_mod(x_ref[...], gain_ref[...], sc_ref[...], sh_ref[...]).astype(BF16)

    acc = _dot(h_ref[...], w_ref[...])
    for k in range(nblk):
        o_ref[k] = acc[:, k * LANES:(k + 1) * LANES]


def in_projection(x, gain, sc, sh, w, layer, tn, mod_rows):
    m, d = x.shape
    n = w.shape[2]
    nblk = tn // LANES
    tm = INP_TM
    grp = lambda i, j: (_row_group(i, tm, mod_rows), 0, 0)
    return pl.pallas_call(
        functools.partial(_inproj_kernel, nblk=nblk),
        grid=(m // tm, n // tn),
        in_specs=[
            pl.BlockSpec((tm, d), lambda i, j: (i, 0)),
            pl.BlockSpec((1, d), lambda i, j: (0, 0)),
            pl.BlockSpec((None, 1, d), grp),
            pl.BlockSpec((None, 1, d), grp),
            pl.BlockSpec((None, d, tn), lambda i, j: (layer, 0, j)),
        ],
        out_specs=pl.BlockSpec((nblk, tm, LANES), lambda i, j: (j, i, 0)),
        out_shape=jax.ShapeDtypeStruct((n // LANES, m, LANES), F32),
        scratch_shapes=[pltpu.VMEM((tm, d), BF16)],
        compiler_params=_cparams(("parallel", "arbitrary")),
        name="in_projection",
    )(x, gain, sc, sh, w)


def _ffn_kernel(x_ref, gain_ref, sc_ref, sh_ref, g_ref, wg_ref, wu_ref, wd_ref, o_ref, h_ref, acc_ref):
    f = pl.program_id(1)

    @pl.when(f == 0)
    def _():
        h_ref[...] = _norm_mod(x_ref[...], gain_ref[...], sc_ref[...], sh_ref[...]).astype(BF16)
        acc_ref[...] = jnp.zeros_like(acc_ref)

    h = h_ref[...]
    gate = _dot(h, wg_ref[...])
    up = _dot(h, wu_ref[...])
    ff = (_silu(gate) * up).astype(BF16)
    acc_ref[...] += _dot(ff, wd_ref[...])

    @pl.when(f == pl.num_programs(1) - 1)
    def _():
        o_ref[...] = x_ref[...] + g_ref[...] * acc_ref[...]


def ffn(x, gain, sc, sh, g, wg, wu, wd, layer, mod_rows):
    m, d = x.shape
    dff = wg.shape[2]
    tm = FFN_TM
    grp = lambda i, f: (_row_group(i, tm, mod_rows), 0, 0)
    return pl.pallas_call(
        _ffn_kernel,
        grid=(m // tm, dff // FF_TN),
        in_specs=[
            pl.BlockSpec((tm, d), lambda i, f: (i, 0)),
            pl.BlockSpec((1, d), lambda i, f: (0, 0)),
            pl.BlockSpec((None, 1, d), grp),
            pl.BlockSpec((None, 1, d), grp),
            pl.BlockSpec((None, 1, d), grp),
            pl.BlockSpec((None, d, FF_TN), lambda i, f: (layer, 0, f)),
            pl.BlockSpec((None, d, FF_TN), lambda i, f: (layer, 0, f)),
            pl.BlockSpec((None, FF_TN, d), lambda i, f: (layer, f, 0)),
        ],
        out_specs=pl.BlockSpec((tm, d), lambda i, f: (i, 0)),
        out_shape=jax.ShapeDtypeStruct((m, d), F32),
        scratch_shapes=[pltpu.VMEM((tm, d), BF16), pltpu.VMEM((tm, d), F32)],
        compiler_params=_cparams(("parallel", "arbitrary")),
        name="ffn",
    )(x, gain, sc, sh, g, wg, wu, wd)


def _outproj_even_kernel(x_ref, g_ref, a_ref, b_ref, wa_ref, wb_ref, o_ref):
    acc = _dot(a_ref[...], wa_ref[...]) + _dot(b_ref[...], wb_ref[...])
    o_ref[...] = x_ref[...] + g_ref[...] * acc


def out_projection_even(x, g, ha, ob, w, layer, mod_rows):
    m, d = x.shape
    ka, kb = ha.shape[1], ob.shape[1]
    grp = lambda i: (_row_group(i, TM, mod_rows), 0, 0)
    return pl.pallas_call(
        _outproj_even_kernel,
        grid=(m // TM,),
        in_specs=[
            pl.BlockSpec((TM, d), lambda i: (i, 0)),
            pl.BlockSpec((None, 1, d), grp),
            pl.BlockSpec((TM, ka), lambda i: (i, 0)),
            pl.BlockSpec((TM, kb), lambda i: (i, 0)),
            pl.BlockSpec((None, ka, d), lambda i: (layer, 0, 0)),
            pl.BlockSpec((None, kb, d), lambda i: (layer, ka // kb, 0)),
        ],
        out_specs=pl.BlockSpec((TM, d), lambda i: (i, 0)),
        out_shape=jax.ShapeDtypeStruct((m, d), F32),
        compiler_params=_cparams(("parallel",)),
        name="out_projection_even",
    )(x, g, ha, ob, w, w)


def _outproj_odd_kernel(x_ref, g_ref, y_ref, gain_ref, w_ref, o_ref, yn_ref):
    @pl.when(pl.program_id(1) == 0)
    def _():
        y = y_ref[...]
        ms = jnp.mean(y * y, axis=-1, keepdims=True)
        yn_ref[...] = (y * lax.rsqrt(ms + RMS_EPS) * gain_ref[...]).astype(BF16)

    o_ref[...] = x_ref[...] + g_ref[...] * _dot(yn_ref[...], w_ref[...])


def out_projection_odd(x, g, y, gain, w, layer, mod_rows):
    m, d = x.shape
    k = y.shape[1]
    grp = lambda i, j: (_row_group(i, TM, mod_rows), 0, j)
    return pl.pallas_call(
        _outproj_odd_kernel,
        grid=(m // TM, d // OUT_TN),
        in_specs=[
            pl.BlockSpec((TM, OUT_TN), lambda i, j: (i, j)),
            pl.BlockSpec((None, 1, OUT_TN), grp),
            pl.BlockSpec((TM, k), lambda i, j: (i, 0)),
            pl.BlockSpec((1, k), lambda i, j: (0, 0)),
            pl.BlockSpec((None, k, OUT_TN), lambda i, j: (layer, 0, j)),
        ],
        out_specs=pl.BlockSpec((TM, OUT_TN), lambda i, j: (i, j)),
        out_shape=jax.ShapeDtypeStruct((m, d), F32),
        scratch_shapes=[pltpu.VMEM((TM, k), BF16)],
        compiler_params=_cparams(("parallel", "arbitrary")),
        name="out_projection_odd",
    )(x, g, y, gain, w)


def _lane_iota(shape):
    return lax.broadcasted_iota(jnp.int32, shape, len(shape) - 1)


def _half_rms_norm(x, gain):
    lo = _lane_iota(x.shape) < DB
    x2 = x * x
    s0 = jnp.sum(jnp.where(lo, x2, 0.0), axis=-1, keepdims=True)
    s1 = jnp.sum(jnp.where(lo, 0.0, x2), axis=-1, keepdims=True)
    r = jnp.where(lo, lax.rsqrt(s0 * (1.0 / DB) + RMS_EPS), lax.rsqrt(s1 * (1.0 / DB) + RMS_EPS))
    return x * r * gain


def _rope(x, cos, sin_signed):
    first = (_lane_iota(x.shape) % 32) < ROPE_PAIRS
    partner = jnp.where(first, pltpu.roll(x, LANES - ROPE_PAIRS, 1), pltpu.roll(x, ROPE_PAIRS, 1))
    return x * cos + partner * sin_signed


def _diff_attn_core(q, kk, vv, lam, dgain, lam_init):
    lo = _lane_iota(q.shape) < DB
    qs = q * (DB ** -0.5)
    nt = (((1,), (1,)), ((), ()))
    s = [lax.dot_general(jnp.where(lo, qs, 0.0).astype(BF16), kk, nt, preferred_element_type=F32),
         lax.dot_general(jnp.where(lo, 0.0, qs).astype(BF16), kk, nt, preferred_element_type=F32)]
    p = [jnp.exp(si - jnp.max(si, axis=-1, keepdims=True)) for si in s]
    pv = [_dot(pi.astype(BF16), vv) * (1.0 / jnp.sum(pi, axis=-1, keepdims=True)) for pi in p]
    o = pv[0] - lam * pv[1]
    ms = jnp.mean(o * o, axis=-1, keepdims=True)
    return o * lax.rsqrt(ms + RMS_EPS) * dgain * (1.0 - lam_init)


def _lambda_value(lam_ref, lam_init):
    lp = lam_ref[...]
    a = jnp.sum(lp[0:1] * lp[1:2], axis=-1, keepdims=True)
    b = jnp.sum(lp[2:3] * lp[3:4], axis=-1, keepdims=True)
    return jnp.exp(a) - jnp.exp(b) + lam_init


def _attn_ctx_kernel(lam_ref, q_ref, k_ref, v_ref, qg_ref, kg_ref, dg_ref, *rest, lam_init, plane):
    o_ref, kd_ref, vd_ref = rest[-3:]
    for p in range(kd_ref.shape[0]):
        if p != plane:
            kd_ref[p] = jnp.zeros(kd_ref.shape[1:], F32)
            vd_ref[p] = jnp.zeros(vd_ref.shape[1:], F32)
    lam = _lambda_value(lam_ref, lam_init)
    nt = (((1,), (1,)), ((), ()))

    def head(h):
        cs = slice(h * LANES, (h + 1) * LANES)
        kd = _half_rms_norm(k_ref[h], kg_ref[...])
        kd_ref[plane, :, cs] = kd
        v = v_ref[h]
        vd_ref[plane, :, cs] = v
        q = _half_rms_norm(q_ref[h], qg_ref[...]) * (DB ** -0.5)
        yield
        lo = _lane_iota(q.shape) < DB
        kk, vv = kd.astype(BF16), v.astype(BF16)
        s = [lax.dot_general(jnp.where(lo, q, 0.0).astype(BF16), kk, nt, preferred_element_type=F32),
             lax.dot_general(jnp.where(lo, 0.0, q).astype(BF16), kk, nt, preferred_element_type=F32)]
        yield
        p = [jnp.exp(si - jnp.max(si, axis=-1, keepdims=True)) for si in s]
        yield
        pv = [_dot(pi.astype(BF16), vv) * (1.0 / jnp.sum(pi, axis=-1, keepdims=True)) for pi in p]
        yield
        o = pv[0] - lam * pv[1]
        ms = jnp.mean(o * o, axis=-1, keepdims=True)
        o_ref[:, cs] = (o * lax.rsqrt(ms + RMS_EPS) * dg_ref[...] * (1.0 - lam_init)).astype(BF16)
        yield

    for h0 in range(0, H_B, ATT_CTX_LOCKSTEP):
        for _ in zip(*[head(h) for h in range(h0, h0 + ATT_CTX_LOCKSTEP)]):
            pass


def diff_attention_ctx(proj, lam_p, qg, kg, dg, layer, n_layers, prev_kv, nb, seq, lam_init):
    m = nb * seq
    vec = pl.BlockSpec((1, LANES), lambda b: (0, 0))
    slab = lambda base: pl.BlockSpec((H_B, seq, LANES), lambda b: (base // H_B, b, 0))
    outb = pl.BlockSpec((seq, H_B * LANES), lambda b: (b, 0))
    prev = () if prev_kv is None else tuple(prev_kv)
    n_planes, plane = (1, 0) if prev else (n_layers, layer)
    kvb = pl.BlockSpec((None, n_planes, seq, H_B * LANES), lambda b: (b, layer if prev else 0, 0, 0))
    kv_shape = jax.ShapeDtypeStruct((nb, n_layers, seq, H_B * LANES), F32)
    return pl.pallas_call(
        functools.partial(_attn_ctx_kernel, lam_init=lam_init, plane=plane),
        grid=(nb,),
        in_specs=[pl.BlockSpec((4, DB), lambda b: (0, 0)), slab(EB_BQ), slab(EB_BK), slab(EB_BV), vec, vec, vec]
        + [pl.BlockSpec(memory_space=pl.ANY)] * len(prev),
        out_specs=[outb, kvb, kvb],
        out_shape=[jax.ShapeDtypeStruct((m, B_V), BF16), kv_shape, kv_shape],
        input_output_aliases={7: 1, 8: 2} if prev else {},
        compiler_params=_cparams(("parallel",)),
        name="diff_attention_ctx",
    )(lam_p, proj, proj, proj, qg, kg, dg, *prev)


def _attn_dec_kernel(lam_ref, q_ref, k_ref, v_ref, kc_ref, vc_ref, qg_ref, kg_ref, dg_ref, cq_ref, sq_ref, ck_ref,
                     sk_ref, o_ref, ks_ref, vs_ref, *, lam_init, past):
    @pl.when(pl.program_id(2) == 0)
    def _():
        ks_ref[0:past, :] = kc_ref[...].astype(BF16)
        vs_ref[0:past, :] = vc_ref[...].astype(BF16)
        kd = _rope(_half_rms_norm(k_ref[...], kg_ref[...]), ck_ref[...], sk_ref[...])
        ks_ref[past:, :] = kd.astype(BF16)
        vs_ref[past:, :] = v_ref[...].astype(BF16)

    q = _rope(_half_rms_norm(q_ref[...], qg_ref[...]), cq_ref[...], sq_ref[...])
    lam = _lambda_value(lam_ref, lam_init)
    o_ref[...] = _diff_attn_core(q, ks_ref[...], vs_ref[...], lam, dg_ref[...], lam_init).astype(BF16)


def diff_attention_dec(proj, cache_k, cache_v, lam_p, qg, kg, dg, cos, sin, row0, nb, seq, lam_init):
    past = cache_k.shape[1]
    nq = seq // ATT_TQ
    rq, rk = row0 // ATT_TQ, row0 // seq
    vec = pl.BlockSpec((1, LANES), lambda b, h, i: (0, 0))
    kslab = lambda base: pl.BlockSpec((None, seq, LANES), lambda b, h, i: (base + h, rk + b, 0))
    cslab = pl.BlockSpec((None, past, LANES), lambda b, h, i: (b, 0, h))
    tq = pl.BlockSpec((ATT_TQ, LANES), lambda b, h, i: (i, 0))
    tk = pl.BlockSpec((seq, LANES), lambda b, h, i: (0, 0))
    return pl.pallas_call(
        functools.partial(_attn_dec_kernel, lam_init=lam_init, past=past),
        grid=(nb, H_B, nq),
        in_specs=[pl.BlockSpec((4, DB), lambda b, h, i: (0, 0)),
                  pl.BlockSpec((None, ATT_TQ, LANES), lambda b, h, i: (EB_BQ + h, rq + b * nq + i, 0)),
                  kslab(EB_BK), kslab(EB_BV), cslab, cslab, vec, vec, vec, tq, tq, tk, tk],
        out_specs=pl.BlockSpec((ATT_TQ, LANES), lambda b, h, i: (b * nq + i, h)),
        out_shape=jax.ShapeDtypeStruct((nb * seq, B_V), BF16),
        scratch_shapes=[pltpu.VMEM((past + seq, LANES), BF16), pltpu.VMEM((past + seq, LANES), BF16)],
        compiler_params=_cparams(("parallel", "parallel", "arbitrary")),
        name="diff_attention_dec",
    )(lam_p, proj, proj, proj, cache_k, cache_v, qg, kg, dg, cos, sin, cos, sin)


def rope_tables(n_tokens):
    n_rows = n_tokens // GRID_W
    rows, cols = jnp.meshgrid(jnp.arange(n_rows, dtype=F32), jnp.arange(GRID_W, dtype=F32), indexing='ij')
    inv_freq = ROPE_THETA ** (-jnp.arange(ROPE_PAIRS, dtype=F32) / ROPE_PAIRS)
    ang = jnp.stack([rows.reshape(-1, 1) * inv_freq, cols.reshape(-1, 1) * inv_freq], axis=0)
    cos, sin = jnp.cos(ang), jnp.sin(ang)
    cos_half = jnp.concatenate([cos[0], cos[0], cos[1], cos[1]], axis=-1)
    sin_half = jnp.concatenate([-sin[0], sin[0], -sin[1], sin[1]], axis=-1)
    return jnp.tile(cos_half, (1, 2)), jnp.tile(sin_half, (1, 2))


def _tri(n, lower):
    r = lax.broadcasted_iota(jnp.int32, (n, n), 0)
    c = lax.broadcasted_iota(jnp.int32, (n, n), 1)
    return jnp.where((r >= c) if lower else (r <= c), 1.0, 0.0).astype(BF16)


def _mlstm_stage_a(d, rows, slot, q_ref, k_ref, v_ref, gcol, grow, tri_l, tri_u, ia_ref, ba_ref, mi_ref, ua_ref,
                   sc_ref):
    lc = MLSTM_CHUNK
    ri = lax.broadcasted_iota(jnp.int32, (lc, lc), 0)
    ci = lax.broadcasted_iota(jnp.int32, (lc, lc), 1)
    mask = (ci <= ri) if d == 0 else (ci >= ri)
    lo = _lane_iota((lc, LANES)) < DK_A
    nt = (((1,), (1,)), ((), ()))
    bcol_all = _dot3_r(tri_l if d == 0 else tri_u, jax.nn.log_sigmoid(gcol))
    brow_all = _dot3_l(jax.nn.log_sigmoid(grow), tri_u if d == 0 else tri_l)
    yield
    q = q_ref[rows, :]
    k = k_ref[rows, :] * (DK_A ** -0.5)
    kb = k.astype(BF16)
    ones = jnp.ones((lc, LANES), BF16)
    vs = [v_ref[j, rows, :].astype(BF16) for j in range(2)]
    heads = (0, 1)
    qk = [lax.dot_general((jnp.where(lo, q, 0.0) if j == 0 else jnp.where(lo, 0.0, q)).astype(BF16), kb, nt,
                          preferred_element_type=F32) for j in heads]
    b_all = [jnp.broadcast_to(bcol_all[:, 16 + 8 * d + j:16 + 8 * d + j + 1], (lc, LANES)) for j in heads]
    li_all = [jnp.broadcast_to(gcol[:, 8 * d + j:8 * d + j + 1], (lc, LANES)) for j in heads]
    yield
    dmat = [jnp.where(mask, b_all[j][:, :lc] - brow_all[16 + 8 * d + j:16 + 8 * d + j + 1, :]
                      + grow[8 * d + j:8 * d + j + 1, :], -jnp.inf) for j in heads]
    m_intra = [jnp.max(dmat[j], axis=-1, keepdims=True) for j in heads]
    b_end = [b_all[j][lc - 1:lc, :] if d == 0 else b_all[j][0:1, :] for j in heads]
    g_all = [b_end[j] - b_all[j] + li_all[j] for j in heads]
    g_max = [jnp.max(g_all[j], axis=0, keepdims=True) for j in heads]
    yield
    s = [(qk[j] * jnp.exp(dmat[j] - m_intra[j])).astype(BF16) for j in heads]
    wkb = (k * jnp.where(lo, jnp.exp(g_all[0] - g_max[0]), jnp.exp(g_all[1] - g_max[1]))).astype(BF16)
    yield
    for j in heads:
        ia_ref[slot, d, j] = _dot(s[j], jnp.concatenate([vs[j], ones], axis=1))
        ba_ref[slot, d, j] = b_all[j]
        mi_ref[slot, d, j] = jnp.broadcast_to(m_intra[j], (lc, LANES))
        sc_ref[slot, d, j, 0:1, :] = g_max[j]
        sc_ref[slot, d, j, 1:2, :] = b_end[j]
    tn = (((0,), (0,)), ((), ()))
    ua_ref[slot, d] = lax.dot_general(wkb, jnp.concatenate(vs + [ones], axis=1), tn, preferred_element_type=F32)
    yield


def _mlstm_stage_b(d, rows, slot, q_ref, m_prev, cn_ref, h_out, m_out, ia_ref, ba_ref, mi_ref, ua_ref, sc_ref):
    lc = MLSTM_CHUNK
    lo = _lane_iota((lc, LANES)) < DK_A
    heads = (0, 1)
    q = q_ref[rows, :]
    cn = cn_ref[d]
    cnb = cn.astype(BF16)
    inter_nd = [_dot((jnp.where(lo, q, 0.0) if j == 0 else jnp.where(lo, 0.0, q)).astype(BF16), cnb) for j in heads]
    yield
    inter = [ba_ref[slot, d, j] + m_prev[j] for j in heads]
    m_row = [jnp.maximum(inter[j], mi_ref[slot, d, j]) for j in heads]
    w_inter = [jnp.exp(inter[j] - m_row[j]) for j in heads]
    w_intra = [jnp.exp(mi_ref[slot, d, j] - m_row[j]) for j in heads]
    g_max = [sc_ref[slot, d, j, 0:1, :] for j in heads]
    b_end = [sc_ref[slot, d, j, 1:2, :] for j in heads]
    m_new = [jnp.maximum(b_end[j] + m_prev[j], g_max[j]) for j in heads]
    decay = [jnp.exp(b_end[j] + m_prev[j] - m_new[j]) for j in heads]
    fac = [jnp.exp(g_max[j] - m_new[j]) for j in heads]
    m_out.extend(m_new)
    yield
    u = ua_ref[slot, d]
    top = lax.broadcasted_iota(jnp.int32, (LANES, LANES), 0) < DK_A
    dec = jnp.where(top, decay[0], decay[1])
    f = jnp.where(top, fac[0], fac[1])
    c_new = dec * cn[:, :LANES] + f * jnp.where(top, u[:, :LANES], u[:, LANES:2 * LANES])
    n_new = dec * cn[:, LANES:] + f * u[:, 2 * LANES:]
    cn_ref[d] = jnp.concatenate([c_new, n_new], axis=1)
    yield
    hs = []
    for j in heads:
        intra_nd = ia_ref[slot, d, j]
        num = w_inter[j] * inter_nd[j][:, :LANES] + w_intra[j] * intra_nd[:, :LANES]
        den = w_inter[j] * inter_nd[j][:, LANES:] + w_intra[j] * intra_nd[:, LANES:]
        hs.append(num * (1.0 / jnp.maximum(jnp.abs(den), jnp.exp(-m_row[j]))))
    h_out[rows, :] = jnp.concatenate(hs, axis=-1)
    yield


def _mlstm_kernel(q_ref, k_ref, v_ref, ao_ref, g_ref, bias_ref, gain_ref, c0_ref, n0_ref, m0_ref,
                  h_ref, cf_ref, nf_ref, mf_ref, gc_ref, gt_ref, hf_ref, hb_ref, cn_ref, ia_ref, ba_ref, mi_ref,
                  ua_ref, sc_ref, *, seq):
    lc = MLSTM_CHUNK
    nc = seq // lc
    pair = pl.program_id(1)
    shift = (LANES - 2 * pair) % LANES
    bias = pltpu.roll(jnp.broadcast_to(bias_ref[...], (8, LANES)), shift, 1)[0:1]
    for blk in range(seq // LANES):
        r = slice(blk * LANES, (blk + 1) * LANES)
        gs = pltpu.roll(g_ref[r, :], shift, 1) + bias
        gc_ref[r, :] = gs
        gst = gs.T
        for half in range(LANES // lc):
            gt_ref[blk * (LANES // lc) + half] = gst[0:32, half * lc:(half + 1) * lc]
    for d in range(2):
        cn_ref[d, :, 0:LANES] = c0_ref[d]
        cn_ref[d, :, LANES:] = jnp.broadcast_to(n0_ref[d], (LANES, LANES)).T
    tri_l, tri_u = _tri(lc, True), _tri(lc, False)
    stage = (ia_ref, ba_ref, mi_ref, ua_ref, sc_ref)

    def chunk_rows(c):
        return pl.ds(pl.multiple_of(c * lc, lc), lc)

    def stage_a(i, slot):
        gens = []
        for d, c in ((0, i), (1, nc - 1 - i)):
            rows = chunk_rows(c)
            gens.append(_mlstm_stage_a(d, rows, slot, q_ref, k_ref, v_ref, gc_ref[rows, :], gt_ref[c], tri_l, tri_u,
                                       *stage))
        return gens

    def stage_b(i, slot, ms, outs):
        return [_mlstm_stage_b(d, chunk_rows(c), slot, q_ref, ms[2 * d:2 * d + 2], cn_ref, h_out, outs[d], *stage)
                for d, c, h_out in ((0, i, hf_ref), (1, nc - 1 - i, hb_ref))]

    def run(gens):
        for _ in itertools.zip_longest(*gens):
            pass

    def body(kk, ms):
        outs = ([], [])
        run(stage_b(2 * kk, 0, ms, outs) + stage_a(2 * kk + 1, 1))
        ms = tuple(outs[0] + outs[1])
        outs = ([], [])
        run(stage_b(2 * kk + 1, 1, ms, outs) + stage_a(jnp.minimum(2 * kk + 2, nc - 1), 0))
        return tuple(outs[0] + outs[1])

    m0 = m0_ref[...]
    rep = lambda v: jnp.broadcast_to(v, (1, LANES))
    init = (rep(m0[0][:, 0:1]), rep(m0[0][:, DK_A:DK_A + 1]), rep(m0[1][:, 0:1]), rep(m0[1][:, DK_A:DK_A + 1]))
    run(stage_a(0, 0))
    m_f0, m_f1, m_b0, m_b1 = lax.fori_loop(0, nc // 2, body, init)
    lane1 = _lane_iota((1, LANES)) < DK_A
    mf_ref[0] = jnp.where(lane1, m_f0, m_f1)
    mf_ref[1] = jnp.where(lane1, m_b0, m_b1)
    for d in range(2):
        cf_ref[d] = cn_ref[d, :, 0:LANES]
        nf_ref[d] = cn_ref[d, :, LANES:].T[0:1, :]

    gain = gain_ref[...]
    for blk in range(seq // LANES):
        r = slice(blk * LANES, (blk + 1) * LANES)
        outs = []
        for j in range(2):
            cs = slice(j * DV_A, (j + 1) * DV_A)
            hh = hf_ref[r, cs] + hb_ref[r, cs]
            ms = jnp.mean(hh * hh, axis=-1, keepdims=True)
            outs.append(hh * lax.rsqrt(ms + RMS_EPS) * gain[:, cs] * jax.nn.sigmoid(ao_ref[j, r, :]))
        h_ref[r, :] = jnp.concatenate(outs, axis=-1).astype(BF16)


def mlstm(proj, bias, gain, c0, n0, m0, row0, nb, seq):
    rb = row0 // seq
    npair = H_A // 2
    slab = lambda base: pl.BlockSpec((None, seq, LANES), lambda b, p: (base + p, rb + b, 0))
    slab2 = lambda base: pl.BlockSpec((2, seq, LANES), lambda b, p: (base // 2 + p, rb + b, 0))
    st_c = pl.BlockSpec((None, None, 2, LANES, LANES), lambda b, p: (b, p, 0, 0, 0))
    st_v = pl.BlockSpec((None, None, 2, 1, LANES), lambda b, p: (b, p, 0, 0, 0))
    return pl.pallas_call(
        functools.partial(_mlstm_kernel, seq=seq),
        grid=(nb, npair),
        in_specs=[slab(EB_Q), slab(EB_K), slab2(EB_V), slab2(EB_O),
                  pl.BlockSpec((None, seq, LANES), lambda b, p: (EB_G, rb + b, 0)),
                  pl.BlockSpec((1, LANES), lambda b, p: (0, 0)),
                  pl.BlockSpec((1, 2 * DV_A), lambda b, p: (0, p)),
                  st_c, st_v, st_v],
        out_specs=[pl.BlockSpec((seq, 2 * DV_A), lambda b, p: (b, p)), st_c, st_v, st_v],
        out_shape=[jax.ShapeDtypeStruct((nb * seq, A_V), BF16),
                   jax.ShapeDtypeStruct((nb, npair, 2, LANES, LANES), F32),
                   jax.ShapeDtypeStruct((nb, npair, 2, 1, LANES), F32),
                   jax.ShapeDtypeStruct((nb, npair, 2, 1, LANES), F32)],
        scratch_shapes=[pltpu.VMEM((seq, LANES), F32), pltpu.VMEM((seq // MLSTM_CHUNK, 32, MLSTM_CHUNK), F32),
                        pltpu.VMEM((seq, 2 * DV_A), F32),
                        pltpu.VMEM((seq, 2 * DV_A), F32), pltpu.VMEM((2, LANES, 2 * LANES), F32),
                        pltpu.VMEM((2, 2, 2, MLSTM_CHUNK, 2 * LANES), F32), pltpu.VMEM((2, 2, 2, MLSTM_CHUNK, LANES), F32),
                        pltpu.VMEM((2, 2, 2, MLSTM_CHUNK, LANES), F32), pltpu.VMEM((2, 2, LANES, 3 * LANES), F32),
                        pltpu.VMEM((2, 2, 2, 8, LANES), F32)],
        compiler_params=_cparams(("parallel", "parallel")),
        name="mlstm",
    )(proj, proj, proj, proj, proj, bias, gain, c0, n0, m0)


def _softplus(x):
    return jnp.maximum(x, 0.0) + jnp.log1p(jnp.exp(-jnp.abs(x)))


SSD_REP = 2 * R_C


def _pack3(x, lane):
    hi = x.astype(BF16).astype(F32)
    r1 = x - hi
    mid = r1.astype(BF16).astype(F32)
    r2 = r1 - mid
    grp = lane % (3 * SSD_REP)
    return jnp.where(grp < SSD_REP, x, jnp.where(grp < 2 * SSD_REP, r1, r2)).astype(BF16)


SSD_HALO = 8


def _ssd_kernel(xs_ref, bm_ref, cm_ref, z_ref, dtr_ref, wx_ref, wb_ref, wc_ref, bx_ref, bb_ref, bc_ref,
                 dtb_ref, alog_ref, dsk_ref, rep_ref, selw_ref, expf_ref, expb_ref, *rest, seq, has_init, has_prev,
                 plane):
    if has_init:
        s0_ref = rest[0]
        rest = rest[1:]
    if has_prev:
        rest = rest[1:]
    (y_ref, sfin_ref, pad_ref, xa_ref, ba_ref, ca_ref, dt_ref, af_ref, ab_ref, st_ref, ex_ref, abw_ref,
     at_ref, cb_ref) = rest
    lc = SSD_CHUNK
    nc = seq // lc
    gw = R_C * P_C
    nslab = gw // LANES
    lane = _lane_iota((lc, LANES))
    lo = lane < P_C
    fwd_lane = (lane % SSD_REP) < R_C
    first3 = lane < 3 * SSD_REP
    a_neg = -jnp.exp(alog_ref[...])
    tri_l, tri_u = _tri(lc, True), _tri(lc, False)
    ri = lax.broadcasted_iota(jnp.int32, (lc, lc), 0)
    ci = lax.broadcasted_iota(jnp.int32, (lc, lc), 1)
    causal, anti = ci <= ri, ci >= ri
    nt = (((1,), (1,)), ((), ()))

    def chunk_rows(c):
        return pl.ds(pl.multiple_of(c * lc, lc), lc)

    halo = jnp.zeros((SSD_HALO, LANES), F32)
    for s in range(nslab + 2):
        pad_ref[s, 0:SSD_HALO, :] = halo
        pad_ref[s, SSD_HALO + seq:, :] = halo
        src = xs_ref[s] if s < nslab else (bm_ref[...] if s == nslab else cm_ref[...])
        pad_ref[s, SSD_HALO:SSD_HALO + seq, :] = src

    def conv_chunk(s, base, w, b):
        y = w[0:1] * pad_ref[s, pl.ds(base + SSD_HALO - 2, lc), :]
        y = y + w[1:2] * pad_ref[s, pl.ds(base + SSD_HALO - 1, lc), :]
        y = y + w[2:3] * pad_ref[s, pl.ds(base + SSD_HALO, lc), :]
        y = y + w[3:4] * pad_ref[s, pl.ds(base + SSD_HALO + 1, lc), :]
        return _silu(y + b)

    def prep(c):
        base = pl.multiple_of(c * lc, lc)
        rows = pl.ds(base, lc)
        dt_pre = _dot3_l(dtr_ref[rows, :], rep_ref[...])
        yield
        for s in range(nslab // 2):
            cs = slice(s * LANES, (s + 1) * LANES)
            xa_ref[rows, cs] = conv_chunk(s, base, wx_ref[:, cs], bx_ref[:, cs])
            yield
        dt = _softplus(dt_pre + dtb_ref[...])
        dt_ref[rows, :] = dt
        da = dt * a_neg
        acum_f, acum_b = _dot3_r(tri_l, da), _dot3_r(tri_u, da)
        yield
        for s in range(nslab // 2, nslab):
            cs = slice(s * LANES, (s + 1) * LANES)
            xa_ref[rows, cs] = conv_chunk(s, base, wx_ref[:, cs], bx_ref[:, cs])
            yield
        ba_ref[rows, :] = conv_chunk(nslab, base, wb_ref[...], bb_ref[...])
        yield
        ca_ref[rows, :] = conv_chunk(nslab + 1, base, wc_ref[...], bc_ref[...])
        af_ref[rows, :] = acum_f
        ab_ref[rows, :] = acum_b
        yield

    def prep_body(k, carry):
        for _ in zip(prep(2 * k), prep(2 * k + 1)):
            pass
        return carry

    lax.fori_loop(0, nc // 2, prep_body, 0)

    for d in range(2):
        for k in range(nslab):
            cs = slice(k * LANES, (k + 1) * LANES)
            st_ref[d, :, cs] = s0_ref[d, cs, :].T if has_init else jnp.zeros((N_C, LANES), F32)

    def expand(ea, te, e_ref):
        return _dot(_pack3(jnp.where(first3, ea, te), lane), e_ref[...])

    def fwd_a(c, slot):
        rows = chunk_rows(c)
        acum = af_ref[rows, :]
        a_end = acum[lc - 1:lc, :]
        ea, te = jnp.exp(acum), jnp.exp(a_end - acum) * dt_ref[rows, :]
        yield
        ex_ref[slot] = expand(ea, te, expf_ref)
        yield

    def fwd_b(c, slot):
        rows = chunk_rows(c)
        ea_x, te_x = ex_ref[slot, :, 0:gw], ex_ref[slot, :, gw:2 * gw]
        st = st_ref[0]
        cs_ = _dot(ca_ref[rows, :].astype(BF16), st.astype(BF16))
        bc_t = ba_ref[rows, :].T.astype(BF16)
        yield
        upd = _dot(bc_t, (xa_ref[rows, :] * te_x).astype(BF16))
        yield
        y_ref[rows, :] = cs_ * ea_x
        st_ref[0] = st * ea_x[lc - 1:lc, :] + upd
        yield

    def bwd_a(c, slot):
        rows = chunk_rows(c)
        dt = dt_ref[rows, :]
        acum_b = ab_ref[rows, :]
        acum = jnp.where(fwd_lane, af_ref[rows, :], acum_b)
        at_ref[slot, 0] = acum.T
        at_ref[slot, 1] = dt.T
        yield
        abw_ref[slot] = _dot(_pack3(acum, lane), selw_ref[...])
        yield
        a_end = acum_b[0:1, :]
        ex_ref[slot] = expand(jnp.exp(acum_b), jnp.exp(a_end - acum_b) * dt, expb_ref)
        yield
        cb_ref[slot] = lax.dot_general(ca_ref[rows, :].astype(BF16), ba_ref[rows, :].astype(BF16), nt,
                                       preferred_element_type=F32)
        yield

    def bwd_b(c, slot):
        rows = chunk_rows(c)
        ea_x, te_x = ex_ref[slot, :, 0:gw], ex_ref[slot, :, gw:2 * gw]
        cb = cb_ref[slot]
        st = st_ref[1]
        xs = xa_ref[rows, :]
        y_inter = _dot(ca_ref[rows, :].astype(BF16), st.astype(BF16)) * ea_x
        bc_t = ba_ref[rows, :].T.astype(BF16)
        upd = _dot(bc_t, (xs * te_x).astype(BF16))
        yield
        for s in range(nslab):
            cs = slice(s * LANES, (s + 1) * LANES)
            ws = []
            for j in range(2):
                r = 2 * s + j
                rb = R_C + r
                aj_f, aj_b = at_ref[slot, 0, r:r + 1, :], at_ref[slot, 0, rb:rb + 1, :]
                dt_f, dt_b = at_ref[slot, 1, r:r + 1, :], at_ref[slot, 1, rb:rb + 1, :]
                wf = jnp.exp(jnp.where(causal, abw_ref[slot, :, r * LANES:(r + 1) * LANES] - aj_f, -jnp.inf))
                wb = jnp.exp(jnp.where(anti, abw_ref[slot, :, rb * LANES:(rb + 1) * LANES] - aj_b, -jnp.inf))
                ws.append((cb * (wf * dt_f + wb * dt_b)).astype(BF16))
            xsl = xs[:, cs]
            x2 = jnp.concatenate([jnp.where(lo, xsl, 0.0), jnp.where(lo, 0.0, xsl)], axis=0).astype(BF16)
            y = y_ref[rows, cs] + y_inter[:, cs] + dsk_ref[:, cs] * xsl + _dot(jnp.concatenate(ws, axis=1), x2)
            y_ref[rows, cs] = y * _silu(z_ref[s, rows, :])
            yield
        st_ref[1] = st * ea_x[0:1, :] + upd
        yield

    def run(*gens):
        for _ in itertools.zip_longest(*gens):
            pass

    def sweep(stage_a, stage_b, order):
        run(stage_a(order(0), 0))

        def body(k, carry):
            run(stage_b(order(2 * k), 0), stage_a(order(2 * k + 1), 1))
            run(stage_b(order(2 * k + 1), 1), stage_a(order(jnp.minimum(2 * k + 2, nc - 1)), 0))
            return carry

        lax.fori_loop(0, nc // 2, body, 0)

    sweep(fwd_a, fwd_b, lambda i: i)
    sweep(bwd_a, bwd_b, lambda i: nc - 1 - i)

    for d in range(2):
        for k in range(nslab):
            cs = slice(k * LANES, (k + 1) * LANES)
            sfin_ref[plane, d, cs, :] = st_ref[d, :, cs].T
    for p in range(sfin_ref.shape[0]):
        if p != plane:
            sfin_ref[p] = jnp.zeros(sfin_ref.shape[1:], F32)


def ssd(proj, conv_w, conv_b, dtb, alog, dsk, s0, layer, n_layers, out_plane, prev_states, nb, seq):
    gw = R_C * P_C
    nslab = gw // LANES
    slab = lambda base: pl.BlockSpec((None, seq, LANES), lambda b, g: (base + g, b, 0))
    slab4 = lambda base: pl.BlockSpec((nslab, seq, LANES), lambda b, g: (base // nslab + g, b, 0))
    xo, bo, co = 0, D_INNER // LANES, (D_INNER + G_C * N_C) // LANES
    cw = lambda rws, width, off: pl.BlockSpec((rws, width), lambda b, g: (0, off * LANES // width + g))
    vec = pl.BlockSpec((None, 1, LANES), lambda b, g: (g, 0, 0))
    full = lambda a: pl.BlockSpec(a.shape, lambda b, g: (0,) * a.ndim)
    st_in = pl.BlockSpec((None, None, 2, gw, N_C), lambda b, g: (b, layer, 0, g, 0))
    rep, selw, expf, expb = _ssd_constants()
    has_init, has_prev = s0 is not None, prev_states is not None
    n_planes, plane = (1, 0) if has_prev else (n_layers, out_plane)
    st_out = pl.BlockSpec((None, n_planes, 2, gw, N_C), lambda b, g: (b, out_plane if has_prev else 0, 0, g, 0))
    st_shape = jax.ShapeDtypeStruct((nb, n_layers, 2, H_C * P_C, N_C), F32)
    vm = lambda *shape: pltpu.VMEM(shape, F32)
    n_in = 18 + has_init
    return pl.pallas_call(
        functools.partial(_ssd_kernel, seq=seq, has_init=has_init, has_prev=has_prev, plane=plane),
        grid=(nb, G_C),
        in_specs=[slab4(OB_X), slab(OB_B), slab(OB_C), slab4(OB_Z),
                  pl.BlockSpec((None, seq, LANES), lambda b, g: (OB_DT, b, 0)),
                  cw(SSD_CONV, gw, xo), cw(SSD_CONV, LANES, bo), cw(SSD_CONV, LANES, co),
                  cw(1, gw, xo), cw(1, LANES, bo), cw(1, LANES, co),
                  vec, vec, pl.BlockSpec((1, gw), lambda b, g: (0, g)),
                  pl.BlockSpec((None, LANES, LANES), lambda b, g: (g, 0, 0)), full(selw), full(expf), full(expb)]
        + ([st_in] if has_init else []) + ([pl.BlockSpec(memory_space=pl.ANY)] if has_prev else []),
        out_specs=[pl.BlockSpec((seq, gw), lambda b, g: (b, g)), st_out],
        out_shape=[jax.ShapeDtypeStruct((nb * seq, D_INNER), F32), st_shape],
        input_output_aliases={n_in: 1} if has_prev else {},
        scratch_shapes=[vm(nslab + 2, seq + 2 * SSD_HALO, LANES), vm(seq, gw), vm(seq, LANES), vm(seq, LANES),
                        vm(seq, LANES), vm(seq, LANES), vm(seq, LANES), vm(2, N_C, gw), vm(2, SSD_CHUNK, 2 * gw),
                        vm(2, SSD_CHUNK, SSD_REP * LANES), vm(2, 2, LANES, SSD_CHUNK), vm(2, SSD_CHUNK, SSD_CHUNK)],
        compiler_params=_cparams(("parallel", "parallel")),
        name="ssd",
    )(proj, proj, proj, proj, proj, conv_w, conv_w, conv_w, conv_b, conv_b, conv_b, dtb, alog, dsk,
      rep, selw, expf, expb, *((s0,) if has_init else ()), *((prev_states,) if has_prev else ()))


def _ssd_constants():
    gw = R_C * P_C
    rep = np.zeros((G_C, LANES, LANES), np.float32)
    for g in range(G_C):
        for l in range(LANES):
            c = l % SSD_REP
            rep[g, (R_C * g + c) if c < R_C else (H_C + R_C * g + c - R_C), l] = 1.0
    selw = np.zeros((LANES, SSD_REP * LANES), np.float32)
    expf = np.zeros((LANES, 2 * gw), np.float32)
    expb = np.zeros((LANES, 2 * gw), np.float32)
    for l in range(3 * SSD_REP):
        c = l % SSD_REP
        selw[l, c * LANES:(c + 1) * LANES] = 1.0
    for l in range(6 * SSD_REP):
        c, q = l % SSD_REP, l // (3 * SSD_REP)
        if c < R_C:
            expf[l, q * gw + c * P_C:q * gw + (c + 1) * P_C] = 1.0
        else:
            expb[l, q * gw + (c - R_C) * P_C:q * gw + (c - R_C + 1) * P_C] = 1.0
    return tuple(jnp.asarray(a, BF16) for a in (rep, selw, expf, expb))


def _pair_layout_c(c):
    nb = c.shape[0]
    c = c.transpose(0, 2, 1, 4, 3).reshape(nb, H_A // 2, 2, 2, DK_A, DV_A)
    return c.transpose(0, 1, 3, 2, 4, 5).reshape(nb, H_A // 2, 2, 2 * DK_A, DV_A)


def _pair_layout_c_inv(c):
    nb = c.shape[0]
    c = c.reshape(nb, H_A // 2, 2, 2, DK_A, DV_A).transpose(0, 2, 1, 3, 5, 4)
    return c.reshape(nb, 2, H_A, DV_A, DK_A)


def _pair_layout_n(n):
    nb = n.shape[0]
    return n.reshape(nb, 2, H_A // 2, 2 * DK_A).transpose(0, 2, 1, 3).reshape(nb, H_A // 2, 2, 1, 2 * DK_A)


def _pair_layout_n_inv(n):
    nb = n.shape[0]
    return n.reshape(nb, H_A // 2, 2, 2 * DK_A).transpose(0, 2, 1, 3).reshape(nb, 2, H_A, DK_A)


def _pair_layout_m(m):
    return _pair_layout_n(jnp.repeat(m[..., None], DK_A, axis=-1))


def _pair_layout_m_inv(m):
    return _pair_layout_n_inv(m)[..., 0]


def _group_replicated(v):
    t = v.reshape(2, G_C, R_C).transpose(1, 0, 2).reshape(G_C, 1, SSD_REP)
    return jnp.tile(t, (1, 1, LANES // SSD_REP))


def kernel(x_prompt, x_sample, cache_attn_k, cache_attn_v, state_mlstm_c, state_mlstm_n, state_mlstm_m, state_ssd,
           c, c_ctx, w_ada, b_ada, norm_mix, norm_ffn, w_gate, w_up, w_down, w_in_even, b_gate_mlstm, mlstm_norm,
           q_norm, k_norm, lambda_q1, lambda_k1, lambda_q2, lambda_k2, diff_norm, w_out_even, w_in_odd, conv_w,
           conv_b, dt_bias, a_log, d_skip, ssd_norm, w_out_odd):
    nb_c, seq_c, d = x_prompt.shape
    nb_d, seq_d, _ = x_sample.shape
    n_ctx, n_dec = nb_c * seq_c, nb_d * seq_d
    depth = w_ada.shape[0]

    xc, xd = x_prompt.reshape(n_ctx, d), x_sample.reshape(n_dec, d)
    cvecs = jnp.concatenate([c_ctx[None, :], c, jnp.zeros((8 - 1 - nb_d, d), F32)], axis=0)
    mods = ada_modulation(cvecs, w_ada, b_ada)

    def mod(l, k):
        return mods[l, :, k * d:(k + 1) * d].reshape(8, 1, d)

    cos, sin = rope_tables(seq_d)
    n_even, n_odd = w_in_even.shape[0], w_in_odd.shape[0]
    gate0 = 2 * A_QK + 2 * A_V
    w_even = jnp.concatenate([w_in_even[:, :, :gate0], w_in_even[:, :, gate0 + 4 * H_A:],
                              w_in_even[:, :, gate0:gate0 + 4 * H_A]], axis=2)
    w_even = jnp.pad(w_even, ((0, 0), (0, 0), (0, E_COLS - w_even.shape[2]))).astype(BF16)
    w_odd = jnp.pad(w_in_odd, ((0, 0), (0, 0), (0, O_COLS - w_in_odd.shape[2]))).astype(BF16)
    wo_even, wo_odd = w_out_even.astype(BF16), w_out_odd.astype(BF16)
    ffn_w = (w_gate.astype(BF16), w_up.astype(BF16), w_down.astype(BF16))

    new_c, new_n, new_m = [], [], []
    new_kv, new_s = None, None
    for l in range(depth):
        i = l // 2
        sh1, sc1, g1, sh2, sc2, g2 = (mod(l, k) for k in range(6))
        if l % 2 == 0:
            gmix = norm_mix[l][None]
            pc = in_projection(xc, gmix, sc1, sh1, w_even, i, E_TN, None)
            pd = in_projection(xd, gmix, sc1, sh1, w_even, i, E_TN, seq_d)
            bias = jnp.pad(b_gate_mlstm[i][None], ((0, 0), (0, LANES - 4 * H_A)))
            gain_a = mlstm_norm[i][None]
            zc = jnp.zeros((nb_c, H_A // 2, 2, LANES, LANES), F32)
            zv = jnp.zeros((nb_c, H_A // 2, 2, 1, LANES), F32)
            h_c, cf, nf, mf = mlstm(pc, bias, gain_a, zc, zv, zv, 0, nb_c, seq_c)
            h_d, _, _, _ = mlstm(pd, bias, gain_a, _pair_layout_c(state_mlstm_c[:, i]),
                                 _pair_layout_n(state_mlstm_n[:, i]), _pair_layout_m(state_mlstm_m[:, i]),
                                 0, nb_d, seq_d)
            new_c.append(_pair_layout_c_inv(cf))
            new_n.append(_pair_layout_n_inv(nf[:, :, :, 0]))
            new_m.append(_pair_layout_m_inv(mf[:, :, :, 0]))

            lam_init = 0.8 - 0.6 * math.exp(-0.3 * l)
            lam_p = jnp.stack([lambda_q1[i], lambda_k1[i], lambda_q2[i], lambda_k2[i]], axis=0)
            qg = jnp.tile(q_norm[i][None], (1, 2))
            kg = jnp.tile(k_norm[i][None], (1, 2))
            dg = diff_norm[i][None]
            o_c, *new_kv = diff_attention_ctx(pc, lam_p, qg, kg, dg, i, n_even, new_kv, nb_c, seq_c, lam_init)
            past = cache_attn_k.shape[2]
            o_d = diff_attention_dec(pd, cache_attn_k[:, i].reshape(nb_d, past, B_QK),
                                     cache_attn_v[:, i].reshape(nb_d, past, B_V), lam_p, qg, kg, dg, cos, sin,
                                     0, nb_d, seq_d, lam_init)
            xc = out_projection_even(xc, g1, h_c, o_c, wo_even, i, None)
            xd = out_projection_even(xd, g1, h_d, o_d, wo_even, i, seq_d)
        else:
            gmix = norm_mix[l][None]
            pc = in_projection(xc, gmix, sc1, sh1, w_odd, i, O_TN, None)
            pd = in_projection(xd, gmix, sc1, sh1, w_odd, i, O_TN, seq_d)
            dtb, alog = _group_replicated(dt_bias[i]), _group_replicated(a_log[i])
            dsk = jnp.repeat(d_skip[i], P_C)[None]
            cw, cb = conv_w[i], conv_b[i][None]
            y_c, new_s = ssd(pc, cw, cb, dtb, alog, dsk, None, i, n_odd, i, new_s, nb_c, seq_c)
            y_d, _ = ssd(pd, cw, cb, dtb, alog, dsk, state_ssd.reshape(nb_d, n_odd, 2, H_C * P_C, N_C), i, 1, 0, None,
                         nb_d, seq_d)
            gs = ssd_norm[i][None]
            xc = out_projection_odd(xc, g1, y_c, gs, wo_odd, i, None)
            xd = out_projection_odd(xd, g1, y_d, gs, wo_odd, i, seq_d)
        xc = ffn(xc, norm_ffn[l][None], sc2, sh2, g2, *ffn_w, l, None)
        xd = ffn(xd, norm_ffn[l][None], sc2, sh2, g2, *ffn_w, l, seq_d)

    y_prompt = xc.reshape(nb_c, seq_c, d)
    y_sample = xd.reshape(nb_d, seq_d, d)
    new_k = new_kv[0].reshape(nb_c, n_even, seq_c, H_B, 2, DB)
    new_v = new_kv[1].reshape(nb_c, n_even, seq_c, H_B, DVB)
    return (y_prompt, y_sample, new_k, new_v, jnp.stack(new_c, axis=1), jnp.stack(new_n, axis=1),
            jnp.stack(new_m, axis=1), new_s.reshape(nb_c, n_odd, 2, H_C, P_C, N_C))
```

```python
import functools
import itertools
import math

import jax
import jax.numpy as jnp
import numpy as np
from jax import lax
from jax.experimental import pallas as pl
from jax.experimental.pallas import tpu as pltpu

F32 = jnp.float32
BF16 = jnp.bfloat16

D_MODEL = 2048
DEPTH = 4
GRID_W = 64
RMS_EPS = 1e-6
H_A, DK_A, DV_A, MLSTM_CHUNK = 8, 64, 128, 64
H_B, DB, DVB = 8, 64, 128
ROPE_THETA = 10000.0
ROPE_PAIRS = DB // 4
D_INNER = 2 * D_MODEL
P_C, N_C, G_C = 64, 128, 8
H_C = D_INNER // P_C
R_C = H_C // G_C
SSD_CONV = 4
SSD_CHUNK = 128
D_FF = -(-8 * D_MODEL // (3 * 256)) * 256
A_QK, A_V = H_A * DK_A, H_A * DV_A
B_QK, B_V = H_B * 2 * DB, H_B * DVB
CONV_CH = D_INNER + 2 * G_C * N_C

LANES = 128
VMEM_LIMIT = 56 * 1024 * 1024

TM = 512
INP_TM = 1024
FFN_TM = 512
ADA_TN = 1024
E_COLS = 6400
E_TN = 1280
O_COLS = 10752
O_TN = 1536
FF_TN = 512
OUT_TN = 1024
ATT_TQ = 256
ATT_CTX_LOCKSTEP = 4

EB_Q, EB_K, EB_V, EB_O = 0, 4, 8, 16
EB_BQ, EB_BK, EB_BV, EB_G = 24, 32, 40, 48
OB_Z, OB_X, OB_B, OB_C, OB_DT = 0, 32, 64, 72, 80


def _cparams(sem):
    return pltpu.CompilerParams(dimension_semantics=sem, vmem_limit_bytes=VMEM_LIMIT)


def _silu(x):
    return x * jax.nn.sigmoid(x)


def _split3(x):
    hi = x.astype(BF16)
    r1 = x - hi.astype(F32)
    mid = r1.astype(BF16)
    lo = (r1 - mid.astype(F32)).astype(BF16)
    return hi, mid, lo


def _dot(a, b):
    return jnp.dot(a, b, preferred_element_type=F32)


def _dot3_l(x, m):
    hi, mid, lo = _split3(x)
    return _dot(hi, m) + _dot(mid, m) + _dot(lo, m)


def _dot3_r(m, x):
    hi, mid, lo = _split3(x)
    return _dot(m, hi) + _dot(m, mid) + _dot(m, lo)


def _row_group(i, tm, mod_rows):
    return 0 if mod_rows is None else 1 + (i * tm) // mod_rows


def _ada_kernel(c_ref, w_ref, b_ref, o_ref):
    s = _silu(c_ref[...]).astype(BF16)
    o_ref[...] = _dot(s, w_ref[...].astype(BF16)) + b_ref[...]


def ada_modulation(cvecs, w_ada, b_ada):
    depth, d, n = w_ada.shape
    return pl.pallas_call(
        _ada_kernel,
        grid=(depth, n // ADA_TN),
        in_specs=[
            pl.BlockSpec((8, d), lambda l, j: (0, 0)),
            pl.BlockSpec((None, d, ADA_TN), lambda l, j: (l, 0, j)),
            pl.BlockSpec((None, 1, ADA_TN), lambda l, j: (l, 0, j)),
        ],
        out_specs=pl.BlockSpec((None, 8, ADA_TN), lambda l, j: (l, 0, j)),
        out_shape=jax.ShapeDtypeStruct((depth, 8, n), F32),
        compiler_params=_cparams(("parallel", "parallel")),
        name="ada_modulation",
    )(cvecs, w_ada, b_ada.reshape(depth, 1, n))


def _norm_mod(x, gain, sc, sh):
    ms = jnp.mean(x * x, axis=-1, keepdims=True)
    y = x * lax.rsqrt(ms + RMS_EPS) * gain
    return y * (1.0 + sc) + sh


def _inproj_kernel(x_ref, gain_ref, sc_ref, sh_ref, w_ref, o_ref, h_ref, *, nblk):
    @pl.when(pl.program_id(1) == 0)
    def _():
        h_ref[...] = _norm_mod(x_ref[...], gain_ref[...], sc_ref[...], sh_ref[...]).astype(BF16)

    acc = _dot(h_ref[...], w_ref[...])
    for k in range(nblk):
        o_ref[k] = acc[:, k * LANES:(k + 1) * LANES]


def in_projection(x, gain, sc, sh, w, layer, tn, mod_rows):
    m, d = x.shape
    n = w.shape[2]
    nblk = tn // LANES
    tm = INP_TM
    grp = lambda i, j: (_row_group(i, tm, mod_rows), 0, 0)
    return pl.pallas_call(
        functools.partial(_inproj_kernel, nblk=nblk),
        grid=(m // tm, n // tn),
        in_specs=[
            pl.BlockSpec((tm, d), lambda i, j: (i, 0)),
            pl.BlockSpec((1, d), lambda i, j: (0, 0)),
            pl.BlockSpec((None, 1, d), grp),
            pl.BlockSpec((None, 1, d), grp),
            pl.BlockSpec((None, d, tn), lambda i, j: (layer, 0, j)),
        ],
        out_specs=pl.BlockSpec((nblk, tm, LANES), lambda i, j: (j, i, 0)),
        out_shape=jax.ShapeDtypeStruct((n // LANES, m, LANES), F32),
        scratch_shapes=[pltpu.VMEM((tm, d), BF16)],
        compiler_params=_cparams(("parallel", "arbitrary")),
        name="in_projection",
    )(x, gain, sc, sh, w)


def _ffn_kernel(x_ref, gain_ref, sc_ref, sh_ref, g_ref, wg_ref, wu_ref, wd_ref, o_ref, h_ref, acc_ref):
    f = pl.program_id(1)

    @pl.when(f == 0)
    def _():
        h_ref[...] = _norm_mod(x_ref[...], gain_ref[...], sc_ref[...], sh_ref[...]).astype(BF16)
        acc_ref[...] = jnp.zeros_like(acc_ref)

    h = h_ref[...]
    gate = _dot(h, wg_ref[...])
    up = _dot(h, wu_ref[...])
    ff = (_silu(gate) * up).astype(BF16)
    acc_ref[...] += _dot(ff, wd_ref[...])

    @pl.when(f == pl.num_programs(1) - 1)
    def _():
        o_ref[...] = x_ref[...] + g_ref[...] * acc_ref[...]


def ffn(x, gain, sc, sh, g, wg, wu, wd, layer, mod_rows):
    m, d = x.shape
    dff = wg.shape[2]
    tm = FFN_TM
    grp = lambda i, f: (_row_group(i, tm, mod_rows), 0, 0)
    return pl.pallas_call(
        _ffn_kernel,
        grid=(m // tm, dff // FF_TN),
        in_specs=[
            pl.BlockSpec((tm, d), lambda i, f: (i, 0)),
            pl.BlockSpec((1, d), lambda i, f: (0, 0)),
            pl.BlockSpec((None, 1, d), grp),
            pl.BlockSpec((None, 1, d), grp),
            pl.BlockSpec((None, 1, d), grp),
            pl.BlockSpec((None, d, FF_TN), lambda i, f: (layer, 0, f)),
            pl.BlockSpec((None, d, FF_TN), lambda i, f: (layer, 0, f)),
            pl.BlockSpec((None, FF_TN, d), lambda i, f: (layer, f, 0)),
        ],
        out_specs=pl.BlockSpec((tm, d), lambda i, f: (i, 0)),
        out_shape=jax.ShapeDtypeStruct((m, d), F32),
        scratch_shapes=[pltpu.VMEM((tm, d), BF16), pltpu.VMEM((tm, d), F32)],
        compiler_params=_cparams(("parallel", "arbitrary")),
        name="ffn",
    )(x, gain, sc, sh, g, wg, wu, wd)


def _outproj_even_kernel(x_ref, g_ref, a_ref, b_ref, wa_ref, wb_ref, o_ref):
    acc = _dot(a_ref[...], wa_ref[...]) + _dot(b_ref[...], wb_ref[...])
    o_ref[...] = x_ref[...] + g_ref[...] * acc


def out_projection_even(x, g, ha, ob, w, layer, mod_rows):
    m, d = x.shape
    ka, kb = ha.shape[1], ob.shape[1]
    grp = lambda i: (_row_group(i, TM, mod_rows), 0, 0)
    return pl.pallas_call(
        _outproj_even_kernel,
        grid=(m // TM,),
        in_specs=[
            pl.BlockSpec((TM, d), lambda i: (i, 0)),
            pl.BlockSpec((None, 1, d), grp),
            pl.BlockSpec((TM, ka), lambda i: (i, 0)),
            pl.BlockSpec((TM, kb), lambda i: (i, 0)),
            pl.BlockSpec((None, ka, d), lambda i: (layer, 0, 0)),
            pl.BlockSpec((None, kb, d), lambda i: (layer, ka // kb, 0)),
        ],
        out_specs=pl.BlockSpec((TM, d), lambda i: (i, 0)),
        out_shape=jax.ShapeDtypeStruct((m, d), F32),
        compiler_params=_cparams(("parallel",)),
        name="out_projection_even",
    )(x, g, ha, ob, w, w)


def _outproj_odd_kernel(x_ref, g_ref, y_ref, gain_ref, w_ref, o_ref, yn_ref):
    @pl.when(pl.program_id(1) == 0)
    def _():
        y = y_ref[...]
        ms = jnp.mean(y * y, axis=-1, keepdims=True)
        yn_ref[...] = (y * lax.rsqrt(ms + RMS_EPS) * gain_ref[...]).astype(BF16)

    o_ref[...] = x_ref[...] + g_ref[...] * _dot(yn_ref[...], w_ref[...])


def out_projection_odd(x, g, y, gain, w, layer, mod_rows):
    m, d = x.shape
    k = y.shape[1]
    grp = lambda i, j: (_row_group(i, TM, mod_rows), 0, j)
    return pl.pallas_call(
        _outproj_odd_kernel,
        grid=(m // TM, d // OUT_TN),
        in_specs=[
            pl.BlockSpec((TM, OUT_TN), lambda i, j: (i, j)),
            pl.BlockSpec((None, 1, OUT_TN), grp),
            pl.BlockSpec((TM, k), lambda i, j: (i, 0)),
            pl.BlockSpec((1, k), lambda i, j: (0, 0)),
            pl.BlockSpec((None, k, OUT_TN), lambda i, j: (layer, 0, j)),
        ],
        out_specs=pl.BlockSpec((TM, OUT_TN), lambda i, j: (i, j)),
        out_shape=jax.ShapeDtypeStruct((m, d), F32),
        scratch_shapes=[pltpu.VMEM((TM, k), BF16)],
        compiler_params=_cparams(("parallel", "arbitrary")),
        name="out_projection_odd",
    )(x, g, y, gain, w)


def _lane_iota(shape):
    return lax.broadcasted_iota(jnp.int32, shape, len(shape) - 1)


def _half_rms_norm(x, gain):
    lo = _lane_iota(x.shape) < DB
    x2 = x * x
    s0 = jnp.sum(jnp.where(lo, x2, 0.0), axis=-1, keepdims=True)
    s1 = jnp.sum(jnp.where(lo, 0.0, x2), axis=-1, keepdims=True)
    r = jnp.where(lo, lax.rsqrt(s0 * (1.0 / DB) + RMS_EPS), lax.rsqrt(s1 * (1.0 / DB) + RMS_EPS))
    return x * r * gain


def _rope(x, cos, sin_signed):
    first = (_lane_iota(x.shape) % 32) < ROPE_PAIRS
    partner = jnp.where(first, pltpu.roll(x, LANES - ROPE_PAIRS, 1), pltpu.roll(x, ROPE_PAIRS, 1))
    return x * cos + partner * sin_signed


def _diff_attn_core(q, kk, vv, lam, dgain, lam_init):
    lo = _lane_iota(q.shape) < DB
    qs = q * (DB ** -0.5)
    nt = (((1,), (1,)), ((), ()))
    s = [lax.dot_general(jnp.where(lo, qs, 0.0).astype(BF16), kk, nt, preferred_element_type=F32),
         lax.dot_general(jnp.where(lo, 0.0, qs).astype(BF16), kk, nt, preferred_element_type=F32)]
    p = [jnp.exp(si - jnp.max(si, axis=-1, keepdims=True)) for si in s]
    pv = [_dot(pi.astype(BF16), vv) * (1.0 / jnp.sum(pi, axis=-1, keepdims=True)) for pi in p]
    o = pv[0] - lam * pv[1]
    ms = jnp.mean(o * o, axis=-1, keepdims=True)
    return o * lax.rsqrt(ms + RMS_EPS) * dgain * (1.0 - lam_init)


def _lambda_value(lam_ref, lam_init):
    lp = lam_ref[...]
    a = jnp.sum(lp[0:1] * lp[1:2], axis=-1, keepdims=True)
    b = jnp.sum(lp[2:3] * lp[3:4], axis=-1, keepdims=True)
    return jnp.exp(a) - jnp.exp(b) + lam_init


def _attn_ctx_kernel(lam_ref, q_ref, k_ref, v_ref, qg_ref, kg_ref, dg_ref, *rest, lam_init, plane):
    o_ref, kd_ref, vd_ref = rest[-3:]
    for p in range(kd_ref.shape[0]):
        if p != plane:
            kd_ref[p] = jnp.zeros(kd_ref.shape[1:], F32)
            vd_ref[p] = jnp.zeros(vd_ref.shape[1:], F32)
    lam = _lambda_value(lam_ref, lam_init)
    nt = (((1,), (1,)), ((), ()))

    def head(h):
        cs = slice(h * LANES, (h + 1) * LANES)
        kd = _half_rms_norm(k_ref[h], kg_ref[...])
        kd_ref[plane, :, cs] = kd
        v = v_ref[h]
        vd_ref[plane, :, cs] = v
        q = _half_rms_norm(q_ref[h], qg_ref[...]) * (DB ** -0.5)
        yield
        lo = _lane_iota(q.shape) < DB
        kk, vv = kd.astype(BF16), v.astype(BF16)
        s = [lax.dot_general(jnp.where(lo, q, 0.0).astype(BF16), kk, nt, preferred_element_type=F32),
             lax.dot_general(jnp.where(lo, 0.0, q).astype(BF16), kk, nt, preferred_element_type=F32)]
        yield
        p = [jnp.exp(si - jnp.max(si, axis=-1, keepdims=True)) for si in s]
        yield
        pv = [_dot(pi.astype(BF16), vv) * (1.0 / jnp.sum(pi, axis=-1, keepdims=True)) for pi in p]
        yield
        o = pv[0] - lam * pv[1]
        ms = jnp.mean(o * o, axis=-1, keepdims=True)
        o_ref[:, cs] = (o * lax.rsqrt(ms + RMS_EPS) * dg_ref[...] * (1.0 - lam_init)).astype(BF16)
        yield

    for h0 in range(0, H_B, ATT_CTX_LOCKSTEP):
        for _ in zip(*[head(h) for h in range(h0, h0 + ATT_CTX_LOCKSTEP)]):
            pass


def diff_attention_ctx(proj, lam_p, qg, kg, dg, layer, n_layers, prev_kv, nb, seq, lam_init):
    m = nb * seq
    vec = pl.BlockSpec((1, LANES), lambda b: (0, 0))
    slab = lambda base: pl.BlockSpec((H_B, seq, LANES), lambda b: (base // H_B, b, 0))
    outb = pl.BlockSpec((seq, H_B * LANES), lambda b: (b, 0))
    prev = () if prev_kv is None else tuple(prev_kv)
    n_planes, plane = (1, 0) if prev else (n_layers, layer)
    kvb = pl.BlockSpec((None, n_planes, seq, H_B * LANES), lambda b: (b, layer if prev else 0, 0, 0))
    kv_shape = jax.ShapeDtypeStruct((nb, n_layers, seq, H_B * LANES), F32)
    return pl.pallas_call(
        functools.partial(_attn_ctx_kernel, lam_init=lam_init, plane=plane),
        grid=(nb,),
        in_specs=[pl.BlockSpec((4, DB), lambda b: (0, 0)), slab(EB_BQ), slab(EB_BK), slab(EB_BV), vec, vec, vec]
        + [pl.BlockSpec(memory_space=pl.ANY)] * len(prev),
        out_specs=[outb, kvb, kvb],
        out_shape=[jax.ShapeDtypeStruct((m, B_V), BF16), kv_shape, kv_shape],
        input_output_aliases={7: 1, 8: 2} if prev else {},
        compiler_params=_cparams(("parallel",)),
        name="diff_attention_ctx",
    )(lam_p, proj, proj, proj, qg, kg, dg, *prev)


def _attn_dec_kernel(lam_ref, q_ref, k_ref, v_ref, kc_ref, vc_ref, qg_ref, kg_ref, dg_ref, cq_ref, sq_ref, ck_ref,
                     sk_ref, o_ref, ks_ref, vs_ref, *, lam_init, past):
    @pl.when(pl.program_id(2) == 0)
    def _():
        ks_ref[0:past, :] = kc_ref[...].astype(BF16)
        vs_ref[0:past, :] = vc_ref[...].astype(BF16)
        kd = _rope(_half_rms_norm(k_ref[...], kg_ref[...]), ck_ref[...], sk_ref[...])
        ks_ref[past:, :] = kd.astype(BF16)
        vs_ref[past:, :] = v_ref[...].astype(BF16)

    q = _rope(_half_rms_norm(q_ref[...], qg_ref[...]), cq_ref[...], sq_ref[...])
    lam = _lambda_value(lam_ref, lam_init)
    o_ref[...] = _diff_attn_core(q, ks_ref[...], vs_ref[...], lam, dg_ref[...], lam_init).astype(BF16)


def diff_attention_dec(proj, cache_k, cache_v, lam_p, qg, kg, dg, cos, sin, row0, nb, seq, lam_init):
    past = cache_k.shape[1]
    nq = seq // ATT_TQ
    rq, rk = row0 // ATT_TQ, row0 // seq
    vec = pl.BlockSpec((1, LANES), lambda b, h, i: (0, 0))
    kslab = lambda base: pl.BlockSpec((None, seq, LANES), lambda b, h, i: (base + h, rk + b, 0))
    cslab = pl.BlockSpec((None, past, LANES), lambda b, h, i: (b, 0, h))
    tq = pl.BlockSpec((ATT_TQ, LANES), lambda b, h, i: (i, 0))
    tk = pl.BlockSpec((seq, LANES), lambda b, h, i: (0, 0))
    return pl.pallas_call(
        functools.partial(_attn_dec_kernel, lam_init=lam_init, past=past),
        grid=(nb, H_B, nq),
        in_specs=[pl.BlockSpec((4, DB), lambda b, h, i: (0, 0)),
                  pl.BlockSpec((None, ATT_TQ, LANES), lambda b, h, i: (EB_BQ + h, rq + b * nq + i, 0)),
                  kslab(EB_BK), kslab(EB_BV), cslab, cslab, vec, vec, vec, tq, tq, tk, tk],
        out_specs=pl.BlockSpec((ATT_TQ, LANES), lambda b, h, i: (b * nq + i, h)),
        out_shape=jax.ShapeDtypeStruct((nb * seq, B_V), BF16),
        scratch_shapes=[pltpu.VMEM((past + seq, LANES), BF16), pltpu.VMEM((past + seq, LANES), BF16)],
        compiler_params=_cparams(("parallel", "parallel", "arbitrary")),
        name="diff_attention_dec",
    )(lam_p, proj, proj, proj, cache_k, cache_v, qg, kg, dg, cos, sin, cos, sin)


def rope_tables(n_tokens):
    n_rows = n_tokens // GRID_W
    rows, cols = jnp.meshgrid(jnp.arange(n_rows, dtype=F32), jnp.arange(GRID_W, dtype=F32), indexing='ij')
    inv_freq = ROPE_THETA ** (-jnp.arange(ROPE_PAIRS, dtype=F32) / ROPE_PAIRS)
    ang = jnp.stack([rows.reshape(-1, 1) * inv_freq, cols.reshape(-1, 1) * inv_freq], axis=0)
    cos, sin = jnp.cos(ang), jnp.sin(ang)
    cos_half = jnp.concatenate([cos[0], cos[0], cos[1], cos[1]], axis=-1)
    sin_half = jnp.concatenate([-sin[0], sin[0], -sin[1], sin[1]], axis=-1)
    return jnp.tile(cos_half, (1, 2)), jnp.tile(sin_half, (1, 2))


def _tri(n, lower):
    r = lax.broadcasted_iota(jnp.int32, (n, n), 0)
    c = lax.broadcasted_iota(jnp.int32, (n, n), 1)
    return jnp.where((r >= c) if lower else (r <= c), 1.0, 0.0).astype(BF16)


def _mlstm_stage_a(d, rows, slot, q_ref, k_ref, v_ref, gcol, grow, tri_l, tri_u, ia_ref, ba_ref, mi_ref, ua_ref,
                   sc_ref):
    lc = MLSTM_CHUNK
    ri = lax.broadcasted_iota(jnp.int32, (lc, lc), 0)
    ci = lax.broadcasted_iota(jnp.int32, (lc, lc), 1)
    mask = (ci <= ri) if d == 0 else (ci >= ri)
    lo = _lane_iota((lc, LANES)) < DK_A
    nt = (((1,), (1,)), ((), ()))
    bcol_all = _dot3_r(tri_l if d == 0 else tri_u, jax.nn.log_sigmoid(gcol))
    brow_all = _dot3_l(jax.nn.log_sigmoid(grow), tri_u if d == 0 else tri_l)
    yield
    q = q_ref[rows, :]
    k = k_ref[rows, :] * (DK_A ** -0.5)
    kb = k.astype(BF16)
    ones = jnp.ones((lc, LANES), BF16)
    vs = [v_ref[j, rows, :].astype(BF16) for j in range(2)]
    heads = (0, 1)
    qk = [lax.dot_general((jnp.where(lo, q, 0.0) if j == 0 else jnp.where(lo, 0.0, q)).astype(BF16), kb, nt,
                          preferred_element_type=F32) for j in heads]
    b_all = [jnp.broadcast_to(bcol_all[:, 16 + 8 * d + j:16 + 8 * d + j + 1], (lc, LANES)) for j in heads]
    li_all = [jnp.broadcast_to(gcol[:, 8 * d + j:8 * d + j + 1], (lc, LANES)) for j in heads]
    yield
    dmat = [jnp.where(mask, b_all[j][:, :lc] - brow_all[16 + 8 * d + j:16 + 8 * d + j + 1, :]
                      + grow[8 * d + j:8 * d + j + 1, :], -jnp.inf) for j in heads]
    m_intra = [jnp.max(dmat[j], axis=-1, keepdims=True) for j in heads]
    b_end = [b_all[j][lc - 1:lc, :] if d == 0 else b_all[j][0:1, :] for j in heads]
    g_all = [b_end[j] - b_all[j] + li_all[j] for j in heads]
    g_max = [jnp.max(g_all[j], axis=0, keepdims=True) for j in heads]
    yield
    s = [(qk[j] * jnp.exp(dmat[j] - m_intra[j])).astype(BF16) for j in heads]
    wkb = (k * jnp.where(lo, jnp.exp(g_all[0] - g_max[0]), jnp.exp(g_all[1] - g_max[1]))).astype(BF16)
    yield
    for j in heads:
        ia_ref[slot, d, j] = _dot(s[j], jnp.concatenate([vs[j], ones], axis=1))
        ba_ref[slot, d, j] = b_all[j]
        mi_ref[slot, d, j] = jnp.broadcast_to(m_intra[j], (lc, LANES))
        sc_ref[slot, d, j, 0:1, :] = g_max[j]
        sc_ref[slot, d, j, 1:2, :] = b_end[j]
    tn = (((0,), (0,)), ((), ()))
    ua_ref[slot, d] = lax.dot_general(wkb, jnp.concatenate(vs + [ones], axis=1), tn, preferred_element_type=F32)
    yield


def _mlstm_stage_b(d, rows, slot, q_ref, m_prev, cn_ref, h_out, m_out, ia_ref, ba_ref, mi_ref, ua_ref, sc_ref):
    lc = MLSTM_CHUNK
    lo = _lane_iota((lc, LANES)) < DK_A
    heads = (0, 1)
    q = q_ref[rows, :]
    cn = cn_ref[d]
    cnb = cn.astype(BF16)
    inter_nd = [_dot((jnp.where(lo, q, 0.0) if j == 0 else jnp.where(lo, 0.0, q)).astype(BF16), cnb) for j in heads]
    yield
    inter = [ba_ref[slot, d, j] + m_prev[j] for j in heads]
    m_row = [jnp.maximum(inter[j], mi_ref[slot, d, j]) for j in heads]
    w_inter = [jnp.exp(inter[j] - m_row[j]) for j in heads]
    w_intra = [jnp.exp(mi_ref[slot, d, j] - m_row[j]) for j in heads]
    g_max = [sc_ref[slot, d, j, 0:1, :] for j in heads]
    b_end = [sc_ref[slot, d, j, 1:2, :] for j in heads]
    m_new = [jnp.maximum(b_end[j] + m_prev[j], g_max[j]) for j in heads]
    decay = [jnp.exp(b_end[j] + m_prev[j] - m_new[j]) for j in heads]
    fac = [jnp.exp(g_max[j] - m_new[j]) for j in heads]
    m_out.extend(m_new)
    yield
    u = ua_ref[slot, d]
    top = lax.broadcasted_iota(jnp.int32, (LANES, LANES), 0) < DK_A
    dec = jnp.where(top, decay[0], decay[1])
    f = jnp.where(top, fac[0], fac[1])
    c_new = dec * cn[:, :LANES] + f * jnp.where(top, u[:, :LANES], u[:, LANES:2 * LANES])
    n_new = dec * cn[:, LANES:] + f * u[:, 2 * LANES:]
    cn_ref[d] = jnp.concatenate([c_new, n_new], axis=1)
    yield
    hs = []
    for j in heads:
        intra_nd = ia_ref[slot, d, j]
        num = w_inter[j] * inter_nd[j][:, :LANES] + w_intra[j] * intra_nd[:, :LANES]
        den = w_inter[j] * inter_nd[j][:, LANES:] + w_intra[j] * intra_nd[:, LANES:]
        hs.append(num * (1.0 / jnp.maximum(jnp.abs(den), jnp.exp(-m_row[j]))))
    h_out[rows, :] = jnp.concatenate(hs, axis=-1)
    yield


def _mlstm_kernel(q_ref, k_ref, v_ref, ao_ref, g_ref, bias_ref, gain_ref, c0_ref, n0_ref, m0_ref,
                  h_ref, cf_ref, nf_ref, mf_ref, gc_ref, gt_ref, hf_ref, hb_ref, cn_ref, ia_ref, ba_ref, mi_ref,
                  ua_ref, sc_ref, *, seq):
    lc = MLSTM_CHUNK
    nc = seq // lc
    pair = pl.program_id(1)
    shift = (LANES - 2 * pair) % LANES
    bias = pltpu.roll(jnp.broadcast_to(bias_ref[...], (8, LANES)), shift, 1)[0:1]
    for blk in range(seq // LANES):
        r = slice(blk * LANES, (blk + 1) * LANES)
        gs = pltpu.roll(g_ref[r, :], shift, 1) + bias
        gc_ref[r, :] = gs
        gst = gs.T
        for half in range(LANES // lc):
            gt_ref[blk * (LANES // lc) + half] = gst[0:32, half * lc:(half + 1) * lc]
    for d in range(2):
        cn_ref[d, :, 0:LANES] = c0_ref[d]
        cn_ref[d, :, LANES:] = jnp.broadcast_to(n0_ref[d], (LANES, LANES)).T
    tri_l, tri_u = _tri(lc, True), _tri(lc, False)
    stage = (ia_ref, ba_ref, mi_ref, ua_ref, sc_ref)

    def chunk_rows(c):
        return pl.ds(pl.multiple_of(c * lc, lc), lc)

    def stage_a(i, slot):
        gens = []
        for d, c in ((0, i), (1, nc - 1 - i)):
            rows = chunk_rows(c)
            gens.append(_mlstm_stage_a(d, rows, slot, q_ref, k_ref, v_ref, gc_ref[rows, :], gt_ref[c], tri_l, tri_u,
                                       *stage))
        return gens

    def stage_b(i, slot, ms, outs):
        return [_mlstm_stage_b(d, chunk_rows(c), slot, q_ref, ms[2 * d:2 * d + 2], cn_ref, h_out, outs[d], *stage)
                for d, c, h_out in ((0, i, hf_ref), (1, nc - 1 - i, hb_ref))]

    def run(gens):
        for _ in itertools.zip_longest(*gens):
            pass

    def body(kk, ms):
        outs = ([], [])
        run(stage_b(2 * kk, 0, ms, outs) + stage_a(2 * kk + 1, 1))
        ms = tuple(outs[0] + outs[1])
        outs = ([], [])
        run(stage_b(2 * kk + 1, 1, ms, outs) + stage_a(jnp.minimum(2 * kk + 2, nc - 1), 0))
        return tuple(outs[0] + outs[1])

    m0 = m0_ref[...]
    rep = lambda v: jnp.broadcast_to(v, (1, LANES))
    init = (rep(m0[0][:, 0:1]), rep(m0[0][:, DK_A:DK_A + 1]), rep(m0[1][:, 0:1]), rep(m0[1][:, DK_A:DK_A + 1]))
    run(stage_a(0, 0))
    m_f0, m_f1, m_b0, m_b1 = lax.fori_loop(0, nc // 2, body, init)
    lane1 = _lane_iota((1, LANES)) < DK_A
    mf_ref[0] = jnp.where(lane1, m_f0, m_f1)
    mf_ref[1] = jnp.where(lane1, m_b0, m_b1)
    for d in range(2):
        cf_ref[d] = cn_ref[d, :, 0:LANES]
        nf_ref[d] = cn_ref[d, :, LANES:].T[0:1, :]

    gain = gain_ref[...]
    for blk in range(seq // LANES):
        r = slice(blk * LANES, (blk + 1) * LANES)
        outs = []
        for j in range(2):
            cs = slice(j * DV_A, (j + 1) * DV_A)
            hh = hf_ref[r, cs] + hb_ref[r, cs]
            ms = jnp.mean(hh * hh, axis=-1, keepdims=True)
            outs.append(hh * lax.rsqrt(ms + RMS_EPS) * gain[:, cs] * jax.nn.sigmoid(ao_ref[j, r, :]))
        h_ref[r, :] = jnp.concatenate(outs, axis=-1).astype(BF16)


def mlstm(proj, bias, gain, c0, n0, m0, row0, nb, seq):
    rb = row0 // seq
    npair = H_A // 2
    slab = lambda base: pl.BlockSpec((None, seq, LANES), lambda b, p: (base + p, rb + b, 0))
    slab2 = lambda base: pl.BlockSpec((2, seq, LANES), lambda b, p: (base // 2 + p, rb + b, 0))
    st_c = pl.BlockSpec((None, None, 2, LANES, LANES), lambda b, p: (b, p, 0, 0, 0))
    st_v = pl.BlockSpec((None, None, 2, 1, LANES), lambda b, p: (b, p, 0, 0, 0))
    return pl.pallas_call(
        functools.partial(_mlstm_kernel, seq=seq),
        grid=(nb, npair),
        in_specs=[slab(EB_Q), slab(EB_K), slab2(EB_V), slab2(EB_O),
                  pl.BlockSpec((None, seq, LANES), lambda b, p: (EB_G, rb + b, 0)),
                  pl.BlockSpec((1, LANES), lambda b, p: (0, 0)),
                  pl.BlockSpec((1, 2 * DV_A), lambda b, p: (0, p)),
                  st_c, st_v, st_v],
        out_specs=[pl.BlockSpec((seq, 2 * DV_A), lambda b, p: (b, p)), st_c, st_v, st_v],
        out_shape=[jax.ShapeDtypeStruct((nb * seq, A_V), BF16),
                   jax.ShapeDtypeStruct((nb, npair, 2, LANES, LANES), F32),
                   jax.ShapeDtypeStruct((nb, npair, 2, 1, LANES), F32),
                   jax.ShapeDtypeStruct((nb, npair, 2, 1, LANES), F32)],
        scratch_shapes=[pltpu.VMEM((seq, LANES), F32), pltpu.VMEM((seq // MLSTM_CHUNK, 32, MLSTM_CHUNK), F32),
                        pltpu.VMEM((seq, 2 * DV_A), F32),
                        pltpu.VMEM((seq, 2 * DV_A), F32), pltpu.VMEM((2, LANES, 2 * LANES), F32),
                        pltpu.VMEM((2, 2, 2, MLSTM_CHUNK, 2 * LANES), F32), pltpu.VMEM((2, 2, 2, MLSTM_CHUNK, LANES), F32),
                        pltpu.VMEM((2, 2, 2, MLSTM_CHUNK, LANES), F32), pltpu.VMEM((2, 2, LANES, 3 * LANES), F32),
                        pltpu.VMEM((2, 2, 2, 8, LANES), F32)],
        compiler_params=_cparams(("parallel", "parallel")),
        name="mlstm",
    )(proj, proj, proj, proj, proj, bias, gain, c0, n0, m0)


def _softplus(x):
    return jnp.maximum(x, 0.0) + jnp.log1p(jnp.exp(-jnp.abs(x)))


SSD_REP = 2 * R_C


def _pack3(x, lane):
    hi = x.astype(BF16).astype(F32)
    r1 = x - hi
    mid = r1.astype(BF16).astype(F32)
    r2 = r1 - mid
    grp = lane % (3 * SSD_REP)
    return jnp.where(grp < SSD_REP, x, jnp.where(grp < 2 * SSD_REP, r1, r2)).astype(BF16)


SSD_HALO = 8


def _ssd_kernel(xs_ref, bm_ref, cm_ref, z_ref, dtr_ref, wx_ref, wb_ref, wc_ref, bx_ref, bb_ref, bc_ref,
                 dtb_ref, alog_ref, dsk_ref, rep_ref, selw_ref, expf_ref, expb_ref, *rest, seq, has_init, has_prev,
                 plane):
    if has_init:
        s0_ref = rest[0]
        rest = rest[1:]
    if has_prev:
        rest = rest[1:]
    (y_ref, sfin_ref, pad_ref, xa_ref, ba_ref, ca_ref, dt_ref, af_ref, ab_ref, st_ref, ex_ref, abw_ref,
     at_ref, cb_ref) = rest
    lc = SSD_CHUNK
    nc = seq // lc
    gw = R_C * P_C
    nslab = gw // LANES
    lane = _lane_iota((lc, LANES))
    lo = lane < P_C
    fwd_lane = (lane % SSD_REP) < R_C
    first3 = lane < 3 * SSD_REP
    a_neg = -jnp.exp(alog_ref[...])
    tri_l, tri_u = _tri(lc, True), _tri(lc, False)
    ri = lax.broadcasted_iota(jnp.int32, (lc, lc), 0)
    ci = lax.broadcasted_iota(jnp.int32, (lc, lc), 1)
    causal, anti = ci <= ri, ci >= ri
    nt = (((1,), (1,)), ((), ()))

    def chunk_rows(c):
        return pl.ds(pl.multiple_of(c * lc, lc), lc)

    halo = jnp.zeros((SSD_HALO, LANES), F32)
    for s in range(nslab + 2):
        pad_ref[s, 0:SSD_HALO, :] = halo
        pad_ref[s, SSD_HALO + seq:, :] = halo
        src = xs_ref[s] if s < nslab else (bm_ref[...] if s == nslab else cm_ref[...])
        pad_ref[s, SSD_HALO:SSD_HALO + seq, :] = src

    def conv_chunk(s, base, w, b):
        y = w[0:1] * pad_ref[s, pl.ds(base + SSD_HALO - 2, lc), :]
        y = y + w[1:2] * pad_ref[s, pl.ds(base + SSD_HALO - 1, lc), :]
        y = y + w[2:3] * pad_ref[s, pl.ds(base + SSD_HALO, lc), :]
        y = y + w[3:4] * pad_ref[s, pl.ds(base + SSD_HALO + 1, lc), :]
        return _silu(y + b)

    def prep(c):
        base = pl.multiple_of(c * lc, lc)
        rows = pl.ds(base, lc)
        dt_pre = _dot3_l(dtr_ref[rows, :], rep_ref[...])
        yield
        for s in range(nslab // 2):
            cs = slice(s * LANES, (s + 1) * LANES)
            xa_ref[rows, cs] = conv_chunk(s, base, wx_ref[:, cs], bx_ref[:, cs])
            yield
        dt = _softplus(dt_pre + dtb_ref[...])
        dt_ref[rows, :] = dt
        da = dt * a_neg
        acum_f, acum_b = _dot3_r(tri_l, da), _dot3_r(tri_u, da)
        yield
        for s in range(nslab // 2, nslab):
            cs = slice(s * LANES, (s + 1) * LANES)
            xa_ref[rows, cs] = conv_chunk(s, base, wx_ref[:, cs], bx_ref[:, cs])
            yield
        ba_ref[rows, :] = conv_chunk(nslab, base, wb_ref[...], bb_ref[...])
        yield
        ca_ref[rows, :] = conv_chunk(nslab + 1, base, wc_ref[...], bc_ref[...])
        af_ref[rows, :] = acum_f
        ab_ref[rows, :] = acum_b
        yield

    def prep_body(k, carry):
        for _ in zip(prep(2 * k), prep(2 * k + 1)):
            pass
        return carry

    lax.fori_loop(0, nc // 2, prep_body, 0)

    for d in range(2):
        for k in range(nslab):
            cs = slice(k * LANES, (k + 1) * LANES)
            st_ref[d, :, cs] = s0_ref[d, cs, :].T if has_init else jnp.zeros((N_C, LANES), F32)

    def expand(ea, te, e_ref):
        return _dot(_pack3(jnp.where(first3, ea, te), lane), e_ref[...])

    def fwd_a(c, slot):
        rows = chunk_rows(c)
        acum = af_ref[rows, :]
        a_end = acum[lc - 1:lc, :]
        ea, te = jnp.exp(acum), jnp.exp(a_end - acum) * dt_ref[rows, :]
        yield
        ex_ref[slot] = expand(ea, te, expf_ref)
        yield

    def fwd_b(c, slot):
        rows = chunk_rows(c)
        ea_x, te_x = ex_ref[slot, :, 0:gw], ex_ref[slot, :, gw:2 * gw]
        st = st_ref[0]
        cs_ = _dot(ca_ref[rows, :].astype(BF16), st.astype(BF16))
        bc_t = ba_ref[rows, :].T.astype(BF16)
        yield
        upd = _dot(bc_t, (xa_ref[rows, :] * te_x).astype(BF16))
        yield
        y_ref[rows, :] = cs_ * ea_x
        st_ref[0] = st * ea_x[lc - 1:lc, :] + upd
        yield

    def bwd_a(c, slot):
        rows = chunk_rows(c)
        dt = dt_ref[rows, :]
        acum_b = ab_ref[rows, :]
        acum = jnp.where(fwd_lane, af_ref[rows, :], acum_b)
        at_ref[slot, 0] = (acum - jnp.log(dt)).T
        yield
        abw_ref[slot] = _dot(_pack3(acum, lane), selw_ref[...])
        yield
        a_end = acum_b[0:1, :]
        ex_ref[slot] = expand(jnp.exp(acum_b), jnp.exp(a_end - acum_b) * dt, expb_ref)
        yield
        cb_ref[slot] = lax.dot_general(ca_ref[rows, :].astype(BF16), ba_ref[rows, :].astype(BF16), nt,
                                       preferred_element_type=F32)
        yield

    def bwd_b(c, slot):
        rows = chunk_rows(c)
        ea_x, te_x = ex_ref[slot, :, 0:gw], ex_ref[slot, :, gw:2 * gw]
        cb = cb_ref[slot]
        st = st_ref[1]
        xs = xa_ref[rows, :]
        y_inter = _dot(ca_ref[rows, :].astype(BF16), st.astype(BF16)) * ea_x
        bc_t = ba_ref[rows, :].T.astype(BF16)
        upd = _dot(bc_t, (xs * te_x).astype(BF16))
        yield
        for s in range(nslab):
            cs = slice(s * LANES, (s + 1) * LANES)
            ws = []
            for j in range(2):
                r = 2 * s + j
                rb = R_C + r
                aj_f, aj_b = at_ref[slot, 0, r:r + 1, :], at_ref[slot, 0, rb:rb + 1, :]
                wf = jnp.exp(jnp.where(causal, abw_ref[slot, :, r * LANES:(r + 1) * LANES] - aj_f, -jnp.inf))
                wb = jnp.exp(jnp.where(anti, abw_ref[slot, :, rb * LANES:(rb + 1) * LANES] - aj_b, -jnp.inf))
                ws.append((cb * (wf + wb)).astype(BF16))
            xsl = xs[:, cs]
            x2 = jnp.concatenate([jnp.where(lo, xsl, 0.0), jnp.where(lo, 0.0, xsl)], axis=0).astype(BF16)
            y = y_ref[rows, cs] + y_inter[:, cs] + dsk_ref[:, cs] * xsl + _dot(jnp.concatenate(ws, axis=1), x2)
            y_ref[rows, cs] = y * _silu(z_ref[s, rows, :])
            yield
        st_ref[1] = st * ea_x[0:1, :] + upd
        yield

    def run(*gens):
        for _ in itertools.zip_longest(*gens):
            pass

    def sweep(stage_a, stage_b, order):
        run(stage_a(order(0), 0))

        def body(k, carry):
            run(stage_b(order(2 * k), 0), stage_a(order(2 * k + 1), 1))
            run(stage_b(order(2 * k + 1), 1), stage_a(order(jnp.minimum(2 * k + 2, nc - 1)), 0))
            return carry

        lax.fori_loop(0, nc // 2, body, 0)

    sweep(fwd_a, fwd_b, lambda i: i)
    sweep(bwd_a, bwd_b, lambda i: nc - 1 - i)

    for d in range(2):
        for k in range(nslab):
            cs = slice(k * LANES, (k + 1) * LANES)
            sfin_ref[plane, d, cs, :] = st_ref[d, :, cs].T
    for p in range(sfin_ref.shape[0]):
        if p != plane:
            sfin_ref[p] = jnp.zeros(sfin_ref.shape[1:], F32)


def ssd(proj, conv_w, conv_b, dtb, alog, dsk, s0, layer, n_layers, out_plane, prev_states, nb, seq):
    gw = R_C * P_C
    nslab = gw // LANES
    slab = lambda base: pl.BlockSpec((None, seq, LANES), lambda b, g: (base + g, b, 0))
    slab4 = lambda base: pl.BlockSpec((nslab, seq, LANES), lambda b, g: (base // nslab + g, b, 0))
    xo, bo, co = 0, D_INNER // LANES, (D_INNER + G_C * N_C) // LANES
    cw = lambda rws, width, off: pl.BlockSpec((rws, width), lambda b, g: (0, off * LANES // width + g))
    vec = pl.BlockSpec((None, 1, LANES), lambda b, g: (g, 0, 0))
    full = lambda a: pl.BlockSpec(a.shape, lambda b, g: (0,) * a.ndim)
    st_in = pl.BlockSpec((None, None, 2, gw, N_C), lambda b, g: (b, layer, 0, g, 0))
    rep, selw, expf, expb = _ssd_constants()
    has_init, has_prev = s0 is not None, prev_states is not None
    n_planes, plane = (1, 0) if has_prev else (n_layers, out_plane)
    st_out = pl.BlockSpec((None, n_planes, 2, gw, N_C), lambda b, g: (b, out_plane if has_prev else 0, 0, g, 0))
    st_shape = jax.ShapeDtypeStruct((nb, n_layers, 2, H_C * P_C, N_C), F32)
    vm = lambda *shape: pltpu.VMEM(shape, F32)
    n_in = 18 + has_init
    return pl.pallas_call(
        functools.partial(_ssd_kernel, seq=seq, has_init=has_init, has_prev=has_prev, plane=plane),
        grid=(nb, G_C),
        in_specs=[slab4(OB_X), slab(OB_B), slab(OB_C), slab4(OB_Z),
                  pl.BlockSpec((None, seq, LANES), lambda b, g: (OB_DT, b, 0)),
                  cw(SSD_CONV, gw, xo), cw(SSD_CONV, LANES, bo), cw(SSD_CONV, LANES, co),
                  cw(1, gw, xo), cw(1, LANES, bo), cw(1, LANES, co),
                  vec, vec, pl.BlockSpec((1, gw), lambda b, g: (0, g)),
                  pl.BlockSpec((None, LANES, LANES), lambda b, g: (g, 0, 0)), full(selw), full(expf), full(expb)]
        + ([st_in] if has_init else []) + ([pl.BlockSpec(memory_space=pl.ANY)] if has_prev else []),
        out_specs=[pl.BlockSpec((seq, gw), lambda b, g: (b, g)), st_out],
        out_shape=[jax.ShapeDtypeStruct((nb * seq, D_INNER), F32), st_shape],
        input_output_aliases={n_in: 1} if has_prev else {},
        scratch_shapes=[vm(nslab + 2, seq + 2 * SSD_HALO, LANES), vm(seq, gw), vm(seq, LANES), vm(seq, LANES),
                        vm(seq, LANES), vm(seq, LANES), vm(seq, LANES), vm(2, N_C, gw), vm(2, SSD_CHUNK, 2 * gw),
                        vm(2, SSD_CHUNK, SSD_REP * LANES), vm(2, 2, LANES, SSD_CHUNK), vm(2, SSD_CHUNK, SSD_CHUNK)],
        compiler_params=_cparams(("parallel", "parallel")),
        name="ssd",
    )(proj, proj, proj, proj, proj, conv_w, conv_w, conv_w, conv_b, conv_b, conv_b, dtb, alog, dsk,
      rep, selw, expf, expb, *((s0,) if has_init else ()), *((prev_states,) if has_prev else ()))


def _ssd_constants():
    gw = R_C * P_C
    rep = np.zeros((G_C, LANES, LANES), np.float32)
    for g in range(G_C):
        for l in range(LANES):
            c = l % SSD_REP
            rep[g, (R_C * g + c) if c < R_C else (H_C + R_C * g + c - R_C), l] = 1.0
    selw = np.zeros((LANES, SSD_REP * LANES), np.float32)
    expf = np.zeros((LANES, 2 * gw), np.float32)
    expb = np.zeros((LANES, 2 * gw), np.float32)
    for l in range(3 * SSD_REP):
        c = l % SSD_REP
        selw[l, c * LANES:(c + 1) * LANES] = 1.0
    for l in range(6 * SSD_REP):
        c, q = l % SSD_REP, l // (3 * SSD_REP)
        if c < R_C:
            expf[l, q * gw + c * P_C:q * gw + (c + 1) * P_C] = 1.0
        else:
            expb[l, q * gw + (c - R_C) * P_C:q * gw + (c - R_C + 1) * P_C] = 1.0
    return tuple(jnp.asarray(a, BF16) for a in (rep, selw, expf, expb))


def _pair_layout_c(c):
    nb = c.shape[0]
    c = c.transpose(0, 2, 1, 4, 3).reshape(nb, H_A // 2, 2, 2, DK_A, DV_A)
    return c.transpose(0, 1, 3, 2, 4, 5).reshape(nb, H_A // 2, 2, 2 * DK_A, DV_A)


def _pair_layout_c_inv(c):
    nb = c.shape[0]
    c = c.reshape(nb, H_A // 2, 2, 2, DK_A, DV_A).transpose(0, 2, 1, 3, 5, 4)
    return c.reshape(nb, 2, H_A, DV_A, DK_A)


def _pair_layout_n(n):
    nb = n.shape[0]
    return n.reshape(nb, 2, H_A // 2, 2 * DK_A).transpose(0, 2, 1, 3).reshape(nb, H_A // 2, 2, 1, 2 * DK_A)


def _pair_layout_n_inv(n):
    nb = n.shape[0]
    return n.reshape(nb, H_A // 2, 2, 2 * DK_A).transpose(0, 2, 1, 3).reshape(nb, 2, H_A, DK_A)


def _pair_layout_m(m):
    return _pair_layout_n(jnp.repeat(m[..., None], DK_A, axis=-1))


def _pair_layout_m_inv(m):
    return _pair_layout_n_inv(m)[..., 0]


def _group_replicated(v):
    t = v.reshape(2, G_C, R_C).transpose(1, 0, 2).reshape(G_C, 1, SSD_REP)
    return jnp.tile(t, (1, 1, LANES // SSD_REP))


def kernel(x_prompt, x_sample, cache_attn_k, cache_attn_v, state_mlstm_c, state_mlstm_n, state_mlstm_m, state_ssd,
           c, c_ctx, w_ada, b_ada, norm_mix, norm_ffn, w_gate, w_up, w_down, w_in_even, b_gate_mlstm, mlstm_norm,
           q_norm, k_norm, lambda_q1, lambda_k1, lambda_q2, lambda_k2, diff_norm, w_out_even, w_in_odd, conv_w,
           conv_b, dt_bias, a_log, d_skip, ssd_norm, w_out_odd):
    nb_c, seq_c, d = x_prompt.shape
    nb_d, seq_d, _ = x_sample.shape
    n_ctx, n_dec = nb_c * seq_c, nb_d * seq_d
    depth = w_ada.shape[0]

    xc, xd = x_prompt.reshape(n_ctx, d), x_sample.reshape(n_dec, d)
    cvecs = jnp.concatenate([c_ctx[None, :], c, jnp.zeros((8 - 1 - nb_d, d), F32)], axis=0)
    mods = ada_modulation(cvecs, w_ada, b_ada)

    def mod(l, k):
        return mods[l, :, k * d:(k + 1) * d].reshape(8, 1, d)

    cos, sin = rope_tables(seq_d)
    n_even, n_odd = w_in_even.shape[0], w_in_odd.shape[0]
    gate0 = 2 * A_QK + 2 * A_V
    w_even = jnp.concatenate([w_in_even[:, :, :gate0], w_in_even[:, :, gate0 + 4 * H_A:],
                              w_in_even[:, :, gate0:gate0 + 4 * H_A]], axis=2)
    w_even = jnp.pad(w_even, ((0, 0), (0, 0), (0, E_COLS - w_even.shape[2]))).astype(BF16)
    w_odd = jnp.pad(w_in_odd, ((0, 0), (0, 0), (0, O_COLS - w_in_odd.shape[2]))).astype(BF16)
    wo_even, wo_odd = w_out_even.astype(BF16), w_out_odd.astype(BF16)
    ffn_w = (w_gate.astype(BF16), w_up.astype(BF16), w_down.astype(BF16))

    new_c, new_n, new_m = [], [], []
    new_kv, new_s = None, None
    for l in range(depth):
        i = l // 2
        sh1, sc1, g1, sh2, sc2, g2 = (mod(l, k) for k in range(6))
        if l % 2 == 0:
            gmix = norm_mix[l][None]
            pc = in_projection(xc, gmix, sc1, sh1, w_even, i, E_TN, None)
            pd = in_projection(xd, gmix, sc1, sh1, w_even, i, E_TN, seq_d)
            bias = jnp.pad(b_gate_mlstm[i][None], ((0, 0), (0, LANES - 4 * H_A)))
            gain_a = mlstm_norm[i][None]
            zc = jnp.zeros((nb_c, H_A // 2, 2, LANES, LANES), F32)
            zv = jnp.zeros((nb_c, H_A // 2, 2, 1, LANES), F32)
            h_c, cf, nf, mf = mlstm(pc, bias, gain_a, zc, zv, zv, 0, nb_c, seq_c)
            h_d, _, _, _ = mlstm(pd, bias, gain_a, _pair_layout_c(state_mlstm_c[:, i]),
                                 _pair_layout_n(state_mlstm_n[:, i]), _pair_layout_m(state_mlstm_m[:, i]),
                                 0, nb_d, seq_d)
            new_c.append(_pair_layout_c_inv(cf))
            new_n.append(_pair_layout_n_inv(nf[:, :, :, 0]))
            new_m.append(_pair_layout_m_inv(mf[:, :, :, 0]))

            lam_init = 0.8 - 0.6 * math.exp(-0.3 * l)
            lam_p = jnp.stack([lambda_q1[i], lambda_k1[i], lambda_q2[i], lambda_k2[i]], axis=0)
            qg = jnp.tile(q_norm[i][None], (1, 2))
            kg = jnp.tile(k_norm[i][None], (1, 2))
            dg = diff_norm[i][None]
            o_c, *new_kv = diff_attention_ctx(pc, lam_p, qg, kg, dg, i, n_even, new_kv, nb_c, seq_c, lam_init)
            past = cache_attn_k.shape[2]
            o_d = diff_attention_dec(pd, cache_attn_k[:, i].reshape(nb_d, past, B_QK),
                                     cache_attn_v[:, i].reshape(nb_d, past, B_V), lam_p, qg, kg, dg, cos, sin,
                                     0, nb_d, seq_d, lam_init)
            xc = out_projection_even(xc, g1, h_c, o_c, wo_even, i, None)
            xd = out_projection_even(xd, g1, h_d, o_d, wo_even, i, seq_d)
        else:
            gmix = norm_mix[l][None]
            pc = in_projection(xc, gmix, sc1, sh1, w_odd, i, O_TN, None)
            pd = in_projection(xd, gmix, sc1, sh1, w_odd, i, O_TN, seq_d)
            dtb, alog = _group_replicated(dt_bias[i]), _group_replicated(a_log[i])
            dsk = jnp.repeat(d_skip[i], P_C)[None]
            cw, cb = conv_w[i], conv_b[i][None]
            y_c, new_s = ssd(pc, cw, cb, dtb, alog, dsk, None, i, n_odd, i, new_s, nb_c, seq_c)
            y_d, _ = ssd(pd, cw, cb, dtb, alog, dsk, state_ssd.reshape(nb_d, n_odd, 2, H_C * P_C, N_C), i, 1, 0, None,
                         nb_d, seq_d)
            gs = ssd_norm[i][None]
            xc = out_projection_odd(xc, g1, y_c, gs, wo_odd, i, None)
            xd = out_projection_odd(xd, g1, y_d, gs, wo_odd, i, seq_d)
        xc = ffn(xc, norm_ffn[l][None], sc2, sh2, g2, *ffn_w, l, None)
        xd = ffn(xd, norm_ffn[l][None], sc2, sh2, g2, *ffn_w, l, seq_d)

    y_prompt = xc.reshape(nb_c, seq_c, d)
    y_sample = xd.reshape(nb_d, seq_d, d)
    new_k = new_kv[0].reshape(nb_c, n_even, seq_c, H_B, 2, DB)
    new_v = new_kv[1].reshape(nb_c, n_even, seq_c, H_B, DVB)
    return (y_prompt, y_sample, new_k, new_v, jnp.stack(new_c, axis=1), jnp.stack(new_n, axis=1),
            jnp.stack(new_m, axis=1), new_s.reshape(nb_c, n_odd, 2, H_C, P_C, N_C))
```
